```python
import math
import jax
import jax.numpy as jnp
from jax import lax
import numpy as np

D_MODEL = 1024
BATCH = 2
SEQ = 16384
DEPTH = 2
DEC_BATCH = 16
DEC_SEQ = 2048
PAST_LEN = 128

GRID_W = 64
MIX_W = D_MODEL
W_ATT = MIX_W // 2
W_CONV = MIX_W - W_ATT
W_SSM = MIX_W // 2
W_HYENA = MIX_W - W_SSM
HEAD_DIM = 64
N_HEADS_ATT = W_ATT // HEAD_DIM
WIN_R = 8
WIN_C = 16
COL_BLOCK = 16
KEY_COLS = 2 * COL_BLOCK
CONV_W = 31
S5_GROUP = 16
S5_GROUPS = W_SSM // S5_GROUP
S5_STATE = 64
HY_ORDER = 2
HY_SHORT = 3
HY_BANDS = 8
HY_EMB = 1 + 2 * HY_BANDS
HY_FFN = 64
N_EXPERTS = 16
N_EXPERT_GROUPS = 4
EXPERTS_PER_GROUP = N_EXPERTS // N_EXPERT_GROUPS
TOP_K = 2
D_FF_EXPERT = D_MODEL // 2
N_EVEN = (DEPTH + 1) // 2
N_ODD = DEPTH // 2
EPS = 1e-6
NEG_INF = -1e30

kernel_name = 'hybrid_bidir_natten_conformer_s5_hyena_moe'


def rms_norm(x, g):
    xf = x.astype(jnp.float32)
    y = xf * lax.rsqrt(jnp.mean(xf * xf, axis=-1, keepdims=True) + EPS)
    return (y * g.astype(jnp.float32)).astype(x.dtype)


def layer_norm(x, g, b):
    xf = x.astype(jnp.float32)
    mu = jnp.mean(xf, axis=-1, keepdims=True)
    var = jnp.mean(jnp.square(xf - mu), axis=-1, keepdims=True)
    y = (xf - mu) * lax.rsqrt(var + EPS)
    return (y * g.astype(jnp.float32) + b.astype(jnp.float32)).astype(x.dtype)


def depthwise_conv(x, w, b):
    width = w.shape[0]
    y = lax.conv_general_dilated(
        x, w[:, None, :].astype(x.dtype), window_strides=(1,),
        padding=[(width // 2, width // 2)],
        dimension_numbers=('NWC', 'WIO', 'NWC'),
        feature_group_count=x.shape[-1])
    return y + b.astype(x.dtype)


def na_index_tables(rows):
    kr = min(WIN_R, rows)
    r = np.arange(rows)
    r0 = np.clip(r - kr // 2, 0, rows - kr)
    row_idx = r0[:, None] + np.arange(kr)[None, :]
    rel_r = row_idx - r[:, None] + (WIN_R - 1)
    n_cb = GRID_W // COL_BLOCK
    qcol = np.arange(GRID_W).reshape(n_cb, COL_BLOCK)
    c0 = np.clip(qcol - WIN_C // 2, 0, GRID_W - WIN_C)
    kstart = np.clip(np.arange(n_cb) * COL_BLOCK - WIN_C // 2, 0, GRID_W - KEY_COLS)
    col_idx = kstart[:, None] + np.arange(KEY_COLS)[None, :]
    kc = col_idx[:, None, :]
    valid = (kc >= c0[:, :, None]) & (kc < c0[:, :, None] + WIN_C)
    rel_c = np.clip(kc - qcol[:, :, None], -(WIN_C - 1), WIN_C - 1) + (WIN_C - 1)
    return row_idx, rel_r, col_idx, rel_c, valid


def neighbourhood_attention(q, k, v, q_gain, k_gain, rpb):
    bsz, l, nh, hd = q.shape
    rows = l // GRID_W
    n_cb = GRID_W // COL_BLOCK
    row_idx, rel_r, col_idx, rel_c, valid = na_index_tables(rows)
    q = rms_norm(q, q_gain) * (hd ** -0.5)
    k = rms_norm(k, k_gain)
    qb = q.reshape(bsz, rows, n_cb, COL_BLOCK, nh, hd)
    ri = row_idx[:, None, :, None]
    ci = col_idx[None, :, None, :]
    kb = k.reshape(bsz, rows, GRID_W, nh, hd)[:, ri, ci]
    vb = v.reshape(bsz, rows, GRID_W, nh, hd)[:, ri, ci]
    s = jnp.einsum('brjqhd,brjkwhd->brjhqkw', qb, kb).astype(jnp.float32)
    bias = rpb.astype(jnp.float32)[:, rel_r[:, None, None, :, None], rel_c[None, :, :, None, :]]
    s = s + jnp.moveaxis(bias, 0, 2)[None]
    s = jnp.where(valid[None, None, :, None, :, None, :], s, NEG_INF)
    sh = s.shape
    p = jax.nn.softmax(s.reshape(sh[:-2] + (-1,)), axis=-1).reshape(sh).astype(v.dtype)
    o = jnp.einsum('brjhqkw,brjkwhd->brjqhd', p, vb)
    return o.reshape(bsz, l, nh * hd)


def conv_module(val, gate, dw_w, dw_b, ln_g, ln_b):
    u = val * jax.nn.sigmoid(gate)
    u = depthwise_conv(u, dw_w, dw_b)
    return jax.nn.silu(layer_norm(u, ln_g, ln_b))


def attn_conv_layer(h, w_in, w_out, q_g, k_g, rpb, dw_w, dw_b, ln_g, ln_b):
    bsz, l, _ = h.shape
    proj = h @ w_in
    q, k, v, val, gate = jnp.split(
        proj, [W_ATT, 2 * W_ATT, 3 * W_ATT, 3 * W_ATT + W_CONV], axis=-1)
    heads = lambda t: t.reshape(bsz, l, N_HEADS_ATT, HEAD_DIM)
    att = neighbourhood_attention(heads(q), heads(k), heads(v), q_g, k_g, rpb)
    cnv = conv_module(val, gate, dw_w, dw_b, ln_g, ln_b)
    return jnp.concatenate([att, cnv], axis=-1) @ w_out


def ssm_combine(left, right):
    a1r, a1i, b1r, b1i = left
    a2r, a2i, b2r, b2i = right
    return (a2r * a1r - a2i * a1i,
            a2r * a1i + a2i * a1r,
            a2r * b1r - a2i * b1i + b2r,
            a2r * b1i + a2i * b1r + b2i)


def s5_mixer(u, lam_re, lam_im, log_dt, b_re, b_im, c_re, c_im, d_skip, glu_w):
    f32 = jnp.float32
    bsz, l, _ = u.shape
    uf = u.astype(f32)
    ug = uf.reshape(bsz, l, S5_GROUPS, S5_GROUP)
    y = uf * d_skip.astype(f32)
    for direction in range(2):
        lr = lam_re[direction].astype(f32)
        li = lam_im[direction].astype(f32)
        dt = jnp.exp(log_dt[direction].astype(f32))[:, None]
        mag = jnp.exp(lr * dt)
        ab_re = mag * jnp.cos(li * dt)
        ab_im = mag * jnp.sin(li * dt)
        den = lr * lr + li * li
        f_re = ((ab_re - 1.0) * lr + ab_im * li) / den
        f_im = (ab_im * lr - (ab_re - 1.0) * li) / den
        br = b_re[direction].astype(f32)
        bi = b_im[direction].astype(f32)
        bb_re = f_re[..., None] * br - f_im[..., None] * bi
        bb_im = f_re[..., None] * bi + f_im[..., None] * br
        bu_re = jnp.einsum('blgi,gni->blgn', ug, bb_re)
        bu_im = jnp.einsum('blgi,gni->blgn', ug, bb_im)
        a_re = jnp.broadcast_to(ab_re, bu_re.shape)
        a_im = jnp.broadcast_to(ab_im, bu_im.shape)
        _, _, s_re, s_im = lax.associative_scan(
            ssm_combine, (a_re, a_im, bu_re, bu_im), reverse=(direction == 1), axis=1)
        y_dir = (jnp.einsum('blgn,gon->blgo', s_re, c_re[direction].astype(f32))
                 - jnp.einsum('blgn,gon->blgo', s_im, c_im[direction].astype(f32)))
        y = y + y_dir.reshape(bsz, l, W_SSM)
    y = jax.nn.gelu(y)
    return (y * jax.nn.sigmoid(y @ glu_w.astype(f32))).astype(u.dtype)


def hyena_filters(l, w1, b1, w2, b2, w3, freq, decay):
    f32 = jnp.float32
    pos = jnp.arange(l, dtype=f32)
    t = pos / (l - 1)
    bands = jnp.linspace(1e-4, HY_BANDS - 1, HY_BANDS, dtype=f32)
    ang = (2.0 * math.pi / l) * pos[:, None] * bands[None, :]
    feat = jnp.concatenate([t[:, None], jnp.cos(ang), -jnp.sin(ang)], axis=-1)
    fr = freq.astype(f32)
    hdn = jnp.sin(fr * (feat @ w1.astype(f32) + b1.astype(f32)))
    hdn = jnp.sin(fr * (hdn @ w2.astype(f32) + b2.astype(f32)))
    filt = (hdn @ w3.astype(f32)).reshape(l, HY_ORDER, 2, W_HYENA)
    filt = filt * jnp.exp(-t[:, None, None, None] * jnp.abs(decay.astype(f32)))
    k = jnp.concatenate([filt[:, :, 0],
                         jnp.zeros((1, HY_ORDER, W_HYENA), f32),
                         filt[:0:-1, :, 1]], axis=0)
    return k / jnp.sum(jnp.abs(k), axis=0, keepdims=True)


def hyena_mixer(p, short_w, short_b, w1, b1, w2, b2, w3, freq, decay, bias):
    bsz, l, _ = p.shape
    p = depthwise_conv(p, short_w, short_b)
    v, x1, x2 = jnp.split(p.astype(jnp.float32), 3, axis=-1)
    kf = jnp.fft.rfft(hyena_filters(l, w1, b1, w2, b2, w3, freq, decay), axis=0)
    z = v
    for n, gate in enumerate((x1, x2)):
        zf = jnp.fft.rfft(z, n=2 * l, axis=1)
        conv = jnp.fft.irfft(zf * kf[None, :, n], n=2 * l, axis=1)[:, :l]
        z = gate * (conv + z * bias[n].astype(jnp.float32))
    return z.astype(p.dtype)


def ssm_hyena_layer(h, w_in, w_out, lam_re, lam_im, log_dt, b_re, b_im, c_re, c_im, d_skip,
                    glu_w, short_w, short_b, w1, b1, w2, b2, w3, freq, decay, bias):
    proj = h @ w_in
    u, hz = jnp.split(proj, [W_SSM], axis=-1)
    ssm = s5_mixer(u, lam_re, lam_im, log_dt, b_re, b_im, c_re, c_im, d_skip, glu_w)
    hy = hyena_mixer(hz, short_w, short_b, w1, b1, w2, b2, w3, freq, decay, bias)
    return jnp.concatenate([ssm, hy], axis=-1) @ w_out


def moe_ffn(h, router_w, router_b, w_gate, w_up, w_down):
    scores = jax.nn.sigmoid(h.astype(jnp.float32) @ router_w.astype(jnp.float32))
    biased = scores + router_b.astype(jnp.float32)
    grouped = biased.reshape(scores.shape[:-1] + (N_EXPERT_GROUPS, EXPERTS_PER_GROUP))
    group_score = lax.top_k(grouped, 2)[0].sum(-1)
    best_group = jnp.argmax(group_score, axis=-1)
    in_group = (jnp.arange(N_EXPERTS) // EXPERTS_PER_GROUP) == best_group[..., None]
    _, top_idx = lax.top_k(jnp.where(in_group, biased, -jnp.inf), TOP_K)
    chosen = jax.nn.one_hot(top_idx, N_EXPERTS, dtype=jnp.float32).sum(-2)
    gate = scores * chosen
    gate = gate / gate.sum(-1, keepdims=True)
    hid = jax.nn.silu(jnp.einsum('bld,edf->blef', h, w_gate)) * jnp.einsum('bld,edf->blef', h, w_up)
    hid = hid * gate.astype(h.dtype)[..., None]
    return jnp.einsum('blef,efd->bld', hid, w_down)


def trunk(x, c, weights):
    (ada_w, ada_b, norm_mix_g, norm_ffn_g, router_w, router_b, moe_w_gate, moe_w_up, moe_w_down,
     ab_w_in, ab_w_out, na_q_g, na_k_g, na_rpb, cv_dw_w, cv_dw_b, cv_ln_g, cv_ln_b,
     cd_w_in, cd_w_out, s5_lam_re, s5_lam_im, s5_log_dt, s5_b_re, s5_b_im, s5_c_re, s5_c_im,
     s5_d, s5_glu_w, hy_short_w, hy_short_b, hy_w1, hy_b1, hy_w2, hy_b2, hy_w3, hy_freq,
     hy_decay, hy_bias) = weights
    for i in range(DEPTH):
        j = i // 2
        mod = (jax.nn.silu(c) @ ada_w[i] + ada_b[i])[:, None, :]
        sh_m, sc_m, g_m, sh_f, sc_f, g_f = jnp.split(mod, 6, axis=-1)
        h = rms_norm(x, norm_mix_g[i]) * (1.0 + sc_m) + sh_m
        if i % 2 == 0:
            mix = attn_conv_layer(h, ab_w_in[j], ab_w_out[j], na_q_g[j], na_k_g[j], na_rpb[j],
                                  cv_dw_w[j], cv_dw_b[j], cv_ln_g[j], cv_ln_b[j])
        else:
            mix = ssm_hyena_layer(h, cd_w_in[j], cd_w_out[j], s5_lam_re[j], s5_lam_im[j],
                                  s5_log_dt[j], s5_b_re[j], s5_b_im[j], s5_c_re[j], s5_c_im[j],
                                  s5_d[j], s5_glu_w[j], hy_short_w[j], hy_short_b[j],
                                  hy_w1[j], hy_b1[j], hy_w2[j], hy_b2[j], hy_w3[j],
                                  hy_freq[j], hy_decay[j], hy_bias[j])
        x = x + g_m * mix
        h = rms_norm(x, norm_ffn_g[i]) * (1.0 + sc_f) + sh_f
        x = x + g_f * moe_ffn(h, router_w, router_b, moe_w_gate[i], moe_w_up[i], moe_w_down[i])
    return x


def setup_inputs(seed: int = 0) -> dict:
    key = jax.random.key(seed)
    ks = iter(jax.random.split(key, 48))
    f32 = jnp.float32
    D = D_MODEL

    def nrm(shape, scale):
        return scale * jax.random.normal(next(ks), shape, f32)

    inp = {}
    inp['x_prompt'] = nrm((BATCH, SEQ, D), 1.0)
    inp['x_sample'] = nrm((DEC_BATCH, DEC_SEQ, D), 1.0)
    inp['c_prompt'] = nrm((BATCH, D), 1.0)
    inp['c_sample'] = nrm((DEC_BATCH, D), 1.0)
    inp['ada_w'] = nrm((DEPTH, D, 6 * D), 0.5 * D ** -0.5)
    inp['ada_b'] = nrm((DEPTH, 6 * D), 0.02)
    inp['norm_mix_g'] = 1.0 + nrm((DEPTH, D), 0.02)
    inp['norm_ffn_g'] = 1.0 + nrm((DEPTH, D), 0.02)
    inp['router_w'] = nrm((D, N_EXPERTS), D ** -0.5)
    inp['router_b'] = nrm((N_EXPERTS,), 0.01)
    inp['moe_w_gate'] = nrm((DEPTH, N_EXPERTS, D, D_FF_EXPERT), D ** -0.5)
    inp['moe_w_up'] = nrm((DEPTH, N_EXPERTS, D, D_FF_EXPERT), D ** -0.5)
    inp['moe_w_down'] = nrm((DEPTH, N_EXPERTS, D_FF_EXPERT, D), D_FF_EXPERT ** -0.5)
    inp['ab_w_in'] = nrm((N_EVEN, D, 3 * W_ATT + 2 * W_CONV), D ** -0.5)
    inp['ab_w_out'] = nrm((N_EVEN, MIX_W, D), MIX_W ** -0.5)
    inp['na_q_g'] = 1.0 + nrm((N_EVEN, HEAD_DIM), 0.02)
    inp['na_k_g'] = 1.0 + nrm((N_EVEN, HEAD_DIM), 0.02)
    inp['na_rpb'] = nrm((N_EVEN, N_HEADS_ATT, 2 * WIN_R - 1, 2 * WIN_C - 1), 0.1)
    inp['cv_dw_w'] = nrm((N_EVEN, CONV_W, W_CONV), CONV_W ** -0.5)
    inp['cv_dw_b'] = nrm((N_EVEN, W_CONV), 0.02)
    inp['cv_ln_g'] = 1.0 + nrm((N_EVEN, W_CONV), 0.02)
    inp['cv_ln_b'] = nrm((N_EVEN, W_CONV), 0.02)
    inp['cd_w_in'] = nrm((N_ODD, D, W_SSM + 3 * W_HYENA), D ** -0.5)
    inp['cd_w_out'] = nrm((N_ODD, MIX_W, D), MIX_W ** -0.5)
    inp['s5_lam_re'] = -0.5 + nrm((N_ODD, 2, S5_GROUPS, S5_STATE), 0.01)
    inp['s5_lam_im'] = (math.pi * jnp.arange(S5_STATE, dtype=f32)
                        + nrm((N_ODD, 2, S5_GROUPS, S5_STATE), 0.01))
    inp['s5_log_dt'] = jax.random.uniform(next(ks), (N_ODD, 2, S5_GROUPS), f32,
                                          minval=math.log(1e-3), maxval=math.log(1e-1))
    inp['s5_b_re'] = nrm((N_ODD, 2, S5_GROUPS, S5_STATE, S5_GROUP), (2 * S5_GROUP) ** -0.5)
    inp['s5_b_im'] = nrm((N_ODD, 2, S5_GROUPS, S5_STATE, S5_GROUP), (2 * S5_GROUP) ** -0.5)
    inp['s5_c_re'] = nrm((N_ODD, 2, S5_GROUPS, S5_GROUP, S5_STATE), S5_STATE ** -0.5)
    inp['s5_c_im'] = nrm((N_ODD, 2, S5_GROUPS, S5_GROUP, S5_STATE), S5_STATE ** -0.5)
    inp['s5_d'] = nrm((N_ODD, W_SSM), 1.0)
    inp['s5_glu_w'] = nrm((N_ODD, W_SSM, W_SSM), W_SSM ** -0.5)
    inp['hy_short_w'] = nrm((N_ODD, HY_SHORT, 3 * W_HYENA), HY_SHORT ** -0.5)
    inp['hy_short_b'] = nrm((N_ODD, 3 * W_HYENA), 0.02)
    inp['hy_w1'] = nrm((N_ODD, HY_EMB, HY_FFN), HY_EMB ** -0.5)
    inp['hy_b1'] = nrm((N_ODD, HY_FFN), 0.1)
    inp['hy_w2'] = nrm((N_ODD, HY_FFN, HY_FFN), HY_FFN ** -0.5)
    inp['hy_b2'] = nrm((N_ODD, HY_FFN), 0.1)
    inp['hy_w3'] = nrm((N_ODD, HY_FFN, HY_ORDER * 2 * W_HYENA), HY_FFN ** -0.5)
    inp['hy_freq'] = 1.0 + nrm((N_ODD, HY_FFN), 0.1)
    decay0 = jnp.linspace(math.log(1e-2) / 1.5, math.log(1e-2) / 0.3, W_HYENA, dtype=f32)
    inp['hy_decay'] = decay0 + nrm((N_ODD, HY_ORDER, 2, W_HYENA), 0.1)
    inp['hy_bias'] = nrm((N_ODD, HY_ORDER, W_HYENA), 1.0)
    return inp


def reference(x_prompt, x_sample, c_prompt, c_sample, ada_w, ada_b, norm_mix_g, norm_ffn_g,
              router_w, router_b, moe_w_gate, moe_w_up, moe_w_down, ab_w_in, ab_w_out,
              na_q_g, na_k_g, na_rpb, cv_dw_w, cv_dw_b, cv_ln_g, cv_ln_b, cd_w_in, cd_w_out,
              s5_lam_re, s5_lam_im, s5_log_dt, s5_b_re, s5_b_im, s5_c_re, s5_c_im, s5_d,
              s5_glu_w, hy_short_w, hy_short_b, hy_w1, hy_b1, hy_w2, hy_b2, hy_w3, hy_freq,
              hy_decay, hy_bias):
    weights = (ada_w, ada_b, norm_mix_g, norm_ffn_g, router_w, router_b, moe_w_gate, moe_w_up,
               moe_w_down, ab_w_in, ab_w_out, na_q_g, na_k_g, na_rpb, cv_dw_w, cv_dw_b,
               cv_ln_g, cv_ln_b, cd_w_in, cd_w_out, s5_lam_re, s5_lam_im, s5_log_dt, s5_b_re,
               s5_b_im, s5_c_re, s5_c_im, s5_d, s5_glu_w, hy_short_w, hy_short_b, hy_w1,
               hy_b1, hy_w2, hy_b2, hy_w3, hy_freq, hy_decay, hy_bias)
    y_prompt = trunk(x_prompt, c_prompt, weights)
    y_sample = trunk(x_sample, c_sample, weights)
    return (y_prompt, y_sample)
```

```python
import functools
import math

import numpy as np
import jax
import jax.numpy as jnp
from jax import lax
from jax.experimental import pallas as pl
from jax.experimental.pallas import tpu as pltpu

F32 = jnp.float32
BF16 = jnp.bfloat16
HIGHEST = lax.Precision.HIGHEST

D_MODEL = 1024
DEPTH = 2
GRID_W = 64
W_ATT = 512
W_CONV = 512
W_SSM = 512
W_HYENA = 512
HEAD_DIM = 64
N_HEADS = 8
WIN_R = 8
WIN_C = 16
CONV_W = 31
S5_GROUP = 16
S5_GROUPS = 32
S5_STATE = 64
HY_BANDS = 8
N_EXPERTS = 16
N_GROUPS = 4
D_FF = 512
EPS = 1e-6
NEG_INF = -1e30

VMEM_LIMIT_BYTES = 56 * 1024 * 1024
LANES = 128

ATT_QROWS = 4
ATT_KROWS = 12
S5_CHUNK = LANES
FFT_N2 = 128


def _cparams(sem):
    return pltpu.CompilerParams(dimension_semantics=sem, vmem_limit_bytes=VMEM_LIMIT_BYTES)


def _mod_kernel(c_ref, w_ref, b_ref, o_ref):
    c = c_ref[...]
    s = c * jax.nn.sigmoid(c)
    o_ref[0] = jnp.dot(s, w_ref[0], precision=HIGHEST, preferred_element_type=F32) + b_ref[0]


def adaln_mod(c, ada_w, ada_b):
    bsz = c.shape[0]
    tn = D_MODEL
    out = pl.pallas_call(
        _mod_kernel,
        grid=(DEPTH, 6 * D_MODEL // tn),
        in_specs=[
            pl.BlockSpec((bsz, D_MODEL), lambda i, j: (0, 0)),
            pl.BlockSpec((1, D_MODEL, tn), lambda i, j: (i, 0, j)),
            pl.BlockSpec((1, 1, tn), lambda i, j: (i, 0, j)),
        ],
        out_specs=pl.BlockSpec((1, bsz, tn), lambda i, j: (i, 0, j)),
        out_shape=jax.ShapeDtypeStruct((DEPTH, bsz, 6 * D_MODEL), F32),
        compiler_params=_cparams(("arbitrary", "arbitrary")),
        name="adaln_mod",
    )(c, ada_w, ada_b.reshape(DEPTH, 1, 6 * D_MODEL))
    return out.reshape(DEPTH, bsz, 6, D_MODEL)


def _norm_mod(x, g, shift, scale):
    ms = jnp.mean(x * x, axis=-1, keepdims=True)
    return x * lax.rsqrt(ms + EPS) * g * (1.0 + scale) + shift


def _in0_kernel(x_ref, mod_ref, g_ref, w_ref, hm_ref, qg_ref, kg_ref, q_o, k_o, v_o, u_o):
    m = mod_ref[0]
    h = _norm_mod(x_ref[...], g_ref[...], m[0:1], m[1:2])
    p = jnp.dot(h.astype(BF16), w_ref[...], preferred_element_type=F32)
    q = p[:, 0:W_ATT]
    k = p[:, W_ATT:2 * W_ATT]
    qms = jnp.dot((q * q).astype(BF16), hm_ref[...], preferred_element_type=F32)
    kms = jnp.dot((k * k).astype(BF16), hm_ref[...], preferred_element_type=F32)
    q_o[...] = (q * lax.rsqrt(qms + EPS) * qg_ref[...]).astype(BF16)
    k_o[...] = (k * lax.rsqrt(kms + EPS) * kg_ref[...]).astype(BF16)
    v_o[...] = p[:, 2 * W_ATT:3 * W_ATT].astype(BF16)
    val = p[:, 3 * W_ATT:3 * W_ATT + W_CONV]
    gate = p[:, 3 * W_ATT + W_CONV:]
    u_o[...] = (val * jax.nn.sigmoid(gate)).astype(BF16)


def in_proj0(x, mod, g, w_in_bf, q_gain, k_gain, seq, tm=512):
    n = x.shape[0]
    head_mean = jnp.asarray(np.kron(np.eye(N_HEADS), np.full((HEAD_DIM, HEAD_DIM), 1.0 / HEAD_DIM)), BF16)
    qg = (jnp.tile(q_gain.astype(F32), N_HEADS) * (HEAD_DIM ** -0.5)).reshape(1, W_ATT)
    kg = jnp.tile(k_gain.astype(F32), N_HEADS).reshape(1, W_ATT)
    tok = lambda i: (i, 0)
    fixed = lambda i: (0, 0)
    osd = jax.ShapeDtypeStruct((n, W_ATT), BF16)
    return pl.pallas_call(
        _in0_kernel,
        grid=(n // tm,),
        in_specs=[
            pl.BlockSpec((tm, D_MODEL), tok),
            pl.BlockSpec((1, 6, D_MODEL), lambda i: ((i * tm) // seq, 0, 0)),
            pl.BlockSpec((1, D_MODEL), fixed),
            pl.BlockSpec(w_in_bf.shape, fixed),
            pl.BlockSpec((W_ATT, W_ATT), fixed),
            pl.BlockSpec((1, W_ATT), fixed),
            pl.BlockSpec((1, W_ATT), fixed),
        ],
        out_specs=[pl.BlockSpec((tm, W_ATT), tok)] * 4,
        out_shape=[osd] * 4,
        compiler_params=_cparams(("parallel",)),
        name="in_proj0",
    )(x, mod, g.reshape(1, D_MODEL), w_in_bf, head_mean, qg, kg)


def _att_bias_table(rpb):
    a = np.arange(ATT_QROWS)[:, None, None, None]
    c = np.arange(GRID_W)[None, :, None, None]
    e = np.arange(ATT_KROWS)[None, None, :, None]
    kc = np.arange(GRID_W)[None, None, None, :]
    c0 = np.clip(c - WIN_C // 2, 0, GRID_W - WIN_C)
    col_ok = (kc >= c0) & (kc < c0 + WIN_C)
    dc = np.clip(kc - c, -(WIN_C - 1), WIN_C - 1) + (WIN_C - 1)
    tables = []
    for case in range(3):
        if case == 0:
            dr = e - a
            row_ok = (e >= 0) & (e < WIN_R)
        elif case == 1:
            dr = e - a - WIN_R // 2
            row_ok = (dr >= -(WIN_R // 2)) & (dr < WIN_R // 2)
        else:
            dr = e - a - (ATT_KROWS - ATT_QROWS)
            row_ok = (e >= ATT_KROWS - WIN_R) & (e < ATT_KROWS)
        ok = np.broadcast_to(row_ok & col_ok, (ATT_QROWS, GRID_W, ATT_KROWS, GRID_W))
        dri = np.broadcast_to(np.clip(dr + WIN_R - 1, 0, 2 * WIN_R - 2), ok.shape)
        dci = np.broadcast_to(dc, ok.shape)
        nq, nk = ATT_QROWS * GRID_W, ATT_KROWS * GRID_W
        t = rpb.astype(F32)[:, dri.reshape(nq, nk), dci.reshape(nq, nk)]
        tables.append(jnp.where(jnp.asarray(ok.reshape(nq, nk)), t, NEG_INF))
    return jnp.stack(tables).astype(BF16)


def _att_kernel(q_ref, k0, k1, k2, v0, v1, v2, bias_ref, o_ref):
    kt = [k0, k1, k2]
    vt = [v0, v1, v2]
    nkb = len(kt)
    kw = k0.shape[0]
    tq = q_ref.shape[0]
    first = lax.broadcasted_iota(jnp.int32, (tq, LANES), 1) < HEAD_DIM
    for hp in range(N_HEADS // 2):
        ps = slice(hp * LANES, (hp + 1) * LANES)
        qp = q_ref[:, ps]
        res = []
        for sub in range(2):
            h = 2 * hp + sub
            qm = jnp.where(first if sub == 0 else jnp.logical_not(first), qp, jnp.zeros_like(qp))
            s = [lax.dot_general(qm, kt[j][:, ps], (((1,), (1,)), ((), ())), preferred_element_type=F32)
                 + bias_ref[0, h, :, j * kw:(j + 1) * kw].astype(F32) for j in range(nkb)]
            m = s[0].max(axis=-1, keepdims=True)
            for j in range(1, nkb):
                m = jnp.maximum(m, s[j].max(axis=-1, keepdims=True))
            p = [jnp.exp(sj - m) for sj in s]
            l = p[0].sum(axis=-1, keepdims=True)
            for j in range(1, nkb):
                l = l + p[j].sum(axis=-1, keepdims=True)
            o = jnp.dot(p[0].astype(BF16), vt[0][:, ps], preferred_element_type=F32)
            for j in range(1, nkb):
                o = o + jnp.dot(p[j].astype(BF16), vt[j][:, ps], preferred_element_type=F32)
            res.append(o / l)
        o_ref[:, ps] = jnp.where(first, res[0], res[1]).astype(BF16)


def neighbourhood_attention(q, k, v, bias, bsz, seq):
    rows = seq // GRID_W
    nqb = rows // ATT_QROWS
    tq = ATT_QROWS * GRID_W
    nkb = ATT_KROWS // ATT_QROWS
    assert rows % ATT_QROWS == 0 and nqb >= nkb

    def kmap(j):
        def f(i):
            b, r = i // nqb, i % nqb
            return (b * nqb + jnp.clip(r - 1, 0, nqb - nkb) + j, 0)
        return f

    def bias_map(i):
        r = i % nqb
        return (jnp.where(r == 0, 0, jnp.where(r == nqb - 1, 2, 1)), 0, 0, 0)

    kv_specs = [pl.BlockSpec((tq, W_ATT), kmap(j)) for j in range(nkb)]
    return pl.pallas_call(
        _att_kernel,
        grid=(bsz * nqb,),
        in_specs=[pl.BlockSpec((tq, W_ATT), lambda i: (i, 0))] + kv_specs + kv_specs
        + [pl.BlockSpec((1,) + bias.shape[1:], bias_map)],
        out_specs=pl.BlockSpec((tq, W_ATT), lambda i: (i, 0)),
        out_shape=jax.ShapeDtypeStruct(q.shape, BF16),
        compiler_params=_cparams(("parallel",)),
        name="neighbourhood_attention",
    )(q, k, k, k, v, v, v, bias)


CONV_HALO = 16


def _conv_kernel(prev_ref, cur_ref, next_ref, w_ref, b_ref, g_ref, be_ref, o_ref, win_ref, *, tiles_per_seq):
    i = pl.program_id(0)
    t = i % tiles_per_seq
    tt = cur_ref.shape[0]
    half = CONV_W // 2
    prev = prev_ref[...].astype(F32)
    nxt = next_ref[...].astype(F32)
    win_ref[0:CONV_HALO, :] = jnp.where(t == 0, 0.0, prev)
    win_ref[CONV_HALO:CONV_HALO + tt, :] = cur_ref[...].astype(F32)
    win_ref[CONV_HALO + tt:, :] = jnp.where(t == tiles_per_seq - 1, 0.0, nxt)
    w = w_ref[...]
    acc = jnp.zeros((tt, W_CONV), F32) + b_ref[...]
    for kk in range(CONV_W):
        off = CONV_HALO - half + kk
        acc = acc + win_ref[off:off + tt, :] * w[kk:kk + 1, :]
    mu = jnp.mean(acc, axis=-1, keepdims=True)
    d = acc - mu
    var = jnp.mean(d * d, axis=-1, keepdims=True)
    y = d * lax.rsqrt(var + EPS) * g_ref[...] + be_ref[...]
    o_ref[...] = (y * jax.nn.sigmoid(y)).astype(BF16)


def conv_module(u, dw_w, dw_b, ln_g, ln_b, seq, tt=512):
    n = u.shape[0]
    tps = seq // tt
    hb = tt // CONV_HALO
    nhb = n // CONV_HALO
    row = lambda a: a.astype(F32).reshape(1, W_CONV)
    fixed = lambda i: (0, 0)
    return pl.pallas_call(
        functools.partial(_conv_kernel, tiles_per_seq=tps),
        grid=(n // tt,),
        in_specs=[
            pl.BlockSpec((CONV_HALO, W_CONV), lambda i: (jnp.maximum(i * hb - 1, 0), 0)),
            pl.BlockSpec((tt, W_CONV), lambda i: (i, 0)),
            pl.BlockSpec((CONV_HALO, W_CONV), lambda i: (jnp.minimum((i + 1) * hb, nhb - 1), 0)),
            pl.BlockSpec((CONV_W, W_CONV), fixed),
            pl.BlockSpec((1, W_CONV), fixed),
            pl.BlockSpec((1, W_CONV), fixed),
            pl.BlockSpec((1, W_CONV), fixed),
        ],
        out_specs=pl.BlockSpec((tt, W_CONV), lambda i: (i, 0)),
        out_shape=jax.ShapeDtypeStruct((n, W_CONV), BF16),
        scratch_shapes=[pltpu.VMEM((tt + 2 * CONV_HALO, W_CONV), F32)],
        compiler_params=_cparams(("parallel",)),
        name="conv_module",
    )(u, u, u, dw_w.astype(F32), row(dw_b), row(ln_g), row(ln_b))


def _top2_sum(a, b, c, d):
    hi1, lo1 = jnp.maximum(a, b), jnp.minimum(a, b)
    hi2, lo2 = jnp.maximum(c, d), jnp.minimum(c, d)
    return jnp.maximum(hi1, hi2) + jnp.maximum(jnp.minimum(hi1, hi2), jnp.maximum(lo1, lo2))


def _router_gates(scores_t, bias_t):
    per = N_EXPERTS // N_GROUPS
    rows = [scores_t[e:e + 1, :] + bias_t[e:e + 1, :] for e in range(N_EXPERTS)]
    gscore = [_top2_sum(*rows[g * per:(g + 1) * per]) for g in range(N_GROUPS)]
    best = gscore[0]
    best_idx = jnp.zeros_like(best, dtype=jnp.int32)
    for g in range(1, N_GROUPS):
        better = gscore[g] > best
        best = jnp.where(better, gscore[g], best)
        best_idx = jnp.where(better, g, best_idx)
    out_row = lax.broadcasted_iota(jnp.int32, (LANES, scores_t.shape[1]), 0)
    gates = jnp.zeros((LANES, scores_t.shape[1]), F32)
    total = jnp.zeros_like(best)
    for e in range(N_EXPERTS):
        g = e // per
        rank = jnp.zeros_like(best_idx)
        for e2 in range(g * per, (g + 1) * per):
            if e2 == e:
                continue
            if e2 < e:
                ahead = rows[e2] >= rows[e]
            else:
                ahead = rows[e2] > rows[e]
            rank = rank + jnp.where(ahead, 1, 0)
        chosen = jnp.where(best_idx == g, rank, 2) < 2
        gated = jnp.where(chosen, scores_t[e:e + 1, :], 0.0)
        total = total + gated
        gates = jnp.where(out_row == e, gated, gates)
    return gates / total


def _out_kernel(a_ref, b_ref, x_ref, mod_ref, wa_ref, wb_ref, g_ref, rw_ref, rb_ref, x_o, h_o, gate_o):
    m = mod_ref[0]
    mix = (jnp.dot(a_ref[...], wa_ref[...], preferred_element_type=F32)
           + jnp.dot(b_ref[...], wb_ref[...], preferred_element_type=F32))
    x = x_ref[...] + m[2:3] * mix
    x_o[...] = x
    h = _norm_mod(x, g_ref[...], m[3:4], m[4:5])
    h_o[...] = h.astype(BF16)
    logits = jnp.dot(h, rw_ref[...], precision=HIGHEST, preferred_element_type=F32)
    scores_t = jax.nn.sigmoid(logits).T
    gate_o[...] = _router_gates(scores_t, rb_ref[...]).T


def out_proj_router(a, b, x, mod, w_out_bf, g_ffn, router_w, router_b, seq, tm=512):
    n = x.shape[0]
    half = a.shape[1]
    tok = lambda i: (i, 0)
    fixed = lambda i: (0, 0)
    return pl.pallas_call(
        _out_kernel,
        grid=(n // tm,),
        in_specs=[
            pl.BlockSpec((tm, half), tok),
            pl.BlockSpec((tm, half), tok),
            pl.BlockSpec((tm, D_MODEL), tok),
            pl.BlockSpec((1, 6, D_MODEL), lambda i: ((i * tm) // seq, 0, 0)),
            pl.BlockSpec((half, D_MODEL), lambda i: (0, 0)),
            pl.BlockSpec((half, D_MODEL), lambda i: (1, 0)),
            pl.BlockSpec((1, D_MODEL), fixed),
            pl.BlockSpec((D_MODEL, LANES), fixed),
            pl.BlockSpec((N_EXPERTS, 1), fixed),
        ],
        out_specs=[pl.BlockSpec((tm, D_MODEL), tok), pl.BlockSpec((tm, D_MODEL), tok),
                   pl.BlockSpec((tm, LANES), tok)],
        out_shape=[jax.ShapeDtypeStruct((n, D_MODEL), F32), jax.ShapeDtypeStruct((n, D_MODEL), BF16),
                   jax.ShapeDtypeStruct((n, LANES), F32)],
        compiler_params=_cparams(("parallel",)),
        name="out_proj_router",
    )(a, b, x, mod, w_out_bf, w_out_bf, g_ffn.reshape(1, D_MODEL),
      jnp.pad(router_w.astype(F32), ((0, 0), (0, LANES - N_EXPERTS))),
      router_b.astype(F32).reshape(N_EXPERTS, 1))


def _moe_kernel(h_ref, gate_ref, x_ref, mod_ref, wg_ref, wu_ref, wd_ref, o_ref, acc_ref):
    e = pl.program_id(1)

    @pl.when(e == 0)
    def _():
        acc_ref[...] = jnp.zeros_like(acc_ref)

    h = h_ref[...]
    a = jnp.dot(h, wg_ref[0], preferred_element_type=F32)
    u = jnp.dot(h, wu_ref[0], preferred_element_type=F32)
    gate = gate_ref[...]
    g_hi = gate.astype(BF16)
    g_lo = (gate - g_hi.astype(F32)).astype(BF16)
    sel_row = lax.broadcasted_iota(jnp.int32, (2 * LANES, D_FF), 0)
    sel = jnp.where((sel_row % LANES) == e, 1.0, 0.0).astype(BF16)
    gb = jnp.dot(jnp.concatenate([g_hi, g_lo], axis=1), sel, preferred_element_type=F32)
    hid = (a * jax.nn.sigmoid(a)) * u * gb
    acc_ref[...] += jnp.dot(hid.astype(BF16), wd_ref[0], preferred_element_type=F32)

    @pl.when(e == N_EXPERTS - 1)
    def _():
        o_ref[...] = x_ref[...] + mod_ref[0][5:6] * acc_ref[...]


def moe_ffn(h, gate, x, mod, wg_bf, wu_bf, wd_bf, seq, tm=1024):
    n = x.shape[0]
    tm = min(tm, seq)
    tok = lambda i, e: (i, 0)
    return pl.pallas_call(
        _moe_kernel,
        grid=(n // tm, N_EXPERTS),
        in_specs=[
            pl.BlockSpec((tm, D_MODEL), tok),
            pl.BlockSpec((tm, LANES), tok),
            pl.BlockSpec((tm, D_MODEL), tok),
            pl.BlockSpec((1, 6, D_MODEL), lambda i, e: ((i * tm) // seq, 0, 0)),
            pl.BlockSpec((1, D_MODEL, D_FF), lambda i, e: (e, 0, 0)),
            pl.BlockSpec((1, D_MODEL, D_FF), lambda i, e: (e, 0, 0)),
            pl.BlockSpec((1, D_FF, D_MODEL), lambda i, e: (e, 0, 0)),
        ],
        out_specs=pl.BlockSpec((tm, D_MODEL), tok),
        out_shape=jax.ShapeDtypeStruct((n, D_MODEL), F32),
        scratch_shapes=[pltpu.VMEM((tm, D_MODEL), F32)],
        compiler_params=_cparams(("parallel", "arbitrary")),
        name="moe_ffn",
    )(h, gate, x, mod, wg_bf, wu_bf, wd_bf)


def _in1_kernel(x_ref, mod_ref, g_ref, wut_ref, wz_ref, ut_o, hz_o):
    m = mod_ref[0]
    h = _norm_mod(x_ref[...], g_ref[...], m[0:1], m[1:2]).astype(BF16)
    ut_o[...] = lax.dot_general(wut_ref[...], h, (((1,), (1,)), ((), ())),
                                preferred_element_type=F32).astype(BF16)
    hz_o[...] = jnp.dot(h, wz_ref[...], preferred_element_type=F32).astype(BF16)


def in_proj1(x, mod, g, w_u_t_bf, w_z_bf, seq, tm=512):
    n = x.shape[0]
    fixed = lambda i: (0, 0)
    return pl.pallas_call(
        _in1_kernel,
        grid=(n // tm,),
        in_specs=[
            pl.BlockSpec((tm, D_MODEL), lambda i: (i, 0)),
            pl.BlockSpec((1, 6, D_MODEL), lambda i: ((i * tm) // seq, 0, 0)),
            pl.BlockSpec((1, D_MODEL), fixed),
            pl.BlockSpec(w_u_t_bf.shape, fixed),
            pl.BlockSpec(w_z_bf.shape, fixed),
        ],
        out_specs=[pl.BlockSpec((W_SSM, tm), lambda i: (0, i)),
                   pl.BlockSpec((tm, 3 * W_HYENA), lambda i: (i, 0))],
        out_shape=[jax.ShapeDtypeStruct((W_SSM, n), BF16), jax.ShapeDtypeStruct((n, 3 * W_HYENA), BF16)],
        compiler_params=_cparams(("parallel",)),
        name="in_proj1",
    )(x, mod, g.reshape(1, D_MODEL), w_u_t_bf, w_z_bf)


def s5_tables(lam_re, lam_im, log_dt, b_re, b_im, c_re, c_im, d_skip):
    T = S5_CHUNK
    f32 = F32
    lags = jnp.arange(T, dtype=f32)

    def disc(d):
        lr, li = lam_re[d].astype(f32), lam_im[d].astype(f32)
        dt = jnp.exp(log_dt[d].astype(f32))[:, None]
        mag = jnp.exp(lr * dt)
        ab_re, ab_im = mag * jnp.cos(li * dt), mag * jnp.sin(li * dt)
        den = lr * lr + li * li
        f_re = ((ab_re - 1.0) * lr + ab_im * li) / den
        f_im = (ab_im * lr - (ab_re - 1.0) * li) / den
        br, bi = b_re[d].astype(f32), b_im[d].astype(f32)
        bb_re = f_re[..., None] * br - f_im[..., None] * bi
        bb_im = f_re[..., None] * bi + f_im[..., None] * br

        def power(p):
            ang = li * dt
            mg = jnp.exp(p[:, None, None] * (lr * dt)[None])
            return mg * jnp.cos(p[:, None, None] * ang[None]), mg * jnp.sin(p[:, None, None] * ang[None])
        return bb_re, bb_im, c_re[d].astype(f32), c_im[d].astype(f32), power

    hp = dict(precision=HIGHEST)
    tabs = []
    for d in range(2):
        bb_re, bb_im, cr, ci, power = disc(d)
        pr, pi = power(lags)
        cb_rr = jnp.einsum('gon,lgn,gni->lgoi', cr, pr, bb_re, **hp)
        cb_ii = jnp.einsum('gon,lgn,gni->lgoi', cr, pi, bb_im, **hp)
        cb_ri = jnp.einsum('gon,lgn,gni->lgoi', ci, pr, bb_im, **hp)
        cb_ir = jnp.einsum('gon,lgn,gni->lgoi', ci, pi, bb_re, **hp)
        kern = cb_rr - cb_ii - cb_ri - cb_ir
        qp = (T - 1.0 - lags) if d == 0 else lags
        qr, qi = power(qp)
        wp_re = jnp.einsum('tgn,gni->gitn', qr, bb_re) - jnp.einsum('tgn,gni->gitn', qi, bb_im)
        wp_im = jnp.einsum('tgn,gni->gitn', qr, bb_im) + jnp.einsum('tgn,gni->gitn', qi, bb_re)
        rp = (lags + 1.0) if d == 0 else (T - lags)
        rr, ri = power(rp)
        m_re = jnp.einsum('gon,tgn->gnot', cr, rr) - jnp.einsum('gon,tgn->gnot', ci, ri)
        m_im = jnp.einsum('gon,tgn->gnot', cr, ri) + jnp.einsum('gon,tgn->gnot', ci, rr)
        levels = 2.0 ** jnp.arange(16, dtype=f32) * T
        ar, ai = power(levels)
        tabs.append((kern, wp_re, wp_im, m_re, -m_im, ar, ai))

    kf, kb = tabs[0][0], tabs[1][0]
    skip = jnp.eye(S5_GROUP, dtype=f32)[None] * d_skip.astype(f32).reshape(S5_GROUPS, S5_GROUP, 1)
    k0 = kf[0] + kb[0] + skip
    kk = jnp.concatenate([k0[None], kf[1:], jnp.zeros_like(k0)[None], kb[:0:-1]], axis=0)
    kk = jnp.transpose(kk, (1, 3, 2, 0)).astype(BF16)
    g_, i_, o_ = kk.shape[:3]
    toe = jnp.tile(kk, (1, 1, 1, T))[..., :T * (2 * T - 1)].reshape(g_, i_, o_, T, 2 * T - 1)[..., :T]
    w_intra = jnp.transpose(toe, (0, 1, 3, 2, 4)).reshape(g_, i_ * T, o_ * T)
    w_state = jnp.concatenate([tabs[0][1], tabs[0][2], tabs[1][1], tabs[1][2]], axis=-1)
    w1 = jnp.concatenate([w_intra, w_state.reshape(g_, i_ * T, 4 * S5_STATE).astype(BF16)], axis=-1)
    wc = jnp.concatenate([tabs[0][3], tabs[0][4], tabs[1][3], tabs[1][4]], axis=1)
    wc = wc.reshape(g_, 4 * S5_STATE, o_ * T).astype(BF16)
    mult = []
    for d in range(2):
        ar, ai = tabs[d][5], tabs[d][6]
        mult += [jnp.concatenate([ar, ar], axis=-1), jnp.concatenate([-ai, ai], axis=-1)]
    scan_mult = jnp.transpose(jnp.stack(mult, axis=2), (1, 0, 2, 3))
    return w1, wc, scan_mult


def _s5_kernel(u_ref, w1_ref, wc_ref, mult_ref, o_ref, *, chunks_per_seq):
    T = S5_CHUNK
    ns = 2 * S5_STATE
    x = jnp.concatenate([u_ref[i] for i in range(S5_GROUP)], axis=1)
    y = jnp.dot(x, w1_ref[0], preferred_element_type=F32)
    nc = y.shape[0]
    cidx = lax.broadcasted_iota(jnp.int32, (nc, ns), 0) % chunks_per_seq

    def cmul(s, mre, mim):
        return s * mre + pltpu.roll(s, S5_STATE, axis=1) * mim

    def scan(p, d):
        s = p
        k, step = 0, 1
        while step < chunks_per_seq:
            mre = mult_ref[0, k, 2 * d:2 * d + 1, :]
            mim = mult_ref[0, k, 2 * d + 1:2 * d + 2, :]
            if d == 0:
                sh = jnp.where(cidx >= step, pltpu.roll(s, step, axis=0), 0.0)
            else:
                sh = jnp.where(cidx < chunks_per_seq - step, pltpu.roll(s, nc - step, axis=0), 0.0)
            s = s + cmul(sh, mre, mim)
            k, step = k + 1, step * 2
        if d == 0:
            return jnp.where(cidx >= 1, pltpu.roll(s, 1, axis=0), 0.0)
        return jnp.where(cidx < chunks_per_seq - 1, pltpu.roll(s, nc - 1, axis=0), 0.0)

    base = S5_GROUP * T
    sf = scan(y[:, base:base + ns], 0)
    sb = scan(y[:, base + ns:base + 2 * ns], 1)
    carry = jnp.concatenate([sf, sb], axis=1).astype(BF16)
    ytot = y[:, :base] + jnp.dot(carry, wc_ref[0], preferred_element_type=F32)
    for o in range(S5_GROUP):
        o_ref[o] = ytot[:, o * T:(o + 1) * T]


def s5_scan(ut, w1, wc, scan_mult, seq):
    n = ut.shape[1]
    T = S5_CHUNK
    nc = n // T
    u3 = ut.reshape(W_SSM, nc, T)
    out = pl.pallas_call(
        functools.partial(_s5_kernel, chunks_per_seq=seq // T),
        grid=(S5_GROUPS,),
        in_specs=[
            pl.BlockSpec((S5_GROUP, nc, T), lambda g: (g, 0, 0)),
            pl.BlockSpec((1,) + w1.shape[1:], lambda g: (g, 0, 0)),
            pl.BlockSpec((1,) + wc.shape[1:], lambda g: (g, 0, 0)),
            pl.BlockSpec((1,) + scan_mult.shape[1:], lambda g: (g, 0, 0, 0)),
        ],
        out_specs=pl.BlockSpec((S5_GROUP, nc, T), lambda g: (g, 0, 0)),
        out_shape=jax.ShapeDtypeStruct((W_SSM, nc, T), F32),
        compiler_params=_cparams(("parallel",)),
        name="s5_scan",
    )(u3, w1, wc, scan_mult)
    return out.reshape(W_SSM, n)


def _s5_post_kernel(yt_ref, w_ref, o_ref):
    y = yt_ref[...].T
    y = 0.5 * y * (1.0 + jnp.tanh(math.sqrt(2.0 / math.pi) * (y + 0.044715 * (y * y * y))))
    z = jnp.dot(y.astype(BF16), w_ref[...], preferred_element_type=F32)
    o_ref[...] = (y * jax.nn.sigmoid(z)).astype(BF16)


def s5_post(yt, glu_w_bf, tm=512):
    n = yt.shape[1]
    return pl.pallas_call(
        _s5_post_kernel,
        grid=(n // tm,),
        in_specs=[pl.BlockSpec((W_SSM, tm), lambda i: (0, i)),
                  pl.BlockSpec((W_SSM, W_SSM), lambda i: (0, 0))],
        out_specs=pl.BlockSpec((tm, W_SSM), lambda i: (i, 0)),
        out_shape=jax.ShapeDtypeStruct((n, W_SSM), BF16),
        compiler_params=_cparams(("parallel",)),
        name="s5_post",
    )(yt, glu_w_bf)


SHORT_HALO = 16


def _short_kernel(prev_ref, cur_ref, next_ref, w_ref, b_ref, v_o, x1_o, x2_o, *, tiles_per_seq):
    i = pl.program_id(0)
    t = i % tiles_per_seq
    cur = cur_ref[...].astype(F32)
    tt = cur.shape[0]
    before = jnp.where(t == 0, 0.0, prev_ref[SHORT_HALO - 1:SHORT_HALO, :].astype(F32))
    after = jnp.where(t == tiles_per_seq - 1, 0.0, next_ref[0:1, :].astype(F32))
    ridx = lax.broadcasted_iota(jnp.int32, cur.shape, 0)
    left = jnp.where(ridx == 0, before, pltpu.roll(cur, 1, axis=0))
    right = jnp.where(ridx == tt - 1, after, pltpu.roll(cur, tt - 1, axis=0))
    w = w_ref[...]
    p = left * w[0:1] + cur * w[1:2] + right * w[2:3] + b_ref[...]
    v_o[...] = p[:, 0:W_HYENA].astype(BF16)
    x1_o[...] = p[:, W_HYENA:2 * W_HYENA].astype(BF16)
    x2_o[...] = p[:, 2 * W_HYENA:].astype(BF16)


def hyena_short_conv(hz, short_w, short_b, seq, tt=512):
    n, c3 = hz.shape
    tps = seq // tt
    hb = tt // SHORT_HALO
    nhb = n // SHORT_HALO
    fixed = lambda i: (0, 0)
    osd = jax.ShapeDtypeStruct((n, W_HYENA), BF16)
    return pl.pallas_call(
        functools.partial(_short_kernel, tiles_per_seq=tps),
        grid=(n // tt,),
        in_specs=[
            pl.BlockSpec((SHORT_HALO, c3), lambda i: (jnp.maximum(i * hb - 1, 0), 0)),
            pl.BlockSpec((tt, c3), lambda i: (i, 0)),
            pl.BlockSpec((SHORT_HALO, c3), lambda i: (jnp.minimum((i + 1) * hb, nhb - 1), 0)),
            pl.BlockSpec((3, c3), fixed),
            pl.BlockSpec((1, c3), fixed),
        ],
        out_specs=[pl.BlockSpec((tt, W_HYENA), lambda i: (i, 0))] * 3,
        out_shape=[osd] * 3,
        compiler_params=_cparams(("parallel",)),
        name="hyena_short_conv",
    )(hz, hz, hz, short_w.astype(F32), short_b.astype(F32).reshape(1, c3))


def hyena_filters(l, w1, b1, w2, b2, w3, freq, decay):
    f32 = F32
    pos = jnp.arange(l, dtype=f32)
    t = pos / (l - 1)
    bands = jnp.linspace(1e-4, HY_BANDS - 1, HY_BANDS, dtype=f32)
    ang = (2.0 * math.pi / l) * pos[:, None] * bands[None, :]
    feat = jnp.concatenate([t[:, None], jnp.cos(ang), -jnp.sin(ang)], axis=-1)
    fr = freq.astype(f32)
    hdn = jnp.sin(fr * (jnp.dot(feat, w1.astype(f32), precision=HIGHEST) + b1.astype(f32)))
    hdn = jnp.sin(fr * (jnp.dot(hdn, w2.astype(f32), precision=HIGHEST) + b2.astype(f32)))
    filt = jnp.dot(hdn, w3.astype(f32), precision=HIGHEST).reshape(l, 2, 2, W_HYENA)
    filt = filt * jnp.exp(-t[:, None, None, None] * jnp.abs(decay.astype(f32)))
    k = jnp.concatenate([filt[:, :, 0], jnp.zeros((1, 2, W_HYENA), f32), filt[:0:-1, :, 1]], axis=0)
    k = k / jnp.sum(jnp.abs(k), axis=0, keepdims=True)
    return k.reshape(2 * l, 2 * W_HYENA)


def _dft_tables(n1, rows_in):
    k1 = np.arange(n1)[:, None].astype(np.float64)
    r = np.arange(rows_in)[None, :].astype(np.float64)
    ang = 2.0 * np.pi * k1 * r / n1
    fwd = np.concatenate([np.cos(ang), -np.sin(ang)], axis=0)
    inv = np.concatenate([np.cos(ang).T, -np.sin(ang).T], axis=1)
    return jnp.asarray(fwd, BF16), jnp.asarray(inv, BF16)


def _mid_tables(n1):
    n2 = FFT_N2
    n = n1 * n2
    k = (np.arange(n1)[:, None, None] + n1 * np.arange(n2)[None, :, None]).astype(np.float64)
    m = np.arange(n2)[None, None, :].astype(np.float64)
    ang = 2.0 * np.pi * ((k * m) % n) / n
    gr, gi = np.cos(ang), -np.sin(ang)
    fwd = np.concatenate([np.concatenate([gr, -gi], axis=2), np.concatenate([gi, gr], axis=2)], axis=1)
    hr, hi = np.transpose(gr, (0, 2, 1)), -np.transpose(gi, (0, 2, 1))
    inv = np.concatenate([np.concatenate([hr, -hi], axis=2), np.concatenate([hi, hr], axis=2)], axis=1)
    return jnp.asarray(fwd, BF16), jnp.asarray(inv, BF16)


def _stage1_kernel(z_ref, f_ref, a_ref):
    a_ref[0] = jnp.dot(f_ref[...], z_ref[0], preferred_element_type=F32).astype(BF16)


def fft_stage1(z2, fwd, tc=2048):
    bsz, r, cols = z2.shape
    return pl.pallas_call(
        _stage1_kernel,
        grid=(bsz, cols // tc),
        in_specs=[pl.BlockSpec((1, r, tc), lambda b, j: (b, 0, j)),
                  pl.BlockSpec(fwd.shape, lambda b, j: (0, 0))],
        out_specs=pl.BlockSpec((1, fwd.shape[0], tc), lambda b, j: (b, 0, j)),
        out_shape=jax.ShapeDtypeStruct((bsz, fwd.shape[0], cols), BF16),
        compiler_params=_cparams(("parallel", "parallel")),
        name="fft_stage1",
    )(z2, fwd)


def _mid_kernel(a_ref, g_ref, h_ref, kr_ref, ki_ref, d_ref):
    n2 = FFT_N2
    for j in range(a_ref.shape[2]):
        ab = jnp.concatenate([a_ref[0, 0, j], a_ref[0, 1, j]], axis=0)
        z = jnp.dot(g_ref[j], ab, preferred_element_type=F32)
        zr, zi = z[:n2], z[n2:]
        kr, ki = kr_ref[j], ki_ref[j]
        yb = jnp.concatenate([zr * kr - zi * ki, zr * ki + zi * kr], axis=0).astype(BF16)
        d = jnp.dot(h_ref[j], yb, preferred_element_type=F32)
        d_ref[0, 0, j] = d[:n2].astype(BF16)
        d_ref[0, 1, j] = d[n2:].astype(BF16)


def _mid_fwd_kernel(a_ref, g_ref, zr_ref, zi_ref):
    n2 = FFT_N2
    for j in range(a_ref.shape[2]):
        ab = jnp.concatenate([a_ref[0, 0, j], a_ref[0, 1, j]], axis=0)
        z = jnp.dot(g_ref[j], ab, preferred_element_type=F32)
        zr_ref[j] = z[:n2]
        zi_ref[j] = z[n2:]


def fft_mid(a5, g, h, kf_re, kf_im, order, kb=4):
    bsz, _, n1, n2, c = a5.shape
    kb = min(kb, n1)
    blk = pl.BlockSpec((1, 2, kb, n2, c), lambda k, b: (b, 0, k, 0, 0))
    mat = pl.BlockSpec((kb, 2 * n2, 2 * n2), lambda k, b: (k, 0, 0))
    spec = pl.BlockSpec((kb, n2, c), lambda k, b: (k, 0, order))
    return pl.pallas_call(
        _mid_kernel,
        grid=(n1 // kb, bsz),
        in_specs=[blk, mat, mat, spec, spec],
        out_specs=blk,
        out_shape=jax.ShapeDtypeStruct(a5.shape, BF16),
        compiler_params=_cparams(("parallel", "arbitrary")),
        name="fft_mid",
    )(a5, g, h, kf_re, kf_im)


def fft_mid_fwd(a5, g, kb=4):
    _, _, n1, n2, c = a5.shape
    kb = min(kb, n1)
    spec = pl.BlockSpec((kb, n2, c), lambda k: (k, 0, 0))
    osd = jax.ShapeDtypeStruct((n1, n2, c), F32)
    return pl.pallas_call(
        _mid_fwd_kernel,
        grid=(n1 // kb,),
        in_specs=[pl.BlockSpec((1, 2, kb, n2, c), lambda k: (0, 0, k, 0, 0)),
                  pl.BlockSpec((kb, 2 * n2, 2 * n2), lambda k: (k, 0, 0))],
        out_specs=[spec, spec],
        out_shape=[osd, osd],
        compiler_params=_cparams(("parallel",)),
        name="fft_mid_fwd",
    )(a5, g)


def _fin_kernel(d_ref, inv_ref, z_ref, gate_ref, bias_ref, *rest, scale, chain):
    conv = jnp.dot(inv_ref[...], d_ref[0], preferred_element_type=F32) * scale
    z = gate_ref[0].astype(F32) * (conv + z_ref[0].astype(F32) * bias_ref[...])
    zb = z.astype(BF16)
    if chain:
        f_ref, z_o, a_o = rest
        z_o[0] = zb
        a_o[0] = jnp.dot(f_ref[...], zb, preferred_element_type=F32).astype(BF16)
    else:
        (z_o,) = rest
        z_o[0] = zb


def fft_final(d3, inv, z2, gate2, bias_cols, scale, fwd=None, tc=2048):
    bsz, r, cols = z2.shape
    chain = fwd is not None
    col = lambda b, j: (b, 0, j)
    in_specs = [pl.BlockSpec((1, d3.shape[1], tc), col),
                pl.BlockSpec(inv.shape, lambda b, j: (0, 0)),
                pl.BlockSpec((1, r, tc), col),
                pl.BlockSpec((1, r, tc), col),
                pl.BlockSpec((1, tc), lambda b, j: (0, j))]
    out_specs = [pl.BlockSpec((1, r, tc), col)]
    out_shape = [jax.ShapeDtypeStruct(z2.shape, BF16)]
    args = [d3, inv, z2, gate2, bias_cols]
    if chain:
        in_specs.append(pl.BlockSpec(fwd.shape, lambda b, j: (0, 0)))
        out_specs.append(pl.BlockSpec((1, fwd.shape[0], tc), col))
        out_shape.append(jax.ShapeDtypeStruct((bsz, fwd.shape[0], cols), BF16))
        args.append(fwd)
    return pl.pallas_call(
        functools.partial(_fin_kernel, scale=scale, chain=chain),
        grid=(bsz, cols // tc),
        in_specs=in_specs,
        out_specs=out_specs,
        out_shape=out_shape,
        compiler_params=_cparams(("parallel", "parallel")),
        name="fft_final",
    )(*args)


def hyena_mixer(hz, bsz, seq, short_w, short_b, filt, bias):
    c = W_HYENA
    n2 = FFT_N2
    n = 2 * seq
    n1 = n // n2
    r = n1 // 2
    fwd_half, inv_half = _dft_tables(n1, r)
    fwd_full, _ = _dft_tables(n1, n1)
    g, h = _mid_tables(n1)
    ka = fft_stage1(filt.astype(BF16).reshape(1, n1, n2 * 2 * c), fwd_full)
    kf_re, kf_im = fft_mid_fwd(ka.reshape(1, 2, n1, n2, 2 * c), g)
    v, x1, x2 = hyena_short_conv(hz, short_w, short_b, seq)
    as2 = lambda t: t.reshape(bsz, r, n2 * c)
    z2 = as2(v)
    a = fft_stage1(z2, fwd_half)
    for order, gate in enumerate((x1, x2)):
        d = fft_mid(a.reshape(bsz, 2, n1, n2, c), g, h, kf_re, kf_im, order)
        bias_cols = jnp.tile(bias[order].astype(F32), n2).reshape(1, n2 * c)
        res = fft_final(d.reshape(bsz, 2 * n1, n2 * c), inv_half, z2, as2(gate), bias_cols, 1.0 / n,
                        fwd=fwd_half if order == 0 else None)
        if order == 0:
            z2, a = res
        else:
            (z2,) = res
    return z2.reshape(bsz * seq, c)


def _trunk(x, c, wts):
    bsz, seq, _ = x.shape
    n = bsz * seq
    mod = adaln_mod(c, wts['ada_w'], wts['ada_b'])
    xf = x.reshape(n, D_MODEL)

    q, k, v, u = in_proj0(xf, mod[0], wts['norm_mix_g'][0], wts['ab_w_in'], wts['na_q_g'], wts['na_k_g'], seq)
    att = neighbourhood_attention(q, k, v, wts['att_bias'], bsz, seq)
    cnv = conv_module(u, wts['cv_dw_w'], wts['cv_dw_b'], wts['cv_ln_g'], wts['cv_ln_b'], seq)
    xf, h, gate = out_proj_router(att, cnv, xf, mod[0], wts['ab_w_out'], wts['norm_ffn_g'][0],
                                  wts['router_w'], wts['router_b'], seq)
    xf = moe_ffn(h, gate, xf, mod[0], wts['moe_wg'][0], wts['moe_wu'][0], wts['moe_wd'][0], seq)

    ut, hz = in_proj1(xf, mod[1], wts['norm_mix_g'][1], wts['cd_w_u_t'], wts['cd_w_z'], seq)
    yt = s5_scan(ut, wts['s5_w1'], wts['s5_wc'], wts['s5_mult'], seq)
    ssm = s5_post(yt, wts['s5_glu_w'])
    filt = hyena_filters(seq, *wts['hy_mlp'])
    hy = hyena_mixer(hz, bsz, seq, wts['hy_short_w'], wts['hy_short_b'], filt, wts['hy_bias'])
    xf, h, gate = out_proj_router(ssm, hy, xf, mod[1], wts['cd_w_out'], wts['norm_ffn_g'][1],
                                  wts['router_w'], wts['router_b'], seq)
    xf = moe_ffn(h, gate, xf, mod[1], wts['moe_wg'][1], wts['moe_wu'][1], wts['moe_wd'][1], seq)
    return xf.reshape(bsz, seq, D_MODEL)


def kernel(x_prompt, x_sample, c_prompt, c_sample, ada_w, ada_b, norm_mix_g, norm_ffn_g, router_w, router_b, moe_w_gate, moe_w_up, moe_w_down, ab_w_in, ab_w_out, na_q_g, na_k_g, na_rpb, cv_dw_w, cv_dw_b, cv_ln_g, cv_ln_b, cd_w_in, cd_w_out, s5_lam_re, s5_lam_im, s5_log_dt, s5_b_re, s5_b_im, s5_c_re, s5_c_im, s5_d, s5_glu_w, hy_short_w, hy_short_b, hy_w1, hy_b1, hy_w2, hy_b2, hy_w3, hy_freq, hy_decay, hy_bias):
    s5_w1, s5_wc, s5_mult = s5_tables(s5_lam_re[0], s5_lam_im[0], s5_log_dt[0], s5_b_re[0], s5_b_im[0],
                                      s5_c_re[0], s5_c_im[0], s5_d[0])
    wts = dict(
        ada_w=ada_w, ada_b=ada_b, norm_mix_g=norm_mix_g.astype(F32), norm_ffn_g=norm_ffn_g.astype(F32),
        router_w=router_w, router_b=router_b,
        moe_wg=moe_w_gate.astype(BF16), moe_wu=moe_w_up.astype(BF16), moe_wd=moe_w_down.astype(BF16),
        ab_w_in=ab_w_in[0].astype(BF16), ab_w_out=ab_w_out[0].astype(BF16),
        na_q_g=na_q_g[0], na_k_g=na_k_g[0], att_bias=_att_bias_table(na_rpb[0]),
        cv_dw_w=cv_dw_w[0], cv_dw_b=cv_dw_b[0], cv_ln_g=cv_ln_g[0], cv_ln_b=cv_ln_b[0],
        cd_w_u_t=cd_w_in[0][:, :W_SSM].T.astype(BF16), cd_w_z=cd_w_in[0][:, W_SSM:].astype(BF16),
        cd_w_out=cd_w_out[0].astype(BF16),
        s5_w1=s5_w1, s5_wc=s5_wc, s5_mult=s5_mult, s5_glu_w=s5_glu_w[0].astype(BF16),
        hy_short_w=hy_short_w[0], hy_short_b=hy_short_b[0],
        hy_mlp=(hy_w1[0], hy_b1[0], hy_w2[0], hy_b2[0], hy_w3[0], hy_freq[0], hy_decay[0]),
        hy_bias=hy_bias[0],
    )
    return (_trunk(x_prompt, c_prompt, wts), _trunk(x_sample, c_sample, wts))
```

```python
import functools
import math

import numpy as np
import jax
import jax.numpy as jnp
from jax import lax
from jax.experimental import pallas as pl
from jax.experimental.pallas import tpu as pltpu

F32 = jnp.float32
BF16 = jnp.bfloat16
HIGHEST = lax.Precision.HIGHEST

D_MODEL = 1024
DEPTH = 2
GRID_W = 64
W_ATT = 512
W_CONV = 512
W_SSM = 512
W_HYENA = 512
HEAD_DIM = 64
N_HEADS = 8
WIN_R = 8
WIN_C = 16
CONV_W = 31
S5_GROUP = 16
S5_GROUPS = 32
S5_STATE = 64
HY_BANDS = 8
N_EXPERTS = 16
N_GROUPS = 4
D_FF = 512
EPS = 1e-6
NEG_INF = -1e30

VMEM_LIMIT_BYTES = 56 * 1024 * 1024
LANES = 128

ATT_QROWS = 4
ATT_KROWS = 12
S5_CHUNK = LANES
FFT_N2 = 128


def _cparams(sem):
    return pltpu.CompilerParams(dimension_semantics=sem, vmem_limit_bytes=VMEM_LIMIT_BYTES)


def _mod_kernel(c_ref, w_ref, b_ref, o_ref):
    c = c_ref[...]
    s = c * jax.nn.sigmoid(c)
    o_ref[0] = jnp.dot(s, w_ref[0], precision=HIGHEST, preferred_element_type=F32) + b_ref[0]


def adaln_mod(c, ada_w, ada_b):
    bsz = c.shape[0]
    tn = D_MODEL
    out = pl.pallas_call(
        _mod_kernel,
        grid=(DEPTH, 6 * D_MODEL // tn),
        in_specs=[
            pl.BlockSpec((bsz, D_MODEL), lambda i, j: (0, 0)),
            pl.BlockSpec((1, D_MODEL, tn), lambda i, j: (i, 0, j)),
            pl.BlockSpec((1, 1, tn), lambda i, j: (i, 0, j)),
        ],
        out_specs=pl.BlockSpec((1, bsz, tn), lambda i, j: (i, 0, j)),
        out_shape=jax.ShapeDtypeStruct((DEPTH, bsz, 6 * D_MODEL), F32),
        compiler_params=_cparams(("arbitrary", "arbitrary")),
        name="adaln_mod",
    )(c, ada_w, ada_b.reshape(DEPTH, 1, 6 * D_MODEL))
    return out.reshape(DEPTH, bsz, 6, D_MODEL)


def _norm_mod(x, g, shift, scale):
    ms = jnp.mean(x * x, axis=-1, keepdims=True)
    return x * lax.rsqrt(ms + EPS) * g * (1.0 + scale) + shift


def _in0_kernel(x_ref, mod_ref, g_ref, w_ref, hm_ref, qg_ref, kg_ref, q_o, k_o, v_o, u_o):
    m = mod_ref[0]
    h = _norm_mod(x_ref[...], g_ref[...], m[0:1], m[1:2])
    p = jnp.dot(h.astype(BF16), w_ref[...], preferred_element_type=F32)
    q = p[:, 0:W_ATT]
    k = p[:, W_ATT:2 * W_ATT]
    qms = jnp.dot((q * q).astype(BF16), hm_ref[...], preferred_element_type=F32)
    kms = jnp.dot((k * k).astype(BF16), hm_ref[...], preferred_element_type=F32)
    q_o[...] = (q * lax.rsqrt(qms + EPS) * qg_ref[...]).astype(BF16)
    k_o[...] = (k * lax.rsqrt(kms + EPS) * kg_ref[...]).astype(BF16)
    v_o[...] = p[:, 2 * W_ATT:3 * W_ATT].astype(BF16)
    val = p[:, 3 * W_ATT:3 * W_ATT + W_CONV]
    gate = p[:, 3 * W_ATT + W_CONV:]
    u_o[...] = (val * jax.nn.sigmoid(gate)).astype(BF16)


def in_proj0(x, mod, g, w_in_bf, q_gain, k_gain, seq, tm=512):
    n = x.shape[0]
    head_mean = jnp.asarray(np.kron(np.eye(N_HEADS), np.full((HEAD_DIM, HEAD_DIM), 1.0 / HEAD_DIM)), BF16)
    qg = (jnp.tile(q_gain.astype(F32), N_HEADS) * (HEAD_DIM ** -0.5)).reshape(1, W_ATT)
    kg = jnp.tile(k_gain.astype(F32), N_HEADS).reshape(1, W_ATT)
    tok = lambda i: (i, 0)
    fixed = lambda i: (0, 0)
    osd = jax.ShapeDtypeStruct((n, W_ATT), BF16)
    return pl.pallas_call(
        _in0_kernel,
        grid=(n // tm,),
        in_specs=[
            pl.BlockSpec((tm, D_MODEL), tok),
            pl.BlockSpec((1, 6, D_MODEL), lambda i: ((i * tm) // seq, 0, 0)),
            pl.BlockSpec((1, D_MODEL), fixed),
            pl.BlockSpec(w_in_bf.shape, fixed),
            pl.BlockSpec((W_ATT, W_ATT), fixed),
            pl.BlockSpec((1, W_ATT), fixed),
            pl.BlockSpec((1, W_ATT), fixed),
        ],
        out_specs=[pl.BlockSpec((tm, W_ATT), tok)] * 4,
        out_shape=[osd] * 4,
        compiler_params=_cparams(("parallel",)),
        name="in_proj0",
    )(x, mod, g.reshape(1, D_MODEL), w_in_bf, head_mean, qg, kg)


def _att_bias_table(rpb):
    a = np.arange(ATT_QROWS)[:, None, None, None]
    c = np.arange(GRID_W)[None, :, None, None]
    e = np.arange(ATT_KROWS)[None, None, :, None]
    kc = np.arange(GRID_W)[None, None, None, :]
    c0 = np.clip(c - WIN_C // 2, 0, GRID_W - WIN_C)
    col_ok = (kc >= c0) & (kc < c0 + WIN_C)
    rpb = rpb.astype(F32)
    per = 2 * GRID_W
    vrow = jnp.concatenate([rpb[..., WIN_C - 1:],
                            jnp.zeros(rpb.shape[:2] + (per - (2 * WIN_C - 1),), F32),
                            rpb[..., :WIN_C - 1]], axis=-1)
    tcol = jnp.tile(vrow, (1, 1, GRID_W))[..., :GRID_W * (per - 1)]
    tcol = tcol.reshape(rpb.shape[:2] + (GRID_W, per - 1))[..., :GRID_W]
    tables = []
    for case in range(3):
        if case == 0:
            dr = e - a
            row_ok = (e >= 0) & (e < WIN_R)
        elif case == 1:
            dr = e - a - WIN_R // 2
            row_ok = (dr >= -(WIN_R // 2)) & (dr < WIN_R // 2)
        else:
            dr = e - a - (ATT_KROWS - ATT_QROWS)
            row_ok = (e >= ATT_KROWS - WIN_R) & (e < ATT_KROWS)
        ok = np.broadcast_to(row_ok & col_ok, (ATT_QROWS, GRID_W, ATT_KROWS, GRID_W))
        dri = np.clip(dr + WIN_R - 1, 0, 2 * WIN_R - 2)[:, 0, :, 0]
        nq, nk = ATT_QROWS * GRID_W, ATT_KROWS * GRID_W
        t = jnp.concatenate([jnp.concatenate([tcol[:, int(dri[qa, ke])] for ke in range(ATT_KROWS)], axis=-1)
                             for qa in range(ATT_QROWS)], axis=-2)
        tables.append(jnp.where(jnp.asarray(ok.reshape(nq, nk)), t, NEG_INF))
    return jnp.stack(tables).astype(BF16)


def _att_kernel(q_ref, k0, k1, k2, v0, v1, v2, bias_ref, o_ref):
    kt = [k0, k1, k2]
    vt = [v0, v1, v2]
    nkb = len(kt)
    kw = k0.shape[0]
    tq = q_ref.shape[0]
    first = lax.broadcasted_iota(jnp.int32, (tq, LANES), 1) < HEAD_DIM
    for hp in range(N_HEADS // 2):
        ps = slice(hp * LANES, (hp + 1) * LANES)
        qp = q_ref[:, ps]
        res = []
        for sub in range(2):
            h = 2 * hp + sub
            qm = jnp.where(first if sub == 0 else jnp.logical_not(first), qp, jnp.zeros_like(qp))
            s = [lax.dot_general(qm, kt[j][:, ps], (((1,), (1,)), ((), ())), preferred_element_type=F32)
                 + bias_ref[0, h, :, j * kw:(j + 1) * kw].astype(F32) for j in range(nkb)]
            m = s[0].max(axis=-1, keepdims=True)
            for j in range(1, nkb):
                m = jnp.maximum(m, s[j].max(axis=-1, keepdims=True))
            p = [jnp.exp(sj - m) for sj in s]
            l = p[0].sum(axis=-1, keepdims=True)
            for j in range(1, nkb):
                l = l + p[j].sum(axis=-1, keepdims=True)
            o = jnp.dot(p[0].astype(BF16), vt[0][:, ps], preferred_element_type=F32)
            for j in range(1, nkb):
                o = o + jnp.dot(p[j].astype(BF16), vt[j][:, ps], preferred_element_type=F32)
            res.append(o / l)
        o_ref[:, ps] = jnp.where(first, res[0], res[1]).astype(BF16)


def neighbourhood_attention(q, k, v, bias, bsz, seq):
    rows = seq // GRID_W
    nqb = rows // ATT_QROWS
    tq = ATT_QROWS * GRID_W
    nkb = ATT_KROWS // ATT_QROWS
    assert rows % ATT_QROWS == 0 and nqb >= nkb

    def kmap(j):
        def f(i):
            b, r = i // nqb, i % nqb
            return (b * nqb + jnp.clip(r - 1, 0, nqb - nkb) + j, 0)
        return f

    def bias_map(i):
        r = i % nqb
        return (jnp.where(r == 0, 0, jnp.where(r == nqb - 1, 2, 1)), 0, 0, 0)

    kv_specs = [pl.BlockSpec((tq, W_ATT), kmap(j)) for j in range(nkb)]
    return pl.pallas_call(
        _att_kernel,
        grid=(bsz * nqb,),
        in_specs=[pl.BlockSpec((tq, W_ATT), lambda i: (i, 0))] + kv_specs + kv_specs
        + [pl.BlockSpec((1,) + bias.shape[1:], bias_map)],
        out_specs=pl.BlockSpec((tq, W_ATT), lambda i: (i, 0)),
        out_shape=jax.ShapeDtypeStruct(q.shape, BF16),
        compiler_params=_cparams(("parallel",)),
        name="neighbourhood_attention",
    )(q, k, k, k, v, v, v, bias)


CONV_HALO = 16


def _conv_kernel(prev_ref, cur_ref, next_ref, w_ref, b_ref, g_ref, be_ref, o_ref, win_ref, *, tiles_per_seq):
    i = pl.program_id(0)
    t = i % tiles_per_seq
    tt = cur_ref.shape[0]
    half = CONV_W // 2
    prev = prev_ref[...].astype(F32)
    nxt = next_ref[...].astype(F32)
    win_ref[0:CONV_HALO, :] = jnp.where(t == 0, 0.0, prev)
    win_ref[CONV_HALO:CONV_HALO + tt, :] = cur_ref[...].astype(F32)
    win_ref[CONV_HALO + tt:, :] = jnp.where(t == tiles_per_seq - 1, 0.0, nxt)
    w = w_ref[...]
    acc = jnp.zeros((tt, W_CONV), F32) + b_ref[...]
    for kk in range(CONV_W):
        off = CONV_HALO - half + kk
        acc = acc + win_ref[off:off + tt, :] * w[kk:kk + 1, :]
    mu = jnp.mean(acc, axis=-1, keepdims=True)
    d = acc - mu
    var = jnp.mean(d * d, axis=-1, keepdims=True)
    y = d * lax.rsqrt(var + EPS) * g_ref[...] + be_ref[...]
    o_ref[...] = (y * jax.nn.sigmoid(y)).astype(BF16)


def conv_module(u, dw_w, dw_b, ln_g, ln_b, seq, tt=512):
    n = u.shape[0]
    tps = seq // tt
    hb = tt // CONV_HALO
    nhb = n // CONV_HALO
    row = lambda a: a.astype(F32).reshape(1, W_CONV)
    fixed = lambda i: (0, 0)
    return pl.pallas_call(
        functools.partial(_conv_kernel, tiles_per_seq=tps),
        grid=(n // tt,),
        in_specs=[
            pl.BlockSpec((CONV_HALO, W_CONV), lambda i: (jnp.maximum(i * hb - 1, 0), 0)),
            pl.BlockSpec((tt, W_CONV), lambda i: (i, 0)),
            pl.BlockSpec((CONV_HALO, W_CONV), lambda i: (jnp.minimum((i + 1) * hb, nhb - 1), 0)),
            pl.BlockSpec((CONV_W, W_CONV), fixed),
            pl.BlockSpec((1, W_CONV), fixed),
            pl.BlockSpec((1, W_CONV), fixed),
            pl.BlockSpec((1, W_CONV), fixed),
        ],
        out_specs=pl.BlockSpec((tt, W_CONV), lambda i: (i, 0)),
        out_shape=jax.ShapeDtypeStruct((n, W_CONV), BF16),
        scratch_shapes=[pltpu.VMEM((tt + 2 * CONV_HALO, W_CONV), F32)],
        compiler_params=_cparams(("parallel",)),
        name="conv_module",
    )(u, u, u, dw_w.astype(F32), row(dw_b), row(ln_g), row(ln_b))


def _top2_sum(a, b, c, d):
    hi1, lo1 = jnp.maximum(a, b), jnp.minimum(a, b)
    hi2, lo2 = jnp.maximum(c, d), jnp.minimum(c, d)
    return jnp.maximum(hi1, hi2) + jnp.maximum(jnp.minimum(hi1, hi2), jnp.maximum(lo1, lo2))


def _router_gates(scores_t, bias_t):
    per = N_EXPERTS // N_GROUPS
    rows = [scores_t[e:e + 1, :] + bias_t[e:e + 1, :] for e in range(N_EXPERTS)]
    gscore = [_top2_sum(*rows[g * per:(g + 1) * per]) for g in range(N_GROUPS)]
    best = gscore[0]
    best_idx = jnp.zeros_like(best, dtype=jnp.int32)
    for g in range(1, N_GROUPS):
        better = gscore[g] > best
        best = jnp.where(better, gscore[g], best)
        best_idx = jnp.where(better, g, best_idx)
    out_row = lax.broadcasted_iota(jnp.int32, (LANES, scores_t.shape[1]), 0)
    gates = jnp.zeros((LANES, scores_t.shape[1]), F32)
    total = jnp.zeros_like(best)
    for e in range(N_EXPERTS):
        g = e // per
        rank = jnp.zeros_like(best_idx)
        for e2 in range(g * per, (g + 1) * per):
            if e2 == e:
                continue
            if e2 < e:
                ahead = rows[e2] >= rows[e]
            else:
                ahead = rows[e2] > rows[e]
            rank = rank + jnp.where(ahead, 1, 0)
        chosen = jnp.where(best_idx == g, rank, 2) < 2
        gated = jnp.where(chosen, scores_t[e:e + 1, :], 0.0)
        total = total + gated
        gates = jnp.where(out_row == e, gated, gates)
    return gates / total


def _out_kernel(a_ref, b_ref, x_ref, mod_ref, wa_ref, wb_ref, g_ref, rwh_ref, rwl_ref, rb_ref, x_o, h_o, gate_o,
                gate_t_o):
    m = mod_ref[0]
    mix = (jnp.dot(a_ref[...], wa_ref[...], preferred_element_type=F32)
           + jnp.dot(b_ref[...], wb_ref[...], preferred_element_type=F32))
    x = x_ref[...] + m[2:3] * mix
    x_o[...] = x
    h = _norm_mod(x, g_ref[...], m[3:4], m[4:5])
    h_hi = h.astype(BF16)
    h_o[...] = h_hi
    h_lo = (h - h_hi.astype(F32)).astype(BF16)
    logits = (jnp.dot(h_hi, rwh_ref[...], preferred_element_type=F32)
              + jnp.dot(h_lo, rwh_ref[...], preferred_element_type=F32)
              + jnp.dot(h_hi, rwl_ref[...], preferred_element_type=F32))
    scores_t = jax.nn.sigmoid(logits).T
    gates_t = _router_gates(scores_t, rb_ref[...])
    gate_t_o[...] = gates_t
    gate_o[...] = gates_t.T


def out_proj_router(a, b, x, mod, w_out_bf, g_ffn, router_w, router_b, seq, tm=512):
    n = x.shape[0]
    half = a.shape[1]
    tok = lambda i: (i, 0)
    fixed = lambda i: (0, 0)
    rw = jnp.pad(router_w.astype(F32), ((0, 0), (0, LANES - N_EXPERTS)))
    rw_hi = rw.astype(BF16)
    rw_lo = (rw - rw_hi.astype(F32)).astype(BF16)
    return pl.pallas_call(
        _out_kernel,
        grid=(n // tm,),
        in_specs=[
            pl.BlockSpec((tm, half), tok),
            pl.BlockSpec((tm, half), tok),
            pl.BlockSpec((tm, D_MODEL), tok),
            pl.BlockSpec((1, 6, D_MODEL), lambda i: ((i * tm) // seq, 0, 0)),
            pl.BlockSpec((half, D_MODEL), lambda i: (0, 0)),
            pl.BlockSpec((half, D_MODEL), lambda i: (1, 0)),
            pl.BlockSpec((1, D_MODEL), fixed),
            pl.BlockSpec((D_MODEL, LANES), fixed),
            pl.BlockSpec((D_MODEL, LANES), fixed),
            pl.BlockSpec((N_EXPERTS, 1), fixed),
        ],
        out_specs=[pl.BlockSpec((tm, D_MODEL), tok), pl.BlockSpec((tm, D_MODEL), tok),
                   pl.BlockSpec((tm, LANES), tok), pl.BlockSpec((LANES, tm), lambda i: (0, i))],
        out_shape=[jax.ShapeDtypeStruct((n, D_MODEL), F32), jax.ShapeDtypeStruct((n, D_MODEL), BF16),
                   jax.ShapeDtypeStruct((n, LANES), F32), jax.ShapeDtypeStruct((LANES, n), F32)],
        compiler_params=_cparams(("parallel",)),
        name="out_proj_router",
    )(a, b, x, mod, w_out_bf, w_out_bf, g_ffn.reshape(1, D_MODEL), rw_hi, rw_lo,
      router_b.astype(F32).reshape(N_EXPERTS, 1))


MOE_ROWS = 128
EXPERTS_PER_GROUP = N_EXPERTS // N_GROUPS


def _moe_kernel(h_ref, gate_ref, gate_t_ref, x_ref, mod_ref, tri_ref, wg_ref, wu_ref, wd_ref, o_ref,
                rankc_ref, rankr_ref, xg_ref, gg_ref, yg_ref, acc_ref, cnt_ref):
    e = pl.program_id(1)
    g = e // EXPERTS_PER_GROUP
    j = e % EXPERTS_PER_GROUP
    T = h_ref.shape[0]
    R = MOE_ROWS

    @pl.when(e == 0)
    def _():
        er = lax.broadcasted_iota(jnp.int32, (LANES, LANES), 0)
        ec = lax.broadcasted_iota(jnp.int32, (LANES, LANES), 1)
        sel_c = jnp.where((er < N_EXPERTS) & (er // EXPERTS_PER_GROUP == ec), 1.0, 0.0).astype(BF16)
        sel_r = jnp.where((ec < N_EXPERTS) & (ec // EXPERTS_PER_GROUP == er), 1.0, 0.0).astype(BF16)
        chosen_c = jnp.where(gate_ref[...] > 0.0, 1.0, 0.0).astype(BF16)
        memb_c = jnp.dot(chosen_c, sel_c, preferred_element_type=F32) > 0.0
        rank_c = jnp.dot(tri_ref[...], jnp.where(memb_c, 1.0, 0.0).astype(BF16), preferred_element_type=F32)
        rankc_ref[...] = jnp.where(memb_c, rank_c, -1.0)
        chosen_r = jnp.where(gate_t_ref[...] > 0.0, 1.0, 0.0).astype(BF16)
        memb_r = jnp.dot(sel_r, chosen_r, preferred_element_type=F32) > 0.0
        ones_r = jnp.where(memb_r, 1.0, 0.0)
        rank_r = lax.dot_general(ones_r.astype(BF16), tri_ref[...], (((1,), (1,)), ((), ())),
                                 preferred_element_type=F32)
        rankr_ref[...] = jnp.where(memb_r, rank_r, -1.0)
        for gi in range(N_GROUPS):
            cnt_ref[gi] = jnp.sum(ones_r[gi:gi + 1, :]).astype(jnp.int32)
        acc_ref[...] = jnp.zeros_like(acc_ref)

    nch = (cnt_ref[g] + (R - 1)) // R

    @pl.when(j == 0)
    def _():
        rr = rankr_ref[pl.ds(g, 1), :]
        gt = gate_ref[...]
        g_hi = gt.astype(BF16)
        g_lo = (gt - g_hi.astype(F32)).astype(BF16)
        row = lax.broadcasted_iota(jnp.int32, (R, T), 0).astype(F32)

        def gather(c, carry):
            r0 = pl.multiple_of(c * R, R)
            p = jnp.where(rr - (c * R).astype(F32) == row, 1.0, 0.0).astype(BF16)
            xg_ref[pl.ds(r0, R), :] = jnp.dot(p, h_ref[...], preferred_element_type=F32).astype(BF16)
            gg_ref[pl.ds(r0, R), :] = (jnp.dot(p, g_hi, preferred_element_type=F32)
                                       + jnp.dot(p, g_lo, preferred_element_type=F32))
            yg_ref[pl.ds(r0, R), :] = jnp.zeros((R, D_MODEL), F32)
            return carry

        lax.fori_loop(0, nch, gather, 0)

    lane_r = lax.broadcasted_iota(jnp.int32, (R, LANES), 1)

    def ffn(c, carry):
        r0 = pl.multiple_of(c * R, R)
        xc = xg_ref[pl.ds(r0, R), :]
        a = jnp.dot(xc, wg_ref[0], preferred_element_type=F32)
        u = jnp.dot(xc, wu_ref[0], preferred_element_type=F32)
        gcol = jnp.sum(jnp.where(lane_r == e, gg_ref[pl.ds(r0, R), :], 0.0), axis=1, keepdims=True)
        hid = (a * jax.nn.sigmoid(a)) * u * gcol
        yg_ref[pl.ds(r0, R), :] += jnp.dot(hid.astype(BF16), wd_ref[0], preferred_element_type=F32)
        return carry

    lax.fori_loop(0, nch, ffn, 0)

    @pl.when(j == EXPERTS_PER_GROUP - 1)
    def _():
        lane_t = lax.broadcasted_iota(jnp.int32, (T, LANES), 1)
        rc = jnp.sum(jnp.where(lane_t == g, rankc_ref[...], 0.0), axis=1, keepdims=True)
        col = lax.broadcasted_iota(jnp.int32, (T, R), 1).astype(F32)

        def scatter(c, carry):
            r0 = pl.multiple_of(c * R, R)
            pt = jnp.where(rc - (c * R).astype(F32) == col, 1.0, 0.0).astype(BF16)
            acc_ref[...] += jnp.dot(pt, yg_ref[pl.ds(r0, R), :].astype(BF16), preferred_element_type=F32)
            return carry

        lax.fori_loop(0, nch, scatter, 0)

    @pl.when(e == N_EXPERTS - 1)
    def _():
        o_ref[...] = x_ref[...] + mod_ref[0][5:6] * acc_ref[...]


def moe_ffn(h, gate, gate_t, x, mod, wg_bf, wu_bf, wd_bf, seq, tm=1024):
    n = x.shape[0]
    tm = min(tm, seq)
    tok = lambda i, e: (i, 0)
    tri = jnp.asarray(np.tril(np.ones((tm, tm), np.float32), -1), BF16)
    return pl.pallas_call(
        _moe_kernel,
        grid=(n // tm, N_EXPERTS),
        in_specs=[
            pl.BlockSpec((tm, D_MODEL), tok),
            pl.BlockSpec((tm, LANES), tok),
            pl.BlockSpec((LANES, tm), lambda i, e: (0, i)),
            pl.BlockSpec((tm, D_MODEL), tok),
            pl.BlockSpec((1, 6, D_MODEL), lambda i, e: ((i * tm) // seq, 0, 0)),
            pl.BlockSpec((tm, tm), lambda i, e: (0, 0)),
            pl.BlockSpec((1, D_MODEL, D_FF), lambda i, e: (e, 0, 0)),
            pl.BlockSpec((1, D_MODEL, D_FF), lambda i, e: (e, 0, 0)),
            pl.BlockSpec((1, D_FF, D_MODEL), lambda i, e: (e, 0, 0)),
        ],
        out_specs=pl.BlockSpec((tm, D_MODEL), tok),
        out_shape=jax.ShapeDtypeStruct((n, D_MODEL), F32),
        scratch_shapes=[pltpu.VMEM((tm, LANES), F32), pltpu.VMEM((LANES, tm), F32),
                        pltpu.VMEM((tm, D_MODEL), BF16), pltpu.VMEM((tm, LANES), F32),
                        pltpu.VMEM((tm, D_MODEL), F32), pltpu.VMEM((tm, D_MODEL), F32),
                        pltpu.SMEM((N_GROUPS,), jnp.int32)],
        compiler_params=_cparams(("parallel", "arbitrary")),
        name="moe_ffn",
    )(h, gate, gate_t, x, mod, tri, wg_bf, wu_bf, wd_bf)


def _in1_kernel(x_ref, mod_ref, g_ref, wut_ref, wz_ref, ut_o, hz_o):
    m = mod_ref[0]
    h = _norm_mod(x_ref[...], g_ref[...], m[0:1], m[1:2]).astype(BF16)
    ut_o[...] = lax.dot_general(wut_ref[...], h, (((1,), (1,)), ((), ())),
                                preferred_element_type=F32).astype(BF16)
    hz_o[...] = jnp.dot(h, wz_ref[...], preferred_element_type=F32).astype(BF16)


def in_proj1(x, mod, g, w_u_t_bf, w_z_bf, seq, tm=512):
    n = x.shape[0]
    fixed = lambda i: (0, 0)
    return pl.pallas_call(
        _in1_kernel,
        grid=(n // tm,),
        in_specs=[
            pl.BlockSpec((tm, D_MODEL), lambda i: (i, 0)),
            pl.BlockSpec((1, 6, D_MODEL), lambda i: ((i * tm) // seq, 0, 0)),
            pl.BlockSpec((1, D_MODEL), fixed),
            pl.BlockSpec(w_u_t_bf.shape, fixed),
            pl.BlockSpec(w_z_bf.shape, fixed),
        ],
        out_specs=[pl.BlockSpec((W_SSM, tm), lambda i: (0, i)),
                   pl.BlockSpec((tm, 3 * W_HYENA), lambda i: (i, 0))],
        out_shape=[jax.ShapeDtypeStruct((W_SSM, n), BF16), jax.ShapeDtypeStruct((n, 3 * W_HYENA), BF16)],
        compiler_params=_cparams(("parallel",)),
        name="in_proj1",
    )(x, mod, g.reshape(1, D_MODEL), w_u_t_bf, w_z_bf)


def s5_tables(lam_re, lam_im, log_dt, b_re, b_im, c_re, c_im, d_skip):
    T = S5_CHUNK
    f32 = F32
    lags = jnp.arange(T, dtype=f32)

    def disc(d):
        lr, li = lam_re[d].astype(f32), lam_im[d].astype(f32)
        dt = jnp.exp(log_dt[d].astype(f32))[:, None]
        mag = jnp.exp(lr * dt)
        ab_re, ab_im = mag * jnp.cos(li * dt), mag * jnp.sin(li * dt)
        den = lr * lr + li * li
        f_re = ((ab_re - 1.0) * lr + ab_im * li) / den
        f_im = (ab_im * lr - (ab_re - 1.0) * li) / den
        br, bi = b_re[d].astype(f32), b_im[d].astype(f32)
        bb_re = f_re[..., None] * br - f_im[..., None] * bi
        bb_im = f_re[..., None] * bi + f_im[..., None] * br

        def power(p):
            ang = li * dt
            mg = jnp.exp(p[:, None, None] * (lr * dt)[None])
            return mg * jnp.cos(p[:, None, None] * ang[None]), mg * jnp.sin(p[:, None, None] * ang[None])
        return bb_re, bb_im, c_re[d].astype(f32), c_im[d].astype(f32), power

    hp = dict(precision=HIGHEST)
    tabs = []
    for d in range(2):
        bb_re, bb_im, cr, ci, power = disc(d)
        pr, pi = power(lags)
        cb_rr = jnp.einsum('gon,lgn,gni->lgoi', cr, pr, bb_re, **hp)
        cb_ii = jnp.einsum('gon,lgn,gni->lgoi', cr, pi, bb_im, **hp)
        cb_ri = jnp.einsum('gon,lgn,gni->lgoi', ci, pr, bb_im, **hp)
        cb_ir = jnp.einsum('gon,lgn,gni->lgoi', ci, pi, bb_re, **hp)
        kern = cb_rr - cb_ii - cb_ri - cb_ir
        qp = (T - 1.0 - lags) if d == 0 else lags
        qr, qi = power(qp)
        wp_re = jnp.einsum('tgn,gni->gitn', qr, bb_re) - jnp.einsum('tgn,gni->gitn', qi, bb_im)
        wp_im = jnp.einsum('tgn,gni->gitn', qr, bb_im) + jnp.einsum('tgn,gni->gitn', qi, bb_re)
        rp = (lags + 1.0) if d == 0 else (T - lags)
        rr, ri = power(rp)
        m_re = jnp.einsum('gon,tgn->gnot', cr, rr) - jnp.einsum('gon,tgn->gnot', ci, ri)
        m_im = jnp.einsum('gon,tgn->gnot', cr, ri) + jnp.einsum('gon,tgn->gnot', ci, rr)
        levels = 2.0 ** jnp.arange(16, dtype=f32) * T
        ar, ai = power(levels)
        tabs.append((kern, wp_re, wp_im, m_re, -m_im, ar, ai))

    kf, kb = tabs[0][0], tabs[1][0]
    skip = jnp.eye(S5_GROUP, dtype=f32)[None] * d_skip.astype(f32).reshape(S5_GROUPS, S5_GROUP, 1)
    k0 = kf[0] + kb[0] + skip
    kk = jnp.concatenate([k0[None], kf[1:], jnp.zeros_like(k0)[None], kb[:0:-1]], axis=0)
    kk = jnp.transpose(kk, (1, 3, 2, 0))
    g_, i_, o_ = kk.shape[:3]
    kk = kk.reshape(g_, i_ * o_, 2 * T)
    w_state = jnp.concatenate([tabs[0][1], tabs[0][2], tabs[1][1], tabs[1][2]], axis=-1)
    wp = w_state.reshape(g_, i_ * T, 4 * S5_STATE).astype(BF16)
    wc = jnp.concatenate([tabs[0][3], tabs[0][4], tabs[1][3], tabs[1][4]], axis=1)
    wc = wc.reshape(g_, 4 * S5_STATE, o_ * T).astype(BF16)
    mult = []
    for d in range(2):
        ar, ai = tabs[d][5], tabs[d][6]
        mult += [jnp.concatenate([ar, ar], axis=-1), jnp.concatenate([-ai, ai], axis=-1)]
    scan_mult = jnp.transpose(jnp.stack(mult, axis=2), (1, 0, 2, 3))
    return kk, wp, wc, scan_mult


def _s5_kernel(*refs, chunks_per_seq):
    nt = len(chunks_per_seq)
    u_refs, (kk_ref, wp_ref, wc_ref, mult_ref) = refs[:nt], refs[nt:nt + 4]
    o_refs, w_ref = refs[nt + 4:2 * nt + 4], refs[2 * nt + 4]
    T = S5_CHUNK

    def build(i, carry):
        for o in range(S5_GROUP):
            row = kk_ref[0, pl.ds(i * S5_GROUP + o, 1), :]
            toe = pltpu.roll(jnp.broadcast_to(row, (T, 2 * T)), 0, axis=1, stride=1, stride_axis=0)
            w_ref[pl.ds(pl.multiple_of(i * T, T), T), o * T:(o + 1) * T] = toe[:, :T].astype(BF16)
        return carry

    lax.fori_loop(0, S5_GROUP, build, 0)
    for u_ref, o_ref, cps in zip(u_refs, o_refs, chunks_per_seq):
        _s5_apply(u_ref, o_ref, w_ref, wp_ref, wc_ref, mult_ref, cps)


def _s5_apply(u_ref, o_ref, w_ref, wp_ref, wc_ref, mult_ref, chunks_per_seq):
    T = S5_CHUNK
    ns = 2 * S5_STATE
    x = jnp.concatenate([u_ref[i] for i in range(S5_GROUP)], axis=1)
    y = jnp.dot(x, w_ref[...], preferred_element_type=F32)
    pst = jnp.dot(x, wp_ref[0], preferred_element_type=F32)
    nc = y.shape[0]
    cidx = lax.broadcasted_iota(jnp.int32, (nc, ns), 0) % chunks_per_seq

    def cmul(s, mre, mim):
        return s * mre + pltpu.roll(s, S5_STATE, axis=1) * mim

    def scan(p, d):
        s = p
        k, step = 0, 1
        while step < chunks_per_seq:
            mre = mult_ref[0, k, 2 * d:2 * d + 1, :]
            mim = mult_ref[0, k, 2 * d + 1:2 * d + 2, :]
            if d == 0:
                sh = jnp.where(cidx >= step, pltpu.roll(s, step, axis=0), 0.0)
            else:
                sh = jnp.where(cidx < chunks_per_seq - step, pltpu.roll(s, nc - step, axis=0), 0.0)
            s = s + cmul(sh, mre, mim)
            k, step = k + 1, step * 2
        if d == 0:
            return jnp.where(cidx >= 1, pltpu.roll(s, 1, axis=0), 0.0)
        return jnp.where(cidx < chunks_per_seq - 1, pltpu.roll(s, nc - 1, axis=0), 0.0)

    sf = scan(pst[:, :ns], 0)
    sb = scan(pst[:, ns:], 1)
    carry = jnp.concatenate([sf, sb], axis=1).astype(BF16)
    ytot = y + jnp.dot(carry, wc_ref[0], preferred_element_type=F32)
    for o in range(S5_GROUP):
        o_ref[o] = ytot[:, o * T:(o + 1) * T]


def s5_scan(uts, seqs, kk, wp, wc, scan_mult):
    T = S5_CHUNK
    u3 = [ut.reshape(W_SSM, ut.shape[1] // T, T) for ut in uts]
    blk = [pl.BlockSpec((S5_GROUP,) + u.shape[1:], lambda g: (g, 0, 0)) for u in u3]
    grp = lambda a: pl.BlockSpec((1,) + a.shape[1:], lambda g: (g,) + (0,) * (a.ndim - 1))
    outs = pl.pallas_call(
        functools.partial(_s5_kernel, chunks_per_seq=tuple(s // T for s in seqs)),
        grid=(S5_GROUPS,),
        in_specs=blk + [grp(kk), grp(wp), grp(wc), grp(scan_mult)],
        out_specs=blk,
        out_shape=[jax.ShapeDtypeStruct(u.shape, F32) for u in u3],
        scratch_shapes=[pltpu.VMEM((S5_GROUP * T, S5_GROUP * T), BF16)],
        compiler_params=_cparams(("parallel",)),
        name="s5_scan",
    )(*u3, kk, wp, wc, scan_mult)
    return [o.reshape(W_SSM, -1) for o in outs]


def _s5_post_kernel(yt_ref, w_ref, o_ref):
    y = yt_ref[...].T
    y = 0.5 * y * (1.0 + jnp.tanh(math.sqrt(2.0 / math.pi) * (y + 0.044715 * (y * y * y))))
    z = jnp.dot(y.astype(BF16), w_ref[...], preferred_element_type=F32)
    o_ref[...] = (y * jax.nn.sigmoid(z)).astype(BF16)


def s5_post(yt, glu_w_bf, tm=512):
    n = yt.shape[1]
    return pl.pallas_call(
        _s5_post_kernel,
        grid=(n // tm,),
        in_specs=[pl.BlockSpec((W_SSM, tm), lambda i: (0, i)),
                  pl.BlockSpec((W_SSM, W_SSM), lambda i: (0, 0))],
        out_specs=pl.BlockSpec((tm, W_SSM), lambda i: (i, 0)),
        out_shape=jax.ShapeDtypeStruct((n, W_SSM), BF16),
        compiler_params=_cparams(("parallel",)),
        name="s5_post",
    )(yt, glu_w_bf)


SHORT_HALO = 16


def _short_kernel(prev_ref, cur_ref, next_ref, w_ref, b_ref, v_o, x1_o, x2_o, *, tiles_per_seq):
    i = pl.program_id(0)
    t = i % tiles_per_seq
    cur = cur_ref[...].astype(F32)
    tt = cur.shape[0]
    before = jnp.where(t == 0, 0.0, prev_ref[SHORT_HALO - 1:SHORT_HALO, :].astype(F32))
    after = jnp.where(t == tiles_per_seq - 1, 0.0, next_ref[0:1, :].astype(F32))
    ridx = lax.broadcasted_iota(jnp.int32, cur.shape, 0)
    left = jnp.where(ridx == 0, before, pltpu.roll(cur, 1, axis=0))
    right = jnp.where(ridx == tt - 1, after, pltpu.roll(cur, tt - 1, axis=0))
    w = w_ref[...]
    p = left * w[0:1] + cur * w[1:2] + right * w[2:3] + b_ref[...]
    v_o[...] = p[:, 0:W_HYENA].astype(BF16)
    x1_o[...] = p[:, W_HYENA:2 * W_HYENA].astype(BF16)
    x2_o[...] = p[:, 2 * W_HYENA:].astype(BF16)


def hyena_short_conv(hz, short_w, short_b, seq, tt=512):
    n, c3 = hz.shape
    tps = seq // tt
    hb = tt // SHORT_HALO
    nhb = n // SHORT_HALO
    fixed = lambda i: (0, 0)
    osd = jax.ShapeDtypeStruct((n, W_HYENA), BF16)
    return pl.pallas_call(
        functools.partial(_short_kernel, tiles_per_seq=tps),
        grid=(n // tt,),
        in_specs=[
            pl.BlockSpec((SHORT_HALO, c3), lambda i: (jnp.maximum(i * hb - 1, 0), 0)),
            pl.BlockSpec((tt, c3), lambda i: (i, 0)),
            pl.BlockSpec((SHORT_HALO, c3), lambda i: (jnp.minimum((i + 1) * hb, nhb - 1), 0)),
            pl.BlockSpec((3, c3), fixed),
            pl.BlockSpec((1, c3), fixed),
        ],
        out_specs=[pl.BlockSpec((tt, W_HYENA), lambda i: (i, 0))] * 3,
        out_shape=[osd] * 3,
        compiler_params=_cparams(("parallel",)),
        name="hyena_short_conv",
    )(hz, hz, hz, short_w.astype(F32), short_b.astype(F32).reshape(1, c3))


def _filter_kernel(band_ref, w1_ref, b1_ref, w2_ref, b2_ref, w3_ref, freq_ref, decay_ref, k_o, sum_o, *, l):
    i = pl.program_id(0)
    tr = k_o.shape[0]
    m = i * tr + lax.broadcasted_iota(jnp.int32, (tr, LANES), 0)
    pos = jnp.where(m > l, 2 * l - m, m).astype(F32)
    t = pos / (l - 1)
    ang = ((2.0 * math.pi / l) * pos) * band_ref[...]
    lane = lax.broadcasted_iota(jnp.int32, (tr, LANES), 1)
    feat = jnp.where(lane == 0, t, jnp.where(lane <= HY_BANDS, jnp.cos(ang),
                                              jnp.where(lane <= 2 * HY_BANDS, -jnp.sin(ang), 0.0)))
    fr = freq_ref[...]
    hdn = jnp.sin(fr * (jnp.dot(feat, w1_ref[...], precision=HIGHEST, preferred_element_type=F32) + b1_ref[...]))
    hdn = jnp.sin(fr * (jnp.dot(hdn, w2_ref[...], precision=HIGHEST, preferred_element_type=F32) + b2_ref[...]))
    f = jnp.dot(hdn, w3_ref[0], precision=HIGHEST, preferred_element_type=F32)
    reps = f.shape[1] // LANES
    t_all = jnp.concatenate([t] * reps, axis=1)
    m_all = jnp.concatenate([m] * reps, axis=1)
    f = jnp.where(m_all == l, 0.0, f * jnp.exp(-t_all * jnp.abs(decay_ref[0])))
    k_o[...] = f.astype(BF16)

    @pl.when(i == 0)
    def _():
        sum_o[...] = jnp.zeros_like(sum_o)

    sum_o[...] += jnp.sum(jnp.abs(f), axis=0, keepdims=True)


def hyena_filters(l, w1, b1, w2, b2, w3, freq, decay, tr=512):
    f32 = F32
    n_ord = decay.shape[0]
    hid = w1.shape[1]
    cols = n_ord * W_HYENA
    bands = jnp.linspace(1e-4, HY_BANDS - 1, HY_BANDS, dtype=f32)
    band_row = jnp.zeros((1, LANES), f32).at[0, 1:1 + HY_BANDS].set(bands).at[0, 1 + HY_BANDS:1 + 2 * HY_BANDS].set(bands)
    pad2 = lambda a, r, c: jnp.pad(a.astype(f32), ((0, r - a.shape[0]), (0, c - a.shape[1])))
    row = lambda a: pad2(a.reshape(1, -1), 1, LANES)
    w3h = jnp.transpose(w3.astype(f32).reshape(hid, n_ord, 2, W_HYENA), (2, 0, 1, 3)).reshape(2, hid, cols)
    w3h = jnp.pad(w3h, ((0, 0), (0, LANES - hid), (0, 0)))
    dech = jnp.transpose(decay.astype(f32), (1, 0, 2)).reshape(2, 1, cols)
    nt = 2 * l // tr
    fixed = lambda i: (0, 0)
    half = lambda i: (i // (nt // 2), 0, 0)
    k_un, ksum = pl.pallas_call(
        functools.partial(_filter_kernel, l=l),
        grid=(nt,),
        in_specs=[pl.BlockSpec((1, LANES), fixed), pl.BlockSpec((LANES, LANES), fixed), pl.BlockSpec((1, LANES), fixed),
                  pl.BlockSpec((LANES, LANES), fixed), pl.BlockSpec((1, LANES), fixed),
                  pl.BlockSpec((1, LANES, cols), half), pl.BlockSpec((1, LANES), fixed),
                  pl.BlockSpec((1, 1, cols), half)],
        out_specs=[pl.BlockSpec((tr, cols), lambda i: (i, 0)), pl.BlockSpec((8, cols), fixed)],
        out_shape=[jax.ShapeDtypeStruct((2 * l, cols), BF16), jax.ShapeDtypeStruct((8, cols), f32)],
        compiler_params=_cparams(("arbitrary",)),
        name="hyena_filter",
    )(band_row, pad2(w1, LANES, LANES), row(b1), pad2(w2, LANES, LANES), row(b2), w3h, row(freq), dech)
    return k_un, 1.0 / ksum[0:1]


def _dft_tables(n1, rows_in):
    k1 = np.arange(n1)[:, None].astype(np.float64)
    r = np.arange(rows_in)[None, :].astype(np.float64)
    ang = 2.0 * np.pi * k1 * r / n1
    fwd = np.concatenate([np.cos(ang), -np.sin(ang)], axis=0)
    inv = np.concatenate([np.cos(ang).T, -np.sin(ang).T], axis=1)
    return jnp.asarray(fwd, BF16), jnp.asarray(inv, BF16)


def _mid_tables(n1):
    n2 = FFT_N2
    n = n1 * n2
    k = (np.arange(n1)[:, None, None] + n1 * np.arange(n2)[None, :, None]).astype(np.float64)
    m = np.arange(n2)[None, None, :].astype(np.float64)
    ang = 2.0 * np.pi * ((k * m) % n) / n
    gr, gi = np.cos(ang), -np.sin(ang)
    fwd = np.concatenate([np.concatenate([gr, -gi], axis=2), np.concatenate([gi, gr], axis=2)], axis=1)
    hr, hi = np.transpose(gr, (0, 2, 1)), -np.transpose(gi, (0, 2, 1))
    inv = np.concatenate([np.concatenate([hr, -hi], axis=2), np.concatenate([hi, hr], axis=2)], axis=1)
    return jnp.asarray(fwd, BF16), jnp.asarray(inv, BF16)


def _stage1_kernel(z_ref, f_ref, a_ref):
    a_ref[0] = jnp.dot(f_ref[...], z_ref[0], preferred_element_type=F32).astype(BF16)


def fft_stage1(z2, fwd, tc=2048):
    bsz, r, cols = z2.shape
    return pl.pallas_call(
        _stage1_kernel,
        grid=(bsz, cols // tc),
        in_specs=[pl.BlockSpec((1, r, tc), lambda b, j: (b, 0, j)),
                  pl.BlockSpec(fwd.shape, lambda b, j: (0, 0))],
        out_specs=pl.BlockSpec((1, fwd.shape[0], tc), lambda b, j: (b, 0, j)),
        out_shape=jax.ShapeDtypeStruct((bsz, fwd.shape[0], cols), BF16),
        compiler_params=_cparams(("parallel", "parallel")),
        name="fft_stage1",
    )(z2, fwd)


def _mid_kernel(a_ref, g_ref, h_ref, kr_ref, ki_ref, d_ref):
    n2 = FFT_N2
    for j in range(a_ref.shape[2]):
        ab = jnp.concatenate([a_ref[0, 0, j], a_ref[0, 1, j]], axis=0)
        z = jnp.dot(g_ref[j], ab, preferred_element_type=F32)
        zr, zi = z[:n2], z[n2:]
        kr, ki = kr_ref[j], ki_ref[j]
        yb = jnp.concatenate([zr * kr - zi * ki, zr * ki + zi * kr], axis=0).astype(BF16)
        d = jnp.dot(h_ref[j], yb, preferred_element_type=F32)
        d_ref[0, 0, j] = d[:n2].astype(BF16)
        d_ref[0, 1, j] = d[n2:].astype(BF16)


def _mid_fwd_kernel(a_ref, g_ref, s_ref, zr_ref, zi_ref):
    n2 = FFT_N2
    for j in range(a_ref.shape[2]):
        ab = jnp.concatenate([a_ref[0, 0, j], a_ref[0, 1, j]], axis=0)
        z = jnp.dot(g_ref[j], ab, preferred_element_type=F32) * s_ref[...]
        zr_ref[j] = z[:n2]
        zi_ref[j] = z[n2:]


def fft_mid(a5, g, h, kf_re, kf_im, order, kb=4):
    bsz, _, n1, n2, c = a5.shape
    kb = min(kb, n1)
    blk = pl.BlockSpec((1, 2, kb, n2, c), lambda k, b: (b, 0, k, 0, 0))
    mat = pl.BlockSpec((kb, 2 * n2, 2 * n2), lambda k, b: (k, 0, 0))
    spec = pl.BlockSpec((kb, n2, c), lambda k, b: (k, 0, order))
    return pl.pallas_call(
        _mid_kernel,
        grid=(n1 // kb, bsz),
        in_specs=[blk, mat, mat, spec, spec],
        out_specs=blk,
        out_shape=jax.ShapeDtypeStruct(a5.shape, BF16),
        compiler_params=_cparams(("parallel", "arbitrary")),
        name="fft_mid",
    )(a5, g, h, kf_re, kf_im)


def fft_mid_fwd(a5, g, col_scale, kb=4):
    _, _, n1, n2, c = a5.shape
    kb = min(kb, n1)
    spec = pl.BlockSpec((kb, n2, c), lambda k: (k, 0, 0))
    osd = jax.ShapeDtypeStruct((n1, n2, c), F32)
    return pl.pallas_call(
        _mid_fwd_kernel,
        grid=(n1 // kb,),
        in_specs=[pl.BlockSpec((1, 2, kb, n2, c), lambda k: (0, 0, k, 0, 0)),
                  pl.BlockSpec((kb, 2 * n2, 2 * n2), lambda k: (k, 0, 0)),
                  pl.BlockSpec((1, c), lambda k: (0, 0))],
        out_specs=[spec, spec],
        out_shape=[osd, osd],
        compiler_params=_cparams(("parallel",)),
        name="fft_mid_fwd",
    )(a5, g, col_scale)


def _fin_kernel(d_ref, inv_ref, z_ref, gate_ref, bias_ref, *rest, scale, chain):
    conv = jnp.dot(inv_ref[...], d_ref[0], preferred_element_type=F32) * scale
    z = gate_ref[0].astype(F32) * (conv + z_ref[0].astype(F32) * bias_ref[...])
    zb = z.astype(BF16)
    if chain:
        f_ref, z_o, a_o = rest
        z_o[0] = zb
        a_o[0] = jnp.dot(f_ref[...], zb, preferred_element_type=F32).astype(BF16)
    else:
        (z_o,) = rest
        z_o[0] = zb


def fft_final(d3, inv, z2, gate2, bias_cols, scale, fwd=None, tc=2048):
    bsz, r, cols = z2.shape
    chain = fwd is not None
    col = lambda b, j: (b, 0, j)
    in_specs = [pl.BlockSpec((1, d3.shape[1], tc), col),
                pl.BlockSpec(inv.shape, lambda b, j: (0, 0)),
                pl.BlockSpec((1, r, tc), col),
                pl.BlockSpec((1, r, tc), col),
                pl.BlockSpec((1, tc), lambda b, j: (0, j))]
    out_specs = [pl.BlockSpec((1, r, tc), col)]
    out_shape = [jax.ShapeDtypeStruct(z2.shape, BF16)]
    args = [d3, inv, z2, gate2, bias_cols]
    if chain:
        in_specs.append(pl.BlockSpec(fwd.shape, lambda b, j: (0, 0)))
        out_specs.append(pl.BlockSpec((1, fwd.shape[0], tc), col))
        out_shape.append(jax.ShapeDtypeStruct((bsz, fwd.shape[0], cols), BF16))
        args.append(fwd)
    return pl.pallas_call(
        functools.partial(_fin_kernel, scale=scale, chain=chain),
        grid=(bsz, cols // tc),
        in_specs=in_specs,
        out_specs=out_specs,
        out_shape=out_shape,
        compiler_params=_cparams(("parallel", "parallel")),
        name="fft_final",
    )(*args)


def hyena_mixer(hz, bsz, seq, short_w, short_b, filt, filt_scale, bias):
    c = W_HYENA
    n2 = FFT_N2
    n = 2 * seq
    n1 = n // n2
    r = n1 // 2
    fwd_half, inv_half = _dft_tables(n1, r)
    fwd_full, _ = _dft_tables(n1, n1)
    g, h = _mid_tables(n1)
    ka = fft_stage1(filt.reshape(1, n1, n2 * 2 * c), fwd_full)
    kf_re, kf_im = fft_mid_fwd(ka.reshape(1, 2, n1, n2, 2 * c), g, filt_scale)
    v, x1, x2 = hyena_short_conv(hz, short_w, short_b, seq)
    as2 = lambda t: t.reshape(bsz, r, n2 * c)
    z2 = as2(v)
    a = fft_stage1(z2, fwd_half)
    for order, gate in enumerate((x1, x2)):
        d = fft_mid(a.reshape(bsz, 2, n1, n2, c), g, h, kf_re, kf_im, order)
        bias_cols = jnp.tile(bias[order].astype(F32), n2).reshape(1, n2 * c)
        res = fft_final(d.reshape(bsz, 2 * n1, n2 * c), inv_half, z2, as2(gate), bias_cols, 1.0 / n,
                        fwd=fwd_half if order == 0 else None)
        if order == 0:
            z2, a = res
        else:
            (z2,) = res
    return z2.reshape(bsz * seq, c)


def _trunk_layer0(x, c, wts):
    bsz, seq, _ = x.shape
    n = bsz * seq
    mod = adaln_mod(c, wts['ada_w'], wts['ada_b'])
    xf = x.reshape(n, D_MODEL)
    q, k, v, u = in_proj0(xf, mod[0], wts['norm_mix_g'][0], wts['ab_w_in'], wts['na_q_g'], wts['na_k_g'], seq)
    att = neighbourhood_attention(q, k, v, wts['att_bias'], bsz, seq)
    cnv = conv_module(u, wts['cv_dw_w'], wts['cv_dw_b'], wts['cv_ln_g'], wts['cv_ln_b'], seq)
    xf, h, gate, gate_t = out_proj_router(att, cnv, xf, mod[0], wts['ab_w_out'], wts['norm_ffn_g'][0],
                                          wts['router_w'], wts['router_b'], seq)
    xf = moe_ffn(h, gate, gate_t, xf, mod[0], wts['moe_wg'][0], wts['moe_wu'][0], wts['moe_wd'][0], seq)
    ut, hz = in_proj1(xf, mod[1], wts['norm_mix_g'][1], wts['cd_w_u_t'], wts['cd_w_z'], seq)
    return xf, mod, ut, hz


def _trunk_layer1(xf, mod, yt, hz, bsz, seq, wts):
    ssm = s5_post(yt, wts['s5_glu_w'])
    filt, filt_scale = hyena_filters(seq, *wts['hy_mlp'])
    hy = hyena_mixer(hz, bsz, seq, wts['hy_short_w'], wts['hy_short_b'], filt, filt_scale, wts['hy_bias'])
    xf, h, gate, gate_t = out_proj_router(ssm, hy, xf, mod[1], wts['cd_w_out'], wts['norm_ffn_g'][1],
                                          wts['router_w'], wts['router_b'], seq)
    xf = moe_ffn(h, gate, gate_t, xf, mod[1], wts['moe_wg'][1], wts['moe_wu'][1], wts['moe_wd'][1], seq)
    return xf.reshape(bsz, seq, D_MODEL)


def kernel(x_prompt, x_sample, c_prompt, c_sample, ada_w, ada_b, norm_mix_g, norm_ffn_g, router_w, router_b, moe_w_gate, moe_w_up, moe_w_down, ab_w_in, ab_w_out, na_q_g, na_k_g, na_rpb, cv_dw_w, cv_dw_b, cv_ln_g, cv_ln_b, cd_w_in, cd_w_out, s5_lam_re, s5_lam_im, s5_log_dt, s5_b_re, s5_b_im, s5_c_re, s5_c_im, s5_d, s5_glu_w, hy_short_w, hy_short_b, hy_w1, hy_b1, hy_w2, hy_b2, hy_w3, hy_freq, hy_decay, hy_bias):
    s5_tabs = s5_tables(s5_lam_re[0], s5_lam_im[0], s5_log_dt[0], s5_b_re[0], s5_b_im[0],
                        s5_c_re[0], s5_c_im[0], s5_d[0])
    wts = dict(
        ada_w=ada_w, ada_b=ada_b, norm_mix_g=norm_mix_g.astype(F32), norm_ffn_g=norm_ffn_g.astype(F32),
        router_w=router_w, router_b=router_b,
        moe_wg=moe_w_gate.astype(BF16), moe_wu=moe_w_up.astype(BF16), moe_wd=moe_w_down.astype(BF16),
        ab_w_in=ab_w_in[0].astype(BF16), ab_w_out=ab_w_out[0].astype(BF16),
        na_q_g=na_q_g[0], na_k_g=na_k_g[0], att_bias=_att_bias_table(na_rpb[0]),
        cv_dw_w=cv_dw_w[0], cv_dw_b=cv_dw_b[0], cv_ln_g=cv_ln_g[0], cv_ln_b=cv_ln_b[0],
        cd_w_u_t=cd_w_in[0][:, :W_SSM].T.astype(BF16), cd_w_z=cd_w_in[0][:, W_SSM:].astype(BF16),
        cd_w_out=cd_w_out[0].astype(BF16),
        s5_glu_w=s5_glu_w[0].astype(BF16),
        hy_short_w=hy_short_w[0], hy_short_b=hy_short_b[0],
        hy_mlp=(hy_w1[0], hy_b1[0], hy_w2[0], hy_b2[0], hy_w3[0], hy_freq[0], hy_decay[0]),
        hy_bias=hy_bias[0],
    )
    xs = (x_prompt, x_sample)
    mids = [_trunk_layer0(x, c, wts) for x, c in zip(xs, (c_prompt, c_sample))]
    yts = s5_scan([m[2] for m in mids], [x.shape[1] for x in xs], *s5_tabs)
    return tuple(_trunk_layer1(m[0], m[1], yt, m[3], x.shape[0], x.shape[1], wts)
                 for m, yt, x in zip(mids, yts, xs))
```

```python
import functools
import math

import numpy as np
import jax
import jax.numpy as jnp
from jax import lax
from jax.experimental import pallas as pl
from jax.experimental.pallas import tpu as pltpu

F32 = jnp.float32
BF16 = jnp.bfloat16
HIGHEST = lax.Precision.HIGHEST

D_MODEL = 1024
DEPTH = 2
GRID_W = 64
W_ATT = 512
W_CONV = 512
W_SSM = 512
W_HYENA = 512
HEAD_DIM = 64
N_HEADS = 8
WIN_R = 8
WIN_C = 16
CONV_W = 31
S5_GROUP = 16
S5_GROUPS = 32
S5_STATE = 64
HY_BANDS = 8
N_EXPERTS = 16
N_GROUPS = 4
D_FF = 512
EPS = 1e-6
NEG_INF = -1e30

VMEM_LIMIT_BYTES = 56 * 1024 * 1024
LANES = 128

ATT_QROWS = 4
ATT_KROWS = 12
S5_CHUNK = LANES
FFT_N2 = 128


def _cparams(sem):
    return pltpu.CompilerParams(dimension_semantics=sem, vmem_limit_bytes=VMEM_LIMIT_BYTES)


def _mod_kernel(c_ref, w_ref, b_ref, o_ref):
    c = c_ref[...]
    s = c * jax.nn.sigmoid(c)
    o_ref[0] = jnp.dot(s, w_ref[0], precision=HIGHEST, preferred_element_type=F32) + b_ref[0]


def adaln_mod(c, ada_w, ada_b):
    bsz = c.shape[0]
    tn = D_MODEL
    out = pl.pallas_call(
        _mod_kernel,
        grid=(DEPTH, 6 * D_MODEL // tn),
        in_specs=[
            pl.BlockSpec((bsz, D_MODEL), lambda i, j: (0, 0)),
            pl.BlockSpec((1, D_MODEL, tn), lambda i, j: (i, 0, j)),
            pl.BlockSpec((1, 1, tn), lambda i, j: (i, 0, j)),
        ],
        out_specs=pl.BlockSpec((1, bsz, tn), lambda i, j: (i, 0, j)),
        out_shape=jax.ShapeDtypeStruct((DEPTH, bsz, 6 * D_MODEL), F32),
        compiler_params=_cparams(("arbitrary", "arbitrary")),
        name="adaln_mod",
    )(c, ada_w, ada_b.reshape(DEPTH, 1, 6 * D_MODEL))
    return out.reshape(DEPTH, bsz, 6, D_MODEL)


def _norm_mod(x, g, shift, scale):
    ms = jnp.mean(x * x, axis=-1, keepdims=True)
    return x * lax.rsqrt(ms + EPS) * g * (1.0 + scale) + shift


def _in0_kernel(x_ref, mod_ref, g_ref, w_ref, hm_ref, qg_ref, kg_ref, q_o, k_o, v_o, u_o):
    m = mod_ref[0]
    h = _norm_mod(x_ref[...], g_ref[...], m[0:1], m[1:2])
    p = jnp.dot(h.astype(BF16), w_ref[...], preferred_element_type=F32)
    q = p[:, 0:W_ATT]
    k = p[:, W_ATT:2 * W_ATT]
    qms = jnp.dot((q * q).astype(BF16), hm_ref[...], preferred_element_type=F32)
    kms = jnp.dot((k * k).astype(BF16), hm_ref[...], preferred_element_type=F32)
    q_o[...] = (q * lax.rsqrt(qms + EPS) * qg_ref[...]).astype(BF16)
    k_o[...] = (k * lax.rsqrt(kms + EPS) * kg_ref[...]).astype(BF16)
    v_o[...] = p[:, 2 * W_ATT:3 * W_ATT].astype(BF16)
    val = p[:, 3 * W_ATT:3 * W_ATT + W_CONV]
    gate = p[:, 3 * W_ATT + W_CONV:]
    u_o[...] = (val * jax.nn.sigmoid(gate)).astype(BF16)


def in_proj0(x, mod, g, w_in_bf, q_gain, k_gain, seq, tm=512):
    n = x.shape[0]
    head_mean = jnp.asarray(np.kron(np.eye(N_HEADS), np.full((HEAD_DIM, HEAD_DIM), 1.0 / HEAD_DIM)), BF16)
    qg = (jnp.tile(q_gain.astype(F32), N_HEADS) * (HEAD_DIM ** -0.5)).reshape(1, W_ATT)
    kg = jnp.tile(k_gain.astype(F32), N_HEADS).reshape(1, W_ATT)
    tok = lambda i: (i, 0)
    fixed = lambda i: (0, 0)
    osd = jax.ShapeDtypeStruct((n, W_ATT), BF16)
    return pl.pallas_call(
        _in0_kernel,
        grid=(n // tm,),
        in_specs=[
            pl.BlockSpec((tm, D_MODEL), tok),
            pl.BlockSpec((1, 6, D_MODEL), lambda i: ((i * tm) // seq, 0, 0)),
            pl.BlockSpec((1, D_MODEL), fixed),
            pl.BlockSpec(w_in_bf.shape, fixed),
            pl.BlockSpec((W_ATT, W_ATT), fixed),
            pl.BlockSpec((1, W_ATT), fixed),
            pl.BlockSpec((1, W_ATT), fixed),
        ],
        out_specs=[pl.BlockSpec((tm, W_ATT), tok)] * 4,
        out_shape=[osd] * 4,
        compiler_params=_cparams(("parallel",)),
        name="in_proj0",
    )(x, mod, g.reshape(1, D_MODEL), w_in_bf, head_mean, qg, kg)


def _att_bias_table(rpb):
    a = np.arange(ATT_QROWS)[:, None, None, None]
    c = np.arange(GRID_W)[None, :, None, None]
    e = np.arange(ATT_KROWS)[None, None, :, None]
    kc = np.arange(GRID_W)[None, None, None, :]
    c0 = np.clip(c - WIN_C // 2, 0, GRID_W - WIN_C)
    col_ok = (kc >= c0) & (kc < c0 + WIN_C)
    rpb = rpb.astype(F32)
    per = 2 * GRID_W
    vrow = jnp.concatenate([rpb[..., WIN_C - 1:],
                            jnp.zeros(rpb.shape[:2] + (per - (2 * WIN_C - 1),), F32),
                            rpb[..., :WIN_C - 1]], axis=-1)
    tcol = jnp.tile(vrow, (1, 1, GRID_W))[..., :GRID_W * (per - 1)]
    tcol = tcol.reshape(rpb.shape[:2] + (GRID_W, per - 1))[..., :GRID_W]
    tables = []
    for case in range(3):
        if case == 0:
            dr = e - a
            row_ok = (e >= 0) & (e < WIN_R)
        elif case == 1:
            dr = e - a - WIN_R // 2
            row_ok = (dr >= -(WIN_R // 2)) & (dr < WIN_R // 2)
        else:
            dr = e - a - (ATT_KROWS - ATT_QROWS)
            row_ok = (e >= ATT_KROWS - WIN_R) & (e < ATT_KROWS)
        ok = np.broadcast_to(row_ok & col_ok, (ATT_QROWS, GRID_W, ATT_KROWS, GRID_W))
        dri = np.clip(dr + WIN_R - 1, 0, 2 * WIN_R - 2)[:, 0, :, 0]
        nq, nk = ATT_QROWS * GRID_W, ATT_KROWS * GRID_W
        t = jnp.concatenate([jnp.concatenate([tcol[:, int(dri[qa, ke])] for ke in range(ATT_KROWS)], axis=-1)
                             for qa in range(ATT_QROWS)], axis=-2)
        tables.append(jnp.where(jnp.asarray(ok.reshape(nq, nk)), t, NEG_INF))
    return jnp.stack(tables).astype(BF16)


def _att_kernel(q_ref, k0, k1, k2, v0, v1, v2, bias_ref, o_ref):
    kt = [k0, k1, k2]
    vt = [v0, v1, v2]
    nkb = len(kt)
    kw = k0.shape[0]
    tq = q_ref.shape[0]
    first = lax.broadcasted_iota(jnp.int32, (tq, LANES), 1) < HEAD_DIM
    for hp in range(N_HEADS // 2):
        ps = slice(hp * LANES, (hp + 1) * LANES)
        qp = q_ref[:, ps]
        res = []
        for sub in range(2):
            h = 2 * hp + sub
            qm = jnp.where(first if sub == 0 else jnp.logical_not(first), qp, jnp.zeros_like(qp))
            s = [lax.dot_general(qm, kt[j][:, ps], (((1,), (1,)), ((), ())), preferred_element_type=F32)
                 + bias_ref[0, h, :, j * kw:(j + 1) * kw].astype(F32) for j in range(nkb)]
            m = s[0].max(axis=-1, keepdims=True)
            for j in range(1, nkb):
                m = jnp.maximum(m, s[j].max(axis=-1, keepdims=True))
            p = [jnp.exp(sj - m) for sj in s]
            l = p[0].sum(axis=-1, keepdims=True)
            for j in range(1, nkb):
                l = l + p[j].sum(axis=-1, keepdims=True)
            o = jnp.dot(p[0].astype(BF16), vt[0][:, ps], preferred_element_type=F32)
            for j in range(1, nkb):
                o = o + jnp.dot(p[j].astype(BF16), vt[j][:, ps], preferred_element_type=F32)
            res.append(o / l)
        o_ref[:, ps] = jnp.where(first, res[0], res[1]).astype(BF16)


def neighbourhood_attention(q, k, v, bias, bsz, seq):
    rows = seq // GRID_W
    nqb = rows // ATT_QROWS
    tq = ATT_QROWS * GRID_W
    nkb = ATT_KROWS // ATT_QROWS
    assert rows % ATT_QROWS == 0 and nqb >= nkb

    def kmap(j):
        def f(i):
            b, r = i // nqb, i % nqb
            return (b * nqb + jnp.clip(r - 1, 0, nqb - nkb) + j, 0)
        return f

    def bias_map(i):
        r = i % nqb
        return (jnp.where(r == 0, 0, jnp.where(r == nqb - 1, 2, 1)), 0, 0, 0)

    kv_specs = [pl.BlockSpec((tq, W_ATT), kmap(j)) for j in range(nkb)]
    return pl.pallas_call(
        _att_kernel,
        grid=(bsz * nqb,),
        in_specs=[pl.BlockSpec((tq, W_ATT), lambda i: (i, 0))] + kv_specs + kv_specs
        + [pl.BlockSpec((1,) + bias.shape[1:], bias_map)],
        out_specs=pl.BlockSpec((tq, W_ATT), lambda i: (i, 0)),
        out_shape=jax.ShapeDtypeStruct(q.shape, BF16),
        compiler_params=_cparams(("parallel",)),
        name="neighbourhood_attention",
    )(q, k, k, k, v, v, v, bias)


CONV_HALO = 16


def _conv_kernel(prev_ref, cur_ref, next_ref, w_ref, b_ref, g_ref, be_ref, o_ref, win_ref, *, tiles_per_seq):
    i = pl.program_id(0)
    t = i % tiles_per_seq
    tt = cur_ref.shape[0]
    half = CONV_W // 2
    prev = prev_ref[...].astype(F32)
    nxt = next_ref[...].astype(F32)
    win_ref[0:CONV_HALO, :] = jnp.where(t == 0, 0.0, prev)
    win_ref[CONV_HALO:CONV_HALO + tt, :] = cur_ref[...].astype(F32)
    win_ref[CONV_HALO + tt:, :] = jnp.where(t == tiles_per_seq - 1, 0.0, nxt)
    w = w_ref[...]
    acc = jnp.zeros((tt, W_CONV), F32) + b_ref[...]
    for kk in range(CONV_W):
        off = CONV_HALO - half + kk
        acc = acc + win_ref[off:off + tt, :] * w[kk:kk + 1, :]
    mu = jnp.mean(acc, axis=-1, keepdims=True)
    d = acc - mu
    var = jnp.mean(d * d, axis=-1, keepdims=True)
    y = d * lax.rsqrt(var + EPS) * g_ref[...] + be_ref[...]
    o_ref[...] = (y * jax.nn.sigmoid(y)).astype(BF16)


def conv_module(u, dw_w, dw_b, ln_g, ln_b, seq, tt=512):
    n = u.shape[0]
    tps = seq // tt
    hb = tt // CONV_HALO
    nhb = n // CONV_HALO
    row = lambda a: a.astype(F32).reshape(1, W_CONV)
    fixed = lambda i: (0, 0)
    return pl.pallas_call(
        functools.partial(_conv_kernel, tiles_per_seq=tps),
        grid=(n // tt,),
        in_specs=[
            pl.BlockSpec((CONV_HALO, W_CONV), lambda i: (jnp.maximum(i * hb - 1, 0), 0)),
            pl.BlockSpec((tt, W_CONV), lambda i: (i, 0)),
            pl.BlockSpec((CONV_HALO, W_CONV), lambda i: (jnp.minimum((i + 1) * hb, nhb - 1), 0)),
            pl.BlockSpec((CONV_W, W_CONV), fixed),
            pl.BlockSpec((1, W_CONV), fixed),
            pl.BlockSpec((1, W_CONV), fixed),
            pl.BlockSpec((1, W_CONV), fixed),
        ],
        out_specs=pl.BlockSpec((tt, W_CONV), lambda i: (i, 0)),
        out_shape=jax.ShapeDtypeStruct((n, W_CONV), BF16),
        scratch_shapes=[pltpu.VMEM((tt + 2 * CONV_HALO, W_CONV), F32)],
        compiler_params=_cparams(("parallel",)),
        name="conv_module",
    )(u, u, u, dw_w.astype(F32), row(dw_b), row(ln_g), row(ln_b))


def _top2_sum(a, b, c, d):
    hi1, lo1 = jnp.maximum(a, b), jnp.minimum(a, b)
    hi2, lo2 = jnp.maximum(c, d), jnp.minimum(c, d)
    return jnp.maximum(hi1, hi2) + jnp.maximum(jnp.minimum(hi1, hi2), jnp.maximum(lo1, lo2))


def _router_gates(scores_t, bias_t):
    per = N_EXPERTS // N_GROUPS
    rows = [scores_t[e:e + 1, :] + bias_t[e:e + 1, :] for e in range(N_EXPERTS)]
    gscore = [_top2_sum(*rows[g * per:(g + 1) * per]) for g in range(N_GROUPS)]
    best = gscore[0]
    best_idx = jnp.zeros_like(best, dtype=jnp.int32)
    for g in range(1, N_GROUPS):
        better = gscore[g] > best
        best = jnp.where(better, gscore[g], best)
        best_idx = jnp.where(better, g, best_idx)
    out_row = lax.broadcasted_iota(jnp.int32, (LANES, scores_t.shape[1]), 0)
    gates = jnp.zeros((LANES, scores_t.shape[1]), F32)
    total = jnp.zeros_like(best)
    for e in range(N_EXPERTS):
        g = e // per
        rank = jnp.zeros_like(best_idx)
        for e2 in range(g * per, (g + 1) * per):
            if e2 == e:
                continue
            if e2 < e:
                ahead = rows[e2] >= rows[e]
            else:
                ahead = rows[e2] > rows[e]
            rank = rank + jnp.where(ahead, 1, 0)
        chosen = jnp.where(best_idx == g, rank, 2) < 2
        gated = jnp.where(chosen, scores_t[e:e + 1, :], 0.0)
        total = total + gated
        gates = jnp.where(out_row == e, gated, gates)
    return gates / total


def _out_kernel(a_ref, b_ref, x_ref, mod_ref, wa_ref, wb_ref, g_ref, rwh_ref, rwl_ref, rb_ref, x_o, h_o, gate_o,
                gate_t_o):
    m = mod_ref[0]
    mix = (jnp.dot(a_ref[...], wa_ref[...], preferred_element_type=F32)
           + jnp.dot(b_ref[...], wb_ref[...], preferred_element_type=F32))
    x = x_ref[...] + m[2:3] * mix
    x_o[...] = x
    h = _norm_mod(x, g_ref[...], m[3:4], m[4:5])
    h_hi = h.astype(BF16)
    h_o[...] = h_hi
    h_lo = (h - h_hi.astype(F32)).astype(BF16)
    logits = (jnp.dot(h_hi, rwh_ref[...], preferred_element_type=F32)
              + jnp.dot(h_lo, rwh_ref[...], preferred_element_type=F32)
              + jnp.dot(h_hi, rwl_ref[...], preferred_element_type=F32))
    scores_t = jax.nn.sigmoid(logits).T
    gates_t = _router_gates(scores_t, rb_ref[...])
    gate_t_o[...] = gates_t
    gate_o[...] = gates_t.T


def out_proj_router(a, b, x, mod, w_out_bf, g_ffn, router_w, router_b, seq, tm=512):
    n = x.shape[0]
    half = a.shape[1]
    tok = lambda i: (i, 0)
    fixed = lambda i: (0, 0)
    rw = jnp.pad(router_w.astype(F32), ((0, 0), (0, LANES - N_EXPERTS)))
    rw_hi = rw.astype(BF16)
    rw_lo = (rw - rw_hi.astype(F32)).astype(BF16)
    return pl.pallas_call(
        _out_kernel,
        grid=(n // tm,),
        in_specs=[
            pl.BlockSpec((tm, half), tok),
            pl.BlockSpec((tm, half), tok),
            pl.BlockSpec((tm, D_MODEL), tok),
            pl.BlockSpec((1, 6, D_MODEL), lambda i: ((i * tm) // seq, 0, 0)),
            pl.BlockSpec((half, D_MODEL), lambda i: (0, 0)),
            pl.BlockSpec((half, D_MODEL), lambda i: (1, 0)),
            pl.BlockSpec((1, D_MODEL), fixed),
            pl.BlockSpec((D_MODEL, LANES), fixed),
            pl.BlockSpec((D_MODEL, LANES), fixed),
            pl.BlockSpec((N_EXPERTS, 1), fixed),
        ],
        out_specs=[pl.BlockSpec((tm, D_MODEL), tok), pl.BlockSpec((tm, D_MODEL), tok),
                   pl.BlockSpec((tm, LANES), tok), pl.BlockSpec((LANES, tm), lambda i: (0, i))],
        out_shape=[jax.ShapeDtypeStruct((n, D_MODEL), F32), jax.ShapeDtypeStruct((n, D_MODEL), BF16),
                   jax.ShapeDtypeStruct((n, LANES), F32), jax.ShapeDtypeStruct((LANES, n), F32)],
        compiler_params=_cparams(("parallel",)),
        name="out_proj_router",
    )(a, b, x, mod, w_out_bf, w_out_bf, g_ffn.reshape(1, D_MODEL), rw_hi, rw_lo,
      router_b.astype(F32).reshape(N_EXPERTS, 1))


MOE_COLS = 256
EXPERTS_PER_GROUP = N_EXPERTS // N_GROUPS


def _moe_kernel(h_ref, gate_ref, gate_t_ref, x_ref, mod_ref, tri_ref, wg_ref, wu_ref, wd_ref, o_ref,
                rankc_ref, rankr_ref, ht_ref, xg_ref, gg_ref, yg_ref, acc_ref, cnt_ref):
    e = pl.program_id(1)
    g = e // EXPERTS_PER_GROUP
    j = e % EXPERTS_PER_GROUP
    T = h_ref.shape[0]
    R = MOE_COLS

    @pl.when(e == 0)
    def _():
        er = lax.broadcasted_iota(jnp.int32, (LANES, LANES), 0)
        ec = lax.broadcasted_iota(jnp.int32, (LANES, LANES), 1)
        sel_c = jnp.where((er < N_EXPERTS) & (er // EXPERTS_PER_GROUP == ec), 1.0, 0.0).astype(BF16)
        sel_r = jnp.where((ec < N_EXPERTS) & (ec // EXPERTS_PER_GROUP == er), 1.0, 0.0).astype(BF16)
        chosen_c = jnp.where(gate_ref[...] > 0.0, 1.0, 0.0).astype(BF16)
        memb_c = jnp.dot(chosen_c, sel_c, preferred_element_type=F32) > 0.0
        rank_c = jnp.dot(tri_ref[...], jnp.where(memb_c, 1.0, 0.0).astype(BF16), preferred_element_type=F32)
        rankc_ref[...] = jnp.where(memb_c, rank_c, -1.0)
        chosen_r = jnp.where(gate_t_ref[...] > 0.0, 1.0, 0.0).astype(BF16)
        memb_r = jnp.dot(sel_r, chosen_r, preferred_element_type=F32) > 0.0
        ones_r = jnp.where(memb_r, 1.0, 0.0)
        rank_r = lax.dot_general(ones_r.astype(BF16), tri_ref[...], (((1,), (1,)), ((), ())),
                                 preferred_element_type=F32)
        rankr_ref[...] = jnp.where(memb_r, rank_r, -1.0)
        for gi in range(N_GROUPS):
            cnt_ref[gi] = jnp.sum(ones_r[gi:gi + 1, :]).astype(jnp.int32)
        ht_ref[...] = h_ref[...].astype(F32).T.astype(BF16)
        acc_ref[...] = jnp.zeros_like(acc_ref)

    nch = (cnt_ref[g] + (R - 1)) // R

    @pl.when(j == 0)
    def _():
        lane_t = lax.broadcasted_iota(jnp.int32, (T, LANES), 1)
        rc = jnp.sum(jnp.where(lane_t == g, rankc_ref[...], 0.0), axis=1, keepdims=True)
        col = lax.broadcasted_iota(jnp.int32, (T, R), 1).astype(F32)
        gt = gate_t_ref[...]
        g_hi = gt.astype(BF16)
        g_lo = (gt - g_hi.astype(F32)).astype(BF16)

        def gather(c, carry):
            pt = jnp.where(rc - (c * R).astype(F32) == col, 1.0, 0.0).astype(BF16)
            xg_ref[c] = jnp.dot(ht_ref[...], pt, preferred_element_type=F32).astype(BF16)
            gg_ref[c] = (jnp.dot(g_hi, pt, preferred_element_type=F32)
                         + jnp.dot(g_lo, pt, preferred_element_type=F32))
            yg_ref[c] = jnp.zeros((D_MODEL, R), F32)
            return carry

        lax.fori_loop(0, nch, gather, 0)

    def ffn(c, carry):
        xc = xg_ref[c]
        a = jnp.dot(wg_ref[0], xc, preferred_element_type=F32)
        u = jnp.dot(wu_ref[0], xc, preferred_element_type=F32)
        hid = (a * jax.nn.sigmoid(a)) * u * gg_ref[c, pl.ds(e, 1), :]
        yg_ref[c] += jnp.dot(wd_ref[0], hid.astype(BF16), preferred_element_type=F32)
        return carry

    lax.fori_loop(0, nch, ffn, 0)

    @pl.when(j == EXPERTS_PER_GROUP - 1)
    def _():
        rr = rankr_ref[pl.ds(g, 1), :]
        row = lax.broadcasted_iota(jnp.int32, (R, T), 0).astype(F32)

        def scatter(c, carry):
            p = jnp.where(rr - (c * R).astype(F32) == row, 1.0, 0.0).astype(BF16)
            acc_ref[...] += jnp.dot(yg_ref[c].astype(BF16), p, preferred_element_type=F32)
            return carry

        lax.fori_loop(0, nch, scatter, 0)

    @pl.when(e == N_EXPERTS - 1)
    def _():
        o_ref[...] = x_ref[...] + mod_ref[0][5:6] * acc_ref[...].T


def moe_ffn(h, gate, gate_t, x, mod, wgt_bf, wut_bf, wdt_bf, seq, tm=1024):
    n = x.shape[0]
    tm = min(tm, seq)
    nch = tm // MOE_COLS
    tok = lambda i, e: (i, 0)
    tri = jnp.asarray(np.tril(np.ones((tm, tm), np.float32), -1), BF16)
    return pl.pallas_call(
        _moe_kernel,
        grid=(n // tm, N_EXPERTS),
        in_specs=[
            pl.BlockSpec((tm, D_MODEL), tok),
            pl.BlockSpec((tm, LANES), tok),
            pl.BlockSpec((LANES, tm), lambda i, e: (0, i)),
            pl.BlockSpec((tm, D_MODEL), tok),
            pl.BlockSpec((1, 6, D_MODEL), lambda i, e: ((i * tm) // seq, 0, 0)),
            pl.BlockSpec((tm, tm), lambda i, e: (0, 0)),
            pl.BlockSpec((1, D_FF, D_MODEL), lambda i, e: (e, 0, 0)),
            pl.BlockSpec((1, D_FF, D_MODEL), lambda i, e: (e, 0, 0)),
            pl.BlockSpec((1, D_MODEL, D_FF), lambda i, e: (e, 0, 0)),
        ],
        out_specs=pl.BlockSpec((tm, D_MODEL), tok),
        out_shape=jax.ShapeDtypeStruct((n, D_MODEL), F32),
        scratch_shapes=[pltpu.VMEM((tm, LANES), F32), pltpu.VMEM((LANES, tm), F32),
                        pltpu.VMEM((D_MODEL, tm), BF16), pltpu.VMEM((nch, D_MODEL, MOE_COLS), BF16),
                        pltpu.VMEM((nch, LANES, MOE_COLS), F32), pltpu.VMEM((nch, D_MODEL, MOE_COLS), F32),
                        pltpu.VMEM((D_MODEL, tm), F32), pltpu.SMEM((N_GROUPS,), jnp.int32)],
        compiler_params=_cparams(("parallel", "arbitrary")),
        name="moe_ffn",
    )(h, gate, gate_t, x, mod, tri, wgt_bf, wut_bf, wdt_bf)


def _in1_kernel(x_ref, mod_ref, g_ref, wut_ref, wz_ref, ut_o, hz_o):
    m = mod_ref[0]
    h = _norm_mod(x_ref[...], g_ref[...], m[0:1], m[1:2]).astype(BF16)
    ut_o[...] = lax.dot_general(wut_ref[...], h, (((1,), (1,)), ((), ())),
                                preferred_element_type=F32).astype(BF16)
    hz_o[...] = jnp.dot(h, wz_ref[...], preferred_element_type=F32).astype(BF16)


def in_proj1(x, mod, g, w_u_t_bf, w_z_bf, seq, tm=512):
    n = x.shape[0]
    fixed = lambda i: (0, 0)
    return pl.pallas_call(
        _in1_kernel,
        grid=(n // tm,),
        in_specs=[
            pl.BlockSpec((tm, D_MODEL), lambda i: (i, 0)),
            pl.BlockSpec((1, 6, D_MODEL), lambda i: ((i * tm) // seq, 0, 0)),
            pl.BlockSpec((1, D_MODEL), fixed),
            pl.BlockSpec(w_u_t_bf.shape, fixed),
            pl.BlockSpec(w_z_bf.shape, fixed),
        ],
        out_specs=[pl.BlockSpec((W_SSM, tm), lambda i: (0, i)),
                   pl.BlockSpec((tm, 3 * W_HYENA), lambda i: (i, 0))],
        out_shape=[jax.ShapeDtypeStruct((W_SSM, n), BF16), jax.ShapeDtypeStruct((n, 3 * W_HYENA), BF16)],
        compiler_params=_cparams(("parallel",)),
        name="in_proj1",
    )(x, mod, g.reshape(1, D_MODEL), w_u_t_bf, w_z_bf)


def s5_tables(lam_re, lam_im, log_dt, b_re, b_im, c_re, c_im, d_skip):
    T = S5_CHUNK
    f32 = F32
    lags = jnp.arange(T, dtype=f32)

    def disc(d):
        lr, li = lam_re[d].astype(f32), lam_im[d].astype(f32)
        dt = jnp.exp(log_dt[d].astype(f32))[:, None]
        mag = jnp.exp(lr * dt)
        ab_re, ab_im = mag * jnp.cos(li * dt), mag * jnp.sin(li * dt)
        den = lr * lr + li * li
        f_re = ((ab_re - 1.0) * lr + ab_im * li) / den
        f_im = (ab_im * lr - (ab_re - 1.0) * li) / den
        br, bi = b_re[d].astype(f32), b_im[d].astype(f32)
        bb_re = f_re[..., None] * br - f_im[..., None] * bi
        bb_im = f_re[..., None] * bi + f_im[..., None] * br

        def power(p):
            ang = li * dt
            mg = jnp.exp(p[:, None, None] * (lr * dt)[None])
            return mg * jnp.cos(p[:, None, None] * ang[None]), mg * jnp.sin(p[:, None, None] * ang[None])
        return bb_re, bb_im, c_re[d].astype(f32), c_im[d].astype(f32), power

    hp = dict(precision=HIGHEST)
    tabs = []
    for d in range(2):
        bb_re, bb_im, cr, ci, power = disc(d)
        pr, pi = power(lags)
        cb_rr = jnp.einsum('gon,lgn,gni->lgoi', cr, pr, bb_re, **hp)
        cb_ii = jnp.einsum('gon,lgn,gni->lgoi', cr, pi, bb_im, **hp)
        cb_ri = jnp.einsum('gon,lgn,gni->lgoi', ci, pr, bb_im, **hp)
        cb_ir = jnp.einsum('gon,lgn,gni->lgoi', ci, pi, bb_re, **hp)
        kern = cb_rr - cb_ii - cb_ri - cb_ir
        qp = (T - 1.0 - lags) if d == 0 else lags
        qr, qi = power(qp)
        wp_re = jnp.einsum('tgn,gni->gitn', qr, bb_re) - jnp.einsum('tgn,gni->gitn', qi, bb_im)
        wp_im = jnp.einsum('tgn,gni->gitn', qr, bb_im) + jnp.einsum('tgn,gni->gitn', qi, bb_re)
        rp = (lags + 1.0) if d == 0 else (T - lags)
        rr, ri = power(rp)
        m_re = jnp.einsum('gon,tgn->gnot', cr, rr) - jnp.einsum('gon,tgn->gnot', ci, ri)
        m_im = jnp.einsum('gon,tgn->gnot', cr, ri) + jnp.einsum('gon,tgn->gnot', ci, rr)
        levels = 2.0 ** jnp.arange(16, dtype=f32) * T
        ar, ai = power(levels)
        tabs.append((kern, wp_re, wp_im, m_re, -m_im, ar, ai))

    kf, kb = tabs[0][0], tabs[1][0]
    skip = jnp.eye(S5_GROUP, dtype=f32)[None] * d_skip.astype(f32).reshape(S5_GROUPS, S5_GROUP, 1)
    k0 = kf[0] + kb[0] + skip
    kk = jnp.concatenate([k0[None], kf[1:], jnp.zeros_like(k0)[None], kb[:0:-1]], axis=0)
    kk = jnp.transpose(kk, (1, 3, 2, 0))
    g_, i_, o_ = kk.shape[:3]
    kk = kk.reshape(g_, i_ * o_, 2 * T)
    w_state = jnp.concatenate([tabs[0][1], tabs[0][2], tabs[1][1], tabs[1][2]], axis=-1)
    wp = w_state.reshape(g_, i_ * T, 4 * S5_STATE).astype(BF16)
    wc = jnp.concatenate([tabs[0][3], tabs[0][4], tabs[1][3], tabs[1][4]], axis=1)
    wc = wc.reshape(g_, 4 * S5_STATE, o_ * T).astype(BF16)
    mult = []
    for d in range(2):
        ar, ai = tabs[d][5], tabs[d][6]
        mult += [jnp.concatenate([ar, ar], axis=-1), jnp.concatenate([-ai, ai], axis=-1)]
    scan_mult = jnp.transpose(jnp.stack(mult, axis=2), (1, 0, 2, 3))
    return kk, wp, wc, scan_mult


def _s5_kernel(*refs, chunks_per_seq):
    nt = len(chunks_per_seq)
    u_refs, (kk_ref, wp_ref, wc_ref, mult_ref) = refs[:nt], refs[nt:nt + 4]
    o_refs, w_ref = refs[nt + 4:2 * nt + 4], refs[2 * nt + 4]
    T = S5_CHUNK

    def build(i, carry):
        for o in range(S5_GROUP):
            row = kk_ref[0, pl.ds(i * S5_GROUP + o, 1), :]
            toe = pltpu.roll(jnp.broadcast_to(row, (T, 2 * T)), 0, axis=1, stride=1, stride_axis=0)
            w_ref[pl.ds(pl.multiple_of(i * T, T), T), o * T:(o + 1) * T] = toe[:, :T].astype(BF16)
        return carry

    lax.fori_loop(0, S5_GROUP, build, 0)
    for u_ref, o_ref, cps in zip(u_refs, o_refs, chunks_per_seq):
        _s5_apply(u_ref, o_ref, w_ref, wp_ref, wc_ref, mult_ref, cps)


def _s5_apply(u_ref, o_ref, w_ref, wp_ref, wc_ref, mult_ref, chunks_per_seq):
    T = S5_CHUNK
    ns = 2 * S5_STATE
    x = jnp.concatenate([u_ref[i] for i in range(S5_GROUP)], axis=1)
    y = jnp.dot(x, w_ref[...], preferred_element_type=F32)
    pst = jnp.dot(x, wp_ref[0], preferred_element_type=F32)
    nc = y.shape[0]
    cidx = lax.broadcasted_iota(jnp.int32, (nc, ns), 0) % chunks_per_seq

    def cmul(s, mre, mim):
        return s * mre + pltpu.roll(s, S5_STATE, axis=1) * mim

    def scan(p, d):
        s = p
        k, step = 0, 1
        while step < chunks_per_seq:
            mre = mult_ref[0, k, 2 * d:2 * d + 1, :]
            mim = mult_ref[0, k, 2 * d + 1:2 * d + 2, :]
            if d == 0:
                sh = jnp.where(cidx >= step, pltpu.roll(s, step, axis=0), 0.0)
            else:
                sh = jnp.where(cidx < chunks_per_seq - step, pltpu.roll(s, nc - step, axis=0), 0.0)
            s = s + cmul(sh, mre, mim)
            k, step = k + 1, step * 2
        if d == 0:
            return jnp.where(cidx >= 1, pltpu.roll(s, 1, axis=0), 0.0)
        return jnp.where(cidx < chunks_per_seq - 1, pltpu.roll(s, nc - 1, axis=0), 0.0)

    sf = scan(pst[:, :ns], 0)
    sb = scan(pst[:, ns:], 1)
    carry = jnp.concatenate([sf, sb], axis=1).astype(BF16)
    ytot = y + jnp.dot(carry, wc_ref[0], preferred_element_type=F32)
    for o in range(S5_GROUP):
        o_ref[o] = ytot[:, o * T:(o + 1) * T]


def s5_scan(uts, seqs, kk, wp, wc, scan_mult):
    T = S5_CHUNK
    u3 = [ut.reshape(W_SSM, ut.shape[1] // T, T) for ut in uts]
    blk = [pl.BlockSpec((S5_GROUP,) + u.shape[1:], lambda g: (g, 0, 0)) for u in u3]
    grp = lambda a: pl.BlockSpec((1,) + a.shape[1:], lambda g: (g,) + (0,) * (a.ndim - 1))
    outs = pl.pallas_call(
        functools.partial(_s5_kernel, chunks_per_seq=tuple(s // T for s in seqs)),
        grid=(S5_GROUPS,),
        in_specs=blk + [grp(kk), grp(wp), grp(wc), grp(scan_mult)],
        out_specs=blk,
        out_shape=[jax.ShapeDtypeStruct(u.shape, F32) for u in u3],
        scratch_shapes=[pltpu.VMEM((S5_GROUP * T, S5_GROUP * T), BF16)],
        compiler_params=_cparams(("parallel",)),
        name="s5_scan",
    )(*u3, kk, wp, wc, scan_mult)
    return [o.reshape(W_SSM, -1) for o in outs]


def _s5_post_kernel(yt_ref, w_ref, o_ref):
    y = yt_ref[...].T
    y = 0.5 * y * (1.0 + jnp.tanh(math.sqrt(2.0 / math.pi) * (y + 0.044715 * (y * y * y))))
    z = jnp.dot(y.astype(BF16), w_ref[...], preferred_element_type=F32)
    o_ref[...] = (y * jax.nn.sigmoid(z)).astype(BF16)


def s5_post(yt, glu_w_bf, tm=512):
    n = yt.shape[1]
    return pl.pallas_call(
        _s5_post_kernel,
        grid=(n // tm,),
        in_specs=[pl.BlockSpec((W_SSM, tm), lambda i: (0, i)),
                  pl.BlockSpec((W_SSM, W_SSM), lambda i: (0, 0))],
        out_specs=pl.BlockSpec((tm, W_SSM), lambda i: (i, 0)),
        out_shape=jax.ShapeDtypeStruct((n, W_SSM), BF16),
        compiler_params=_cparams(("parallel",)),
        name="s5_post",
    )(yt, glu_w_bf)


SHORT_HALO = 16


def _short_kernel(prev_ref, cur_ref, next_ref, w_ref, b_ref, v_o, x1_o, x2_o, *, tiles_per_seq):
    i = pl.program_id(0)
    t = i % tiles_per_seq
    cur = cur_ref[...].astype(F32)
    tt = cur.shape[0]
    before = jnp.where(t == 0, 0.0, prev_ref[SHORT_HALO - 1:SHORT_HALO, :].astype(F32))
    after = jnp.where(t == tiles_per_seq - 1, 0.0, next_ref[0:1, :].astype(F32))
    ridx = lax.broadcasted_iota(jnp.int32, cur.shape, 0)
    left = jnp.where(ridx == 0, before, pltpu.roll(cur, 1, axis=0))
    right = jnp.where(ridx == tt - 1, after, pltpu.roll(cur, tt - 1, axis=0))
    w = w_ref[...]
    p = left * w[0:1] + cur * w[1:2] + right * w[2:3] + b_ref[...]
    v_o[...] = p[:, 0:W_HYENA].astype(BF16)
    x1_o[...] = p[:, W_HYENA:2 * W_HYENA].astype(BF16)
    x2_o[...] = p[:, 2 * W_HYENA:].astype(BF16)


def hyena_short_conv(hz, short_w, short_b, seq, tt=512):
    n, c3 = hz.shape
    tps = seq // tt
    hb = tt // SHORT_HALO
    nhb = n // SHORT_HALO
    fixed = lambda i: (0, 0)
    osd = jax.ShapeDtypeStruct((n, W_HYENA), BF16)
    return pl.pallas_call(
        functools.partial(_short_kernel, tiles_per_seq=tps),
        grid=(n // tt,),
        in_specs=[
            pl.BlockSpec((SHORT_HALO, c3), lambda i: (jnp.maximum(i * hb - 1, 0), 0)),
            pl.BlockSpec((tt, c3), lambda i: (i, 0)),
            pl.BlockSpec((SHORT_HALO, c3), lambda i: (jnp.minimum((i + 1) * hb, nhb - 1), 0)),
            pl.BlockSpec((3, c3), fixed),
            pl.BlockSpec((1, c3), fixed),
        ],
        out_specs=[pl.BlockSpec((tt, W_HYENA), lambda i: (i, 0))] * 3,
        out_shape=[osd] * 3,
        compiler_params=_cparams(("parallel",)),
        name="hyena_short_conv",
    )(hz, hz, hz, short_w.astype(F32), short_b.astype(F32).reshape(1, c3))


def _filter_kernel(band_ref, w1_ref, b1_ref, w2_ref, b2_ref, w3_ref, freq_ref, decay_ref, k_o, sum_o, *, l):
    i = pl.program_id(0)
    tr = k_o.shape[0]
    m = i * tr + lax.broadcasted_iota(jnp.int32, (tr, LANES), 0)
    pos = jnp.where(m > l, 2 * l - m, m).astype(F32)
    t = pos / (l - 1)
    ang = ((2.0 * math.pi / l) * pos) * band_ref[...]
    lane = lax.broadcasted_iota(jnp.int32, (tr, LANES), 1)
    feat = jnp.where(lane == 0, t, jnp.where(lane <= HY_BANDS, jnp.cos(ang),
                                              jnp.where(lane <= 2 * HY_BANDS, -jnp.sin(ang), 0.0)))
    fr = freq_ref[...]
    hdn = jnp.sin(fr * (jnp.dot(feat, w1_ref[...], precision=HIGHEST, preferred_element_type=F32) + b1_ref[...]))
    hdn = jnp.sin(fr * (jnp.dot(hdn, w2_ref[...], precision=HIGHEST, preferred_element_type=F32) + b2_ref[...]))
    f = jnp.dot(hdn, w3_ref[0], precision=HIGHEST, preferred_element_type=F32)
    reps = f.shape[1] // LANES
    t_all = jnp.concatenate([t] * reps, axis=1)
    m_all = jnp.concatenate([m] * reps, axis=1)
    f = jnp.where(m_all == l, 0.0, f * jnp.exp(-t_all * jnp.abs(decay_ref[0])))
    k_o[...] = f.astype(BF16)

    @pl.when(i == 0)
    def _():
        sum_o[...] = jnp.zeros_like(sum_o)

    sum_o[...] += jnp.sum(jnp.abs(f), axis=0, keepdims=True)


def hyena_filters(l, w1, b1, w2, b2, w3, freq, decay, tr=512):
    f32 = F32
    n_ord = decay.shape[0]
    hid = w1.shape[1]
    cols = n_ord * W_HYENA
    bands = jnp.linspace(1e-4, HY_BANDS - 1, HY_BANDS, dtype=f32)
    band_row = jnp.zeros((1, LANES), f32).at[0, 1:1 + HY_BANDS].set(bands).at[0, 1 + HY_BANDS:1 + 2 * HY_BANDS].set(bands)
    pad2 = lambda a, r, c: jnp.pad(a.astype(f32), ((0, r - a.shape[0]), (0, c - a.shape[1])))
    row = lambda a: pad2(a.reshape(1, -1), 1, LANES)
    w3h = jnp.transpose(w3.astype(f32).reshape(hid, n_ord, 2, W_HYENA), (2, 0, 1, 3)).reshape(2, hid, cols)
    w3h = jnp.pad(w3h, ((0, 0), (0, LANES - hid), (0, 0)))
    dech = jnp.transpose(decay.astype(f32), (1, 0, 2)).reshape(2, 1, cols)
    nt = 2 * l // tr
    fixed = lambda i: (0, 0)
    half = lambda i: (i // (nt // 2), 0, 0)
    k_un, ksum = pl.pallas_call(
        functools.partial(_filter_kernel, l=l),
        grid=(nt,),
        in_specs=[pl.BlockSpec((1, LANES), fixed), pl.BlockSpec((LANES, LANES), fixed), pl.BlockSpec((1, LANES), fixed),
                  pl.BlockSpec((LANES, LANES), fixed), pl.BlockSpec((1, LANES), fixed),
                  pl.BlockSpec((1, LANES, cols), half), pl.BlockSpec((1, LANES), fixed),
                  pl.BlockSpec((1, 1, cols), half)],
        out_specs=[pl.BlockSpec((tr, cols), lambda i: (i, 0)), pl.BlockSpec((8, cols), fixed)],
        out_shape=[jax.ShapeDtypeStruct((2 * l, cols), BF16), jax.ShapeDtypeStruct((8, cols), f32)],
        compiler_params=_cparams(("arbitrary",)),
        name="hyena_filter",
    )(band_row, pad2(w1, LANES, LANES), row(b1), pad2(w2, LANES, LANES), row(b2), w3h, row(freq), dech)
    return k_un, 1.0 / ksum[0:1]


def _dft_tables(n1, rows_in):
    k1 = np.arange(n1)[:, None].astype(np.float64)
    r = np.arange(rows_in)[None, :].astype(np.float64)
    ang = 2.0 * np.pi * k1 * r / n1
    fwd = np.concatenate([np.cos(ang), -np.sin(ang)], axis=0)
    inv = np.concatenate([np.cos(ang).T, -np.sin(ang).T], axis=1)
    return jnp.asarray(fwd, BF16), jnp.asarray(inv, BF16)


def _mid_tables(n1):
    n2 = FFT_N2
    n = n1 * n2
    k = (np.arange(n1)[:, None, None] + n1 * np.arange(n2)[None, :, None]).astype(np.float64)
    m = np.arange(n2)[None, None, :].astype(np.float64)
    ang = 2.0 * np.pi * ((k * m) % n) / n
    gr, gi = np.cos(ang), -np.sin(ang)
    fwd = np.concatenate([np.concatenate([gr, -gi], axis=2), np.concatenate([gi, gr], axis=2)], axis=1)
    hr, hi = np.transpose(gr, (0, 2, 1)), -np.transpose(gi, (0, 2, 1))
    inv = np.concatenate([np.concatenate([hr, -hi], axis=2), np.concatenate([hi, hr], axis=2)], axis=1)
    return jnp.asarray(fwd, BF16), jnp.asarray(inv, BF16)


def _stage1_kernel(z_ref, f_ref, a_ref):
    a_ref[0] = jnp.dot(f_ref[...], z_ref[0], preferred_element_type=F32).astype(BF16)


def fft_stage1(z2, fwd, tc=2048):
    bsz, r, cols = z2.shape
    return pl.pallas_call(
        _stage1_kernel,
        grid=(bsz, cols // tc),
        in_specs=[pl.BlockSpec((1, r, tc), lambda b, j: (b, 0, j)),
                  pl.BlockSpec(fwd.shape, lambda b, j: (0, 0))],
        out_specs=pl.BlockSpec((1, fwd.shape[0], tc), lambda b, j: (b, 0, j)),
        out_shape=jax.ShapeDtypeStruct((bsz, fwd.shape[0], cols), BF16),
        compiler_params=_cparams(("parallel", "parallel")),
        name="fft_stage1",
    )(z2, fwd)


def _mid_kernel(a_ref, g_ref, h_ref, kr_ref, ki_ref, d_ref):
    n2 = FFT_N2
    for j in range(a_ref.shape[2]):
        ab = jnp.concatenate([a_ref[0, 0, j], a_ref[0, 1, j]], axis=0)
        z = jnp.dot(g_ref[j], ab, preferred_element_type=F32)
        zr, zi = z[:n2], z[n2:]
        kr, ki = kr_ref[j], ki_ref[j]
        yb = jnp.concatenate([zr * kr - zi * ki, zr * ki + zi * kr], axis=0).astype(BF16)
        d = jnp.dot(h_ref[j], yb, preferred_element_type=F32)
        d_ref[0, 0, j] = d[:n2].astype(BF16)
        d_ref[0, 1, j] = d[n2:].astype(BF16)


def _mid_fwd_kernel(a_ref, g_ref, s_ref, zr_ref, zi_ref):
    n2 = FFT_N2
    for j in range(a_ref.shape[2]):
        ab = jnp.concatenate([a_ref[0, 0, j], a_ref[0, 1, j]], axis=0)
        z = jnp.dot(g_ref[j], ab, preferred_element_type=F32) * s_ref[...]
        zr_ref[j] = z[:n2]
        zi_ref[j] = z[n2:]


def fft_mid(a5, g, h, kf_re, kf_im, order, kb=4):
    bsz, _, n1, n2, c = a5.shape
    kb = min(kb, n1)
    blk = pl.BlockSpec((1, 2, kb, n2, c), lambda k, b: (b, 0, k, 0, 0))
    mat = pl.BlockSpec((kb, 2 * n2, 2 * n2), lambda k, b: (k, 0, 0))
    spec = pl.BlockSpec((kb, n2, c), lambda k, b: (k, 0, order))
    return pl.pallas_call(
        _mid_kernel,
        grid=(n1 // kb, bsz),
        in_specs=[blk, mat, mat, spec, spec],
        out_specs=blk,
        out_shape=jax.ShapeDtypeStruct(a5.shape, BF16),
        compiler_params=_cparams(("parallel", "arbitrary")),
        name="fft_mid",
    )(a5, g, h, kf_re, kf_im)


def fft_mid_fwd(a5, g, col_scale, kb=4):
    _, _, n1, n2, c = a5.shape
    kb = min(kb, n1)
    spec = pl.BlockSpec((kb, n2, c), lambda k: (k, 0, 0))
    osd = jax.ShapeDtypeStruct((n1, n2, c), F32)
    return pl.pallas_call(
        _mid_fwd_kernel,
        grid=(n1 // kb,),
        in_specs=[pl.BlockSpec((1, 2, kb, n2, c), lambda k: (0, 0, k, 0, 0)),
                  pl.BlockSpec((kb, 2 * n2, 2 * n2), lambda k: (k, 0, 0)),
                  pl.BlockSpec((1, c), lambda k: (0, 0))],
        out_specs=[spec, spec],
        out_shape=[osd, osd],
        compiler_params=_cparams(("parallel",)),
        name="fft_mid_fwd",
    )(a5, g, col_scale)


def _fin_kernel(d_ref, inv_ref, z_ref, gate_ref, bias_ref, *rest, scale, chain):
    conv = jnp.dot(inv_ref[...], d_ref[0], preferred_element_type=F32) * scale
    z = gate_ref[0].astype(F32) * (conv + z_ref[0].astype(F32) * bias_ref[...])
    zb = z.astype(BF16)
    if chain:
        f_ref, z_o, a_o = rest
        z_o[0] = zb
        a_o[0] = jnp.dot(f_ref[...], zb, preferred_element_type=F32).astype(BF16)
    else:
        (z_o,) = rest
        z_o[0] = zb


def fft_final(d3, inv, z2, gate2, bias_cols, scale, fwd=None, tc=2048):
    bsz, r, cols = z2.shape
    chain = fwd is not None
    col = lambda b, j: (b, 0, j)
    in_specs = [pl.BlockSpec((1, d3.shape[1], tc), col),
                pl.BlockSpec(inv.shape, lambda b, j: (0, 0)),
                pl.BlockSpec((1, r, tc), col),
                pl.BlockSpec((1, r, tc), col),
                pl.BlockSpec((1, tc), lambda b, j: (0, j))]
    out_specs = [pl.BlockSpec((1, r, tc), col)]
    out_shape = [jax.ShapeDtypeStruct(z2.shape, BF16)]
    args = [d3, inv, z2, gate2, bias_cols]
    if chain:
        in_specs.append(pl.BlockSpec(fwd.shape, lambda b, j: (0, 0)))
        out_specs.append(pl.BlockSpec((1, fwd.shape[0], tc), col))
        out_shape.append(jax.ShapeDtypeStruct((bsz, fwd.shape[0], cols), BF16))
        args.append(fwd)
    return pl.pallas_call(
        functools.partial(_fin_kernel, scale=scale, chain=chain),
        grid=(bsz, cols // tc),
        in_specs=in_specs,
        out_specs=out_specs,
        out_shape=out_shape,
        compiler_params=_cparams(("parallel", "parallel")),
        name="fft_final",
    )(*args)


def hyena_mixer(hz, bsz, seq, short_w, short_b, filt, filt_scale, bias):
    c = W_HYENA
    n2 = FFT_N2
    n = 2 * seq
    n1 = n // n2
    r = n1 // 2
    fwd_half, inv_half = _dft_tables(n1, r)
    fwd_full, _ = _dft_tables(n1, n1)
    g, h = _mid_tables(n1)
    ka = fft_stage1(filt.reshape(1, n1, n2 * 2 * c), fwd_full)
    kf_re, kf_im = fft_mid_fwd(ka.reshape(1, 2, n1, n2, 2 * c), g, filt_scale)
    v, x1, x2 = hyena_short_conv(hz, short_w, short_b, seq)
    as2 = lambda t: t.reshape(bsz, r, n2 * c)
    z2 = as2(v)
    a = fft_stage1(z2, fwd_half)
    for order, gate in enumerate((x1, x2)):
        d = fft_mid(a.reshape(bsz, 2, n1, n2, c), g, h, kf_re, kf_im, order)
        bias_cols = jnp.tile(bias[order].astype(F32), n2).reshape(1, n2 * c)
        res = fft_final(d.reshape(bsz, 2 * n1, n2 * c), inv_half, z2, as2(gate), bias_cols, 1.0 / n,
                        fwd=fwd_half if order == 0 else None)
        if order == 0:
            z2, a = res
        else:
            (z2,) = res
    return z2.reshape(bsz * seq, c)


def _trunk_layer0(x, c, wts):
    bsz, seq, _ = x.shape
    n = bsz * seq
    mod = adaln_mod(c, wts['ada_w'], wts['ada_b'])
    xf = x.reshape(n, D_MODEL)
    q, k, v, u = in_proj0(xf, mod[0], wts['norm_mix_g'][0], wts['ab_w_in'], wts['na_q_g'], wts['na_k_g'], seq)
    att = neighbourhood_attention(q, k, v, wts['att_bias'], bsz, seq)
    cnv = conv_module(u, wts['cv_dw_w'], wts['cv_dw_b'], wts['cv_ln_g'], wts['cv_ln_b'], seq)
    xf, h, gate, gate_t = out_proj_router(att, cnv, xf, mod[0], wts['ab_w_out'], wts['norm_ffn_g'][0],
                                          wts['router_w'], wts['router_b'], seq)
    xf = moe_ffn(h, gate, gate_t, xf, mod[0], wts['moe_wg'][0], wts['moe_wu'][0], wts['moe_wd'][0], seq)
    ut, hz = in_proj1(xf, mod[1], wts['norm_mix_g'][1], wts['cd_w_u_t'], wts['cd_w_z'], seq)
    return xf, mod, ut, hz


def _trunk_layer1(xf, mod, yt, hz, bsz, seq, wts):
    ssm = s5_post(yt, wts['s5_glu_w'])
    filt, filt_scale = hyena_filters(seq, *wts['hy_mlp'])
    hy = hyena_mixer(hz, bsz, seq, wts['hy_short_w'], wts['hy_short_b'], filt, filt_scale, wts['hy_bias'])
    xf, h, gate, gate_t = out_proj_router(ssm, hy, xf, mod[1], wts['cd_w_out'], wts['norm_ffn_g'][1],
                                          wts['router_w'], wts['router_b'], seq)
    xf = moe_ffn(h, gate, gate_t, xf, mod[1], wts['moe_wg'][1], wts['moe_wu'][1], wts['moe_wd'][1], seq)
    return xf.reshape(bsz, seq, D_MODEL)


def kernel(x_prompt, x_sample, c_prompt, c_sample, ada_w, ada_b, norm_mix_g, norm_ffn_g, router_w, router_b, moe_w_gate, moe_w_up, moe_w_down, ab_w_in, ab_w_out, na_q_g, na_k_g, na_rpb, cv_dw_w, cv_dw_b, cv_ln_g, cv_ln_b, cd_w_in, cd_w_out, s5_lam_re, s5_lam_im, s5_log_dt, s5_b_re, s5_b_im, s5_c_re, s5_c_im, s5_d, s5_glu_w, hy_short_w, hy_short_b, hy_w1, hy_b1, hy_w2, hy_b2, hy_w3, hy_freq, hy_decay, hy_bias):
    s5_tabs = s5_tables(s5_lam_re[0], s5_lam_im[0], s5_log_dt[0], s5_b_re[0], s5_b_im[0],
                        s5_c_re[0], s5_c_im[0], s5_d[0])
    wts = dict(
        ada_w=ada_w, ada_b=ada_b, norm_mix_g=norm_mix_g.astype(F32), norm_ffn_g=norm_ffn_g.astype(F32),
        router_w=router_w, router_b=router_b,
        moe_wg=jnp.swapaxes(moe_w_gate, -1, -2).astype(BF16), moe_wu=jnp.swapaxes(moe_w_up, -1, -2).astype(BF16),
        moe_wd=jnp.swapaxes(moe_w_down, -1, -2).astype(BF16),
        ab_w_in=ab_w_in[0].astype(BF16), ab_w_out=ab_w_out[0].astype(BF16),
        na_q_g=na_q_g[0], na_k_g=na_k_g[0], att_bias=_att_bias_table(na_rpb[0]),
        cv_dw_w=cv_dw_w[0], cv_dw_b=cv_dw_b[0], cv_ln_g=cv_ln_g[0], cv_ln_b=cv_ln_b[0],
        cd_w_u_t=cd_w_in[0][:, :W_SSM].T.astype(BF16), cd_w_z=cd_w_in[0][:, W_SSM:].astype(BF16),
        cd_w_out=cd_w_out[0].astype(BF16),
        s5_glu_w=s5_glu_w[0].astype(BF16),
        hy_short_w=hy_short_w[0], hy_short_b=hy_short_b[0],
        hy_mlp=(hy_w1[0], hy_b1[0], hy_w2[0], hy_b2[0], hy_w3[0], hy_freq[0], hy_decay[0]),
        hy_bias=hy_bias[0],
    )
    xs = (x_prompt, x_sample)
    mids = [_trunk_layer0(x, c, wts) for x, c in zip(xs, (c_prompt, c_sample))]
    yts = s5_scan([m[2] for m in mids], [x.shape[1] for x in xs], *s5_tabs)
    return tuple(_trunk_layer1(m[0], m[1], yt, m[3], x.shape[0], x.shape[1], wts)
                 for m, yt, x in zip(mids, yts, xs))
```

```python
import functools
import math

import numpy as np
import jax
import jax.numpy as jnp
from jax import lax
from jax.experimental import pallas as pl
from jax.experimental.pallas import tpu as pltpu

F32 = jnp.float32
BF16 = jnp.bfloat16
HIGHEST = lax.Precision.HIGHEST

D_MODEL = 1024
DEPTH = 2
GRID_W = 64
W_ATT = 512
W_CONV = 512
W_SSM = 512
W_HYENA = 512
HEAD_DIM = 64
N_HEADS = 8
WIN_R = 8
WIN_C = 16
CONV_W = 31
S5_GROUP = 16
S5_GROUPS = 32
S5_STATE = 64
HY_BANDS = 8
N_EXPERTS = 16
N_GROUPS = 4
D_FF = 512
EPS = 1e-6
NEG_INF = -1e30

VMEM_LIMIT_BYTES = 56 * 1024 * 1024
LANES = 128
SUBLANES = 8

ATT_QROWS = 4
ATT_KROWS = 12
S5_CHUNK = LANES
FFT_N2 = 128


def _cparams(sem):
    return pltpu.CompilerParams(dimension_semantics=sem, vmem_limit_bytes=VMEM_LIMIT_BYTES)


def _mod_kernel(c_ref, w_ref, b_ref, o_ref):
    c = c_ref[...]
    s = c * jax.nn.sigmoid(c)
    o_ref[0] = jnp.dot(s, w_ref[0], precision=HIGHEST, preferred_element_type=F32) + b_ref[0]


def adaln_mod(c, ada_w, ada_b):
    bsz = c.shape[0]
    tn = D_MODEL
    out = pl.pallas_call(
        _mod_kernel,
        grid=(DEPTH, 6 * D_MODEL // tn),
        in_specs=[
            pl.BlockSpec((bsz, D_MODEL), lambda i, j: (0, 0)),
            pl.BlockSpec((1, D_MODEL, tn), lambda i, j: (i, 0, j)),
            pl.BlockSpec((1, 1, tn), lambda i, j: (i, 0, j)),
        ],
        out_specs=pl.BlockSpec((1, bsz, tn), lambda i, j: (i, 0, j)),
        out_shape=jax.ShapeDtypeStruct((DEPTH, bsz, 6 * D_MODEL), F32),
        compiler_params=_cparams(("arbitrary", "arbitrary")),
        name="adaln_mod",
    )(c, ada_w, ada_b.reshape(DEPTH, 1, 6 * D_MODEL))
    return out.reshape(DEPTH, bsz, 6, D_MODEL)


def _dot3(a, b):
    a_hi, b_hi = a.astype(BF16), b.astype(BF16)
    a_lo = (a - a_hi.astype(F32)).astype(BF16)
    b_lo = (b - b_hi.astype(F32)).astype(BF16)
    return (jnp.dot(a_hi, b_hi, preferred_element_type=F32) + jnp.dot(a_lo, b_hi, preferred_element_type=F32)
            + jnp.dot(a_hi, b_lo, preferred_element_type=F32))


def _norm_mod(x, g, shift, scale):
    ms = jnp.mean(x * x, axis=-1, keepdims=True)
    return x * lax.rsqrt(ms + EPS) * g * (1.0 + scale) + shift


def _in0_kernel(x_ref, mod_ref, g_ref, w_ref, hm_ref, qg_ref, kg_ref, q_o, k_o, v_o, u_o):
    m = mod_ref[0]
    h = _norm_mod(x_ref[...], g_ref[...], m[0:1], m[1:2])
    p = jnp.dot(h.astype(BF16), w_ref[...], preferred_element_type=F32)
    q = p[:, 0:W_ATT]
    k = p[:, W_ATT:2 * W_ATT]
    qms = jnp.dot((q * q).astype(BF16), hm_ref[...], preferred_element_type=F32)
    kms = jnp.dot((k * k).astype(BF16), hm_ref[...], preferred_element_type=F32)
    q_o[...] = (q * lax.rsqrt(qms + EPS) * qg_ref[...]).astype(BF16)
    k_o[...] = (k * lax.rsqrt(kms + EPS) * kg_ref[...]).astype(BF16)
    v_o[...] = p[:, 2 * W_ATT:3 * W_ATT].astype(BF16)
    val = p[:, 3 * W_ATT:3 * W_ATT + W_CONV]
    gate = p[:, 3 * W_ATT + W_CONV:]
    u_o[...] = (val * jax.nn.sigmoid(gate)).astype(BF16)


def in_proj0(x, mod, g, w_in_bf, q_gain, k_gain, seq, tm=512):
    n = x.shape[0]
    head_mean = jnp.asarray(np.kron(np.eye(N_HEADS), np.full((HEAD_DIM, HEAD_DIM), 1.0 / HEAD_DIM)), BF16)
    qg = (jnp.tile(q_gain.astype(F32), N_HEADS) * (HEAD_DIM ** -0.5)).reshape(1, W_ATT)
    kg = jnp.tile(k_gain.astype(F32), N_HEADS).reshape(1, W_ATT)
    tok = lambda i: (i, 0)
    fixed = lambda i: (0, 0)
    osd = jax.ShapeDtypeStruct((n, W_ATT), BF16)
    return pl.pallas_call(
        _in0_kernel,
        grid=(n // tm,),
        in_specs=[
            pl.BlockSpec((tm, D_MODEL), tok),
            pl.BlockSpec((1, 6, D_MODEL), lambda i: ((i * tm) // seq, 0, 0)),
            pl.BlockSpec((1, D_MODEL), fixed),
            pl.BlockSpec(w_in_bf.shape, fixed),
            pl.BlockSpec((W_ATT, W_ATT), fixed),
            pl.BlockSpec((1, W_ATT), fixed),
            pl.BlockSpec((1, W_ATT), fixed),
        ],
        out_specs=[pl.BlockSpec((tm, W_ATT), tok)] * 4,
        out_shape=[osd] * 4,
        compiler_params=_cparams(("parallel",)),
        name="in_proj0",
    )(x, mod, g.reshape(1, D_MODEL), w_in_bf, head_mean, qg, kg)


def _att_bias_table(rpb):
    a = np.arange(ATT_QROWS)[:, None, None, None]
    c = np.arange(GRID_W)[None, :, None, None]
    e = np.arange(ATT_KROWS)[None, None, :, None]
    kc = np.arange(GRID_W)[None, None, None, :]
    c0 = np.clip(c - WIN_C // 2, 0, GRID_W - WIN_C)
    col_ok = (kc >= c0) & (kc < c0 + WIN_C)
    rpb = rpb.astype(F32)
    per = 2 * GRID_W
    vrow = jnp.concatenate([rpb[..., WIN_C - 1:],
                            jnp.zeros(rpb.shape[:2] + (per - (2 * WIN_C - 1),), F32),
                            rpb[..., :WIN_C - 1]], axis=-1)
    tcol = jnp.tile(vrow, (1, 1, GRID_W))[..., :GRID_W * (per - 1)]
    tcol = tcol.reshape(rpb.shape[:2] + (GRID_W, per - 1))[..., :GRID_W]
    tables = []
    for case in range(3):
        if case == 0:
            dr = e - a
            row_ok = (e >= 0) & (e < WIN_R)
        elif case == 1:
            dr = e - a - WIN_R // 2
            row_ok = (dr >= -(WIN_R // 2)) & (dr < WIN_R // 2)
        else:
            dr = e - a - (ATT_KROWS - ATT_QROWS)
            row_ok = (e >= ATT_KROWS - WIN_R) & (e < ATT_KROWS)
        ok = np.broadcast_to(row_ok & col_ok, (ATT_QROWS, GRID_W, ATT_KROWS, GRID_W))
        dri = np.clip(dr + WIN_R - 1, 0, 2 * WIN_R - 2)[:, 0, :, 0]
        nq, nk = ATT_QROWS * GRID_W, ATT_KROWS * GRID_W
        t = jnp.concatenate([jnp.concatenate([tcol[:, int(dri[qa, ke])] for ke in range(ATT_KROWS)], axis=-1)
                             for qa in range(ATT_QROWS)], axis=-2)
        tables.append(jnp.where(jnp.asarray(ok.reshape(nq, nk)), t, NEG_INF))
    return jnp.stack(tables).astype(BF16)


def _att_kernel(q_ref, k0, k1, k2, v0, v1, v2, bias_ref, o_ref):
    kt = [k0, k1, k2]
    vt = [v0, v1, v2]
    nkb = len(kt)
    kw = k0.shape[0]
    tq = q_ref.shape[0]
    first = lax.broadcasted_iota(jnp.int32, (tq, LANES), 1) < HEAD_DIM
    for hp in range(N_HEADS // 2):
        ps = slice(hp * LANES, (hp + 1) * LANES)
        qp = q_ref[:, ps]
        res = []
        for sub in range(2):
            h = 2 * hp + sub
            qm = jnp.where(first if sub == 0 else jnp.logical_not(first), qp, jnp.zeros_like(qp))
            s = [lax.dot_general(qm, kt[j][:, ps], (((1,), (1,)), ((), ())), preferred_element_type=F32)
                 + bias_ref[0, h, :, j * kw:(j + 1) * kw].astype(F32) for j in range(nkb)]
            m = s[0].max(axis=-1, keepdims=True)
            for j in range(1, nkb):
                m = jnp.maximum(m, s[j].max(axis=-1, keepdims=True))
            p = [jnp.exp(sj - m) for sj in s]
            l = p[0].sum(axis=-1, keepdims=True)
            for j in range(1, nkb):
                l = l + p[j].sum(axis=-1, keepdims=True)
            o = jnp.dot(p[0].astype(BF16), vt[0][:, ps], preferred_element_type=F32)
            for j in range(1, nkb):
                o = o + jnp.dot(p[j].astype(BF16), vt[j][:, ps], preferred_element_type=F32)
            res.append(o / l)
        o_ref[:, ps] = jnp.where(first, res[0], res[1]).astype(BF16)


def neighbourhood_attention(q, k, v, bias, bsz, seq):
    rows = seq // GRID_W
    nqb = rows // ATT_QROWS
    tq = ATT_QROWS * GRID_W
    nkb = ATT_KROWS // ATT_QROWS
    assert rows % ATT_QROWS == 0 and nqb >= nkb

    def kmap(j):
        def f(i):
            b, r = i // nqb, i % nqb
            return (b * nqb + jnp.clip(r - 1, 0, nqb - nkb) + j, 0)
        return f

    def bias_map(i):
        r = i % nqb
        return (jnp.where(r == 0, 0, jnp.where(r == nqb - 1, 2, 1)), 0, 0, 0)

    kv_specs = [pl.BlockSpec((tq, W_ATT), kmap(j)) for j in range(nkb)]
    return pl.pallas_call(
        _att_kernel,
        grid=(bsz * nqb,),
        in_specs=[pl.BlockSpec((tq, W_ATT), lambda i: (i, 0))] + kv_specs + kv_specs
        + [pl.BlockSpec((1,) + bias.shape[1:], bias_map)],
        out_specs=pl.BlockSpec((tq, W_ATT), lambda i: (i, 0)),
        out_shape=jax.ShapeDtypeStruct(q.shape, BF16),
        compiler_params=_cparams(("parallel",)),
        name="neighbourhood_attention",
    )(q, k, k, k, v, v, v, bias)


CONV_HALO = 16


def _conv_kernel(prev_ref, cur_ref, next_ref, w_ref, b_ref, g_ref, be_ref, o_ref, win_ref, sh_ref, *, tiles_per_seq):
    i = pl.program_id(0)
    t = i % tiles_per_seq
    tt = cur_ref.shape[0]
    half = CONV_W // 2
    prev = prev_ref[...].astype(F32)
    nxt = next_ref[...].astype(F32)
    win_ref[0:CONV_HALO, :] = jnp.where(t == 0, 0.0, prev)
    win_ref[CONV_HALO:CONV_HALO + tt, :] = cur_ref[...].astype(F32)
    win_ref[CONV_HALO + tt:, :] = jnp.where(t == tiles_per_seq - 1, 0.0, nxt)
    span = tt + 2 * CONV_HALO - SUBLANES
    for ph in range(1, SUBLANES):
        sh_ref[ph, 0:span, :] = win_ref[ph:ph + span, :]
    w = w_ref[...]
    acc = jnp.zeros((tt, W_CONV), F32) + b_ref[...]
    for kk in range(CONV_W):
        off = CONV_HALO - half + kk
        ph, base = off % SUBLANES, (off // SUBLANES) * SUBLANES
        tap = win_ref[base:base + tt, :] if ph == 0 else sh_ref[ph, base:base + tt, :]
        acc = acc + tap * w[kk:kk + 1, :]
    mu = jnp.mean(acc, axis=-1, keepdims=True)
    d = acc - mu
    var = jnp.mean(d * d, axis=-1, keepdims=True)
    y = d * lax.rsqrt(var + EPS) * g_ref[...] + be_ref[...]
    o_ref[...] = (y * jax.nn.sigmoid(y)).astype(BF16)


def conv_module(u, dw_w, dw_b, ln_g, ln_b, seq, tt=512):
    n = u.shape[0]
    tps = seq // tt
    hb = tt // CONV_HALO
    nhb = n // CONV_HALO
    row = lambda a: a.astype(F32).reshape(1, W_CONV)
    fixed = lambda i: (0, 0)
    return pl.pallas_call(
        functools.partial(_conv_kernel, tiles_per_seq=tps),
        grid=(n // tt,),
        in_specs=[
            pl.BlockSpec((CONV_HALO, W_CONV), lambda i: (jnp.maximum(i * hb - 1, 0), 0)),
            pl.BlockSpec((tt, W_CONV), lambda i: (i, 0)),
            pl.BlockSpec((CONV_HALO, W_CONV), lambda i: (jnp.minimum((i + 1) * hb, nhb - 1), 0)),
            pl.BlockSpec((CONV_W, W_CONV), fixed),
            pl.BlockSpec((1, W_CONV), fixed),
            pl.BlockSpec((1, W_CONV), fixed),
            pl.BlockSpec((1, W_CONV), fixed),
        ],
        out_specs=pl.BlockSpec((tt, W_CONV), lambda i: (i, 0)),
        out_shape=jax.ShapeDtypeStruct((n, W_CONV), BF16),
        scratch_shapes=[pltpu.VMEM((tt + 2 * CONV_HALO, W_CONV), F32),
                        pltpu.VMEM((SUBLANES, tt + 2 * CONV_HALO, W_CONV), F32)],
        compiler_params=_cparams(("parallel",)),
        name="conv_module",
    )(u, u, u, dw_w.astype(F32), row(dw_b), row(ln_g), row(ln_b))


def _top2_sum(a, b, c, d):
    hi1, lo1 = jnp.maximum(a, b), jnp.minimum(a, b)
    hi2, lo2 = jnp.maximum(c, d), jnp.minimum(c, d)
    return jnp.maximum(hi1, hi2) + jnp.maximum(jnp.minimum(hi1, hi2), jnp.maximum(lo1, lo2))


def _router_gates(scores_t, bias_t):
    per = N_EXPERTS // N_GROUPS
    rows = [scores_t[e:e + 1, :] + bias_t[e:e + 1, :] for e in range(N_EXPERTS)]
    gscore = [_top2_sum(*rows[g * per:(g + 1) * per]) for g in range(N_GROUPS)]
    best = gscore[0]
    best_idx = jnp.zeros_like(best, dtype=jnp.int32)
    for g in range(1, N_GROUPS):
        better = gscore[g] > best
        best = jnp.where(better, gscore[g], best)
        best_idx = jnp.where(better, g, best_idx)
    out_row = lax.broadcasted_iota(jnp.int32, (LANES, scores_t.shape[1]), 0)
    gates = jnp.zeros((LANES, scores_t.shape[1]), F32)
    total = jnp.zeros_like(best)
    for e in range(N_EXPERTS):
        g = e // per
        rank = jnp.zeros_like(best_idx)
        for e2 in range(g * per, (g + 1) * per):
            if e2 == e:
                continue
            if e2 < e:
                ahead = rows[e2] >= rows[e]
            else:
                ahead = rows[e2] > rows[e]
            rank = rank + jnp.where(ahead, 1, 0)
        chosen = jnp.where(best_idx == g, rank, 2) < 2
        gated = jnp.where(chosen, scores_t[e:e + 1, :], 0.0)
        total = total + gated
        gates = jnp.where(out_row == e, gated, gates)
    return gates / total


def _out_kernel(a_ref, b_ref, x_ref, mod_ref, wa_ref, wb_ref, g_ref, rwh_ref, rwl_ref, rb_ref, x_o, h_o, gate_o,
                gate_t_o):
    m = mod_ref[0]
    mix = (jnp.dot(a_ref[...], wa_ref[...], preferred_element_type=F32)
           + jnp.dot(b_ref[...], wb_ref[...], preferred_element_type=F32))
    x = x_ref[...] + m[2:3] * mix
    x_o[...] = x
    h = _norm_mod(x, g_ref[...], m[3:4], m[4:5])
    h_hi = h.astype(BF16)
    h_o[...] = h_hi
    h_lo = (h - h_hi.astype(F32)).astype(BF16)
    logits = (jnp.dot(h_hi, rwh_ref[...], preferred_element_type=F32)
              + jnp.dot(h_lo, rwh_ref[...], preferred_element_type=F32)
              + jnp.dot(h_hi, rwl_ref[...], preferred_element_type=F32))
    scores_t = jax.nn.sigmoid(logits).T
    gates_t = _router_gates(scores_t, rb_ref[...])
    gate_t_o[...] = gates_t
    gate_o[...] = gates_t.T


def out_proj_router(a, b, x, mod, w_out_bf, g_ffn, router_w, router_b, seq, tm=512):
    n = x.shape[0]
    half = a.shape[1]
    tok = lambda i: (i, 0)
    fixed = lambda i: (0, 0)
    rw = jnp.pad(router_w.astype(F32), ((0, 0), (0, LANES - N_EXPERTS)))
    rw_hi = rw.astype(BF16)
    rw_lo = (rw - rw_hi.astype(F32)).astype(BF16)
    return pl.pallas_call(
        _out_kernel,
        grid=(n // tm,),
        in_specs=[
            pl.BlockSpec((tm, half), tok),
            pl.BlockSpec((tm, half), tok),
            pl.BlockSpec((tm, D_MODEL), tok),
            pl.BlockSpec((1, 6, D_MODEL), lambda i: ((i * tm) // seq, 0, 0)),
            pl.BlockSpec((half, D_MODEL), lambda i: (0, 0)),
            pl.BlockSpec((half, D_MODEL), lambda i: (1, 0)),
            pl.BlockSpec((1, D_MODEL), fixed),
            pl.BlockSpec((D_MODEL, LANES), fixed),
            pl.BlockSpec((D_MODEL, LANES), fixed),
            pl.BlockSpec((N_EXPERTS, 1), fixed),
        ],
        out_specs=[pl.BlockSpec((tm, D_MODEL), tok), pl.BlockSpec((tm, D_MODEL), tok),
                   pl.BlockSpec((tm, LANES), tok), pl.BlockSpec((LANES, tm), lambda i: (0, i))],
        out_shape=[jax.ShapeDtypeStruct((n, D_MODEL), F32), jax.ShapeDtypeStruct((n, D_MODEL), BF16),
                   jax.ShapeDtypeStruct((n, LANES), F32), jax.ShapeDtypeStruct((LANES, n), F32)],
        compiler_params=_cparams(("parallel",)),
        name="out_proj_router",
    )(a, b, x, mod, w_out_bf, w_out_bf, g_ffn.reshape(1, D_MODEL), rw_hi, rw_lo,
      router_b.astype(F32).reshape(N_EXPERTS, 1))


MOE_COLS = 256
EXPERTS_PER_GROUP = N_EXPERTS // N_GROUPS


def _moe_kernel(h_ref, gate_ref, gate_t_ref, x_ref, mod_ref, tri_ref, wg_ref, wu_ref, wd_ref, o_ref,
                rankc_ref, rankr_ref, ht_ref, xg_ref, gg_ref, yg_ref, acc_ref, cnt_ref):
    e = pl.program_id(1)
    g = e // EXPERTS_PER_GROUP
    j = e % EXPERTS_PER_GROUP
    T = h_ref.shape[0]
    R = MOE_COLS

    @pl.when(e == 0)
    def _():
        er = lax.broadcasted_iota(jnp.int32, (LANES, LANES), 0)
        ec = lax.broadcasted_iota(jnp.int32, (LANES, LANES), 1)
        sel_c = jnp.where((er < N_EXPERTS) & (er // EXPERTS_PER_GROUP == ec), 1.0, 0.0).astype(BF16)
        sel_r = jnp.where((ec < N_EXPERTS) & (ec // EXPERTS_PER_GROUP == er), 1.0, 0.0).astype(BF16)
        chosen_c = jnp.where(gate_ref[...] > 0.0, 1.0, 0.0).astype(BF16)
        memb_c = jnp.dot(chosen_c, sel_c, preferred_element_type=F32) > 0.0
        rank_c = jnp.dot(tri_ref[...], jnp.where(memb_c, 1.0, 0.0).astype(BF16), preferred_element_type=F32)
        rankc_ref[...] = jnp.where(memb_c, rank_c, -1.0)
        chosen_r = jnp.where(gate_t_ref[...] > 0.0, 1.0, 0.0).astype(BF16)
        memb_r = jnp.dot(sel_r, chosen_r, preferred_element_type=F32) > 0.0
        ones_r = jnp.where(memb_r, 1.0, 0.0)
        rank_r = lax.dot_general(ones_r.astype(BF16), tri_ref[...], (((1,), (1,)), ((), ())),
                                 preferred_element_type=F32)
        rankr_ref[...] = jnp.where(memb_r, rank_r, -1.0)
        for gi in range(N_GROUPS):
            cnt_ref[gi] = jnp.sum(ones_r[gi:gi + 1, :]).astype(jnp.int32)
        ht_ref[...] = h_ref[...].astype(F32).T.astype(BF16)
        acc_ref[...] = jnp.zeros_like(acc_ref)

    nch = (cnt_ref[g] + (R - 1)) // R

    @pl.when(j == 0)
    def _():
        lane_t = lax.broadcasted_iota(jnp.int32, (T, LANES), 1)
        rc = jnp.sum(jnp.where(lane_t == g, rankc_ref[...], 0.0), axis=1, keepdims=True)
        col = lax.broadcasted_iota(jnp.int32, (T, R), 1).astype(F32)
        gt = gate_t_ref[...]
        g_hi = gt.astype(BF16)
        g_lo = (gt - g_hi.astype(F32)).astype(BF16)

        def gather(c, carry):
            pt = jnp.where(rc - (c * R).astype(F32) == col, 1.0, 0.0).astype(BF16)
            xg_ref[c] = jnp.dot(ht_ref[...], pt, preferred_element_type=F32).astype(BF16)
            gg_ref[c] = (jnp.dot(g_hi, pt, preferred_element_type=F32)
                         + jnp.dot(g_lo, pt, preferred_element_type=F32))
            yg_ref[c] = jnp.zeros((D_MODEL, R), F32)
            return carry

        lax.fori_loop(0, nch, gather, 0)

    def ffn(c, carry):
        xc = xg_ref[c]
        a = jnp.dot(wg_ref[0], xc, preferred_element_type=F32)
        u = jnp.dot(wu_ref[0], xc, preferred_element_type=F32)
        hid = (a * jax.nn.sigmoid(a)) * u * gg_ref[c, pl.ds(e, 1), :]
        yg_ref[c] += jnp.dot(wd_ref[0], hid.astype(BF16), preferred_element_type=F32)
        return carry

    lax.fori_loop(0, nch, ffn, 0)

    @pl.when(j == EXPERTS_PER_GROUP - 1)
    def _():
        rr = rankr_ref[pl.ds(g, 1), :]
        row = lax.broadcasted_iota(jnp.int32, (R, T), 0).astype(F32)

        def scatter(c, carry):
            p = jnp.where(rr - (c * R).astype(F32) == row, 1.0, 0.0).astype(BF16)
            acc_ref[...] += jnp.dot(yg_ref[c].astype(BF16), p, preferred_element_type=F32)
            return carry

        lax.fori_loop(0, nch, scatter, 0)

    @pl.when(e == N_EXPERTS - 1)
    def _():
        o_ref[...] = x_ref[...] + mod_ref[0][5:6] * acc_ref[...].T


def moe_ffn(h, gate, gate_t, x, mod, wgt_bf, wut_bf, wdt_bf, seq, tm=1024):
    n = x.shape[0]
    tm = min(tm, seq)
    nch = tm // MOE_COLS
    tok = lambda i, e: (i, 0)
    tri = jnp.asarray(np.tril(np.ones((tm, tm), np.float32), -1), BF16)
    return pl.pallas_call(
        _moe_kernel,
        grid=(n // tm, N_EXPERTS),
        in_specs=[
            pl.BlockSpec((tm, D_MODEL), tok),
            pl.BlockSpec((tm, LANES), tok),
            pl.BlockSpec((LANES, tm), lambda i, e: (0, i)),
            pl.BlockSpec((tm, D_MODEL), tok),
            pl.BlockSpec((1, 6, D_MODEL), lambda i, e: ((i * tm) // seq, 0, 0)),
            pl.BlockSpec((tm, tm), lambda i, e: (0, 0)),
            pl.BlockSpec((1, D_FF, D_MODEL), lambda i, e: (e, 0, 0)),
            pl.BlockSpec((1, D_FF, D_MODEL), lambda i, e: (e, 0, 0)),
            pl.BlockSpec((1, D_MODEL, D_FF), lambda i, e: (e, 0, 0)),
        ],
        out_specs=pl.BlockSpec((tm, D_MODEL), tok),
        out_shape=jax.ShapeDtypeStruct((n, D_MODEL), F32),
        scratch_shapes=[pltpu.VMEM((tm, LANES), F32), pltpu.VMEM((LANES, tm), F32),
                        pltpu.VMEM((D_MODEL, tm), BF16), pltpu.VMEM((nch, D_MODEL, MOE_COLS), BF16),
                        pltpu.VMEM((nch, LANES, MOE_COLS), F32), pltpu.VMEM((nch, D_MODEL, MOE_COLS), F32),
                        pltpu.VMEM((D_MODEL, tm), F32), pltpu.SMEM((N_GROUPS,), jnp.int32)],
        compiler_params=_cparams(("parallel", "arbitrary")),
        name="moe_ffn",
    )(h, gate, gate_t, x, mod, tri, wgt_bf, wut_bf, wdt_bf)


def _in1_kernel(x_ref, mod_ref, g_ref, wut_ref, wz_ref, ut_o, hz_o):
    m = mod_ref[0]
    h = _norm_mod(x_ref[...], g_ref[...], m[0:1], m[1:2]).astype(BF16)
    ut_o[...] = lax.dot_general(wut_ref[...], h, (((1,), (1,)), ((), ())),
                                preferred_element_type=F32).astype(BF16)
    hz_o[...] = jnp.dot(h, wz_ref[...], preferred_element_type=F32).astype(BF16)


def in_proj1(x, mod, g, w_u_t_bf, w_z_bf, seq, tm=512):
    n = x.shape[0]
    fixed = lambda i: (0, 0)
    return pl.pallas_call(
        _in1_kernel,
        grid=(n // tm,),
        in_specs=[
            pl.BlockSpec((tm, D_MODEL), lambda i: (i, 0)),
            pl.BlockSpec((1, 6, D_MODEL), lambda i: ((i * tm) // seq, 0, 0)),
            pl.BlockSpec((1, D_MODEL), fixed),
            pl.BlockSpec(w_u_t_bf.shape, fixed),
            pl.BlockSpec(w_z_bf.shape, fixed),
        ],
        out_specs=[pl.BlockSpec((W_SSM, tm), lambda i: (0, i)),
                   pl.BlockSpec((tm, 3 * W_HYENA), lambda i: (i, 0))],
        out_shape=[jax.ShapeDtypeStruct((W_SSM, n), BF16), jax.ShapeDtypeStruct((n, 3 * W_HYENA), BF16)],
        compiler_params=_cparams(("parallel",)),
        name="in_proj1",
    )(x, mod, g.reshape(1, D_MODEL), w_u_t_bf, w_z_bf)


def s5_tables(lam_re, lam_im, log_dt, b_re, b_im, c_re, c_im, d_skip):
    T = S5_CHUNK
    f32 = F32
    lags = jnp.arange(T, dtype=f32)

    def disc(d):
        lr, li = lam_re[d].astype(f32), lam_im[d].astype(f32)
        dt = jnp.exp(log_dt[d].astype(f32))[:, None]
        mag = jnp.exp(lr * dt)
        ab_re, ab_im = mag * jnp.cos(li * dt), mag * jnp.sin(li * dt)
        den = lr * lr + li * li
        f_re = ((ab_re - 1.0) * lr + ab_im * li) / den
        f_im = (ab_im * lr - (ab_re - 1.0) * li) / den
        br, bi = b_re[d].astype(f32), b_im[d].astype(f32)
        bb_re = f_re[..., None] * br - f_im[..., None] * bi
        bb_im = f_re[..., None] * bi + f_im[..., None] * br

        def power(p):
            ang = li * dt
            mg = jnp.exp(p[:, None, None] * (lr * dt)[None])
            return mg * jnp.cos(p[:, None, None] * ang[None]), mg * jnp.sin(p[:, None, None] * ang[None])
        return bb_re, bb_im, c_re[d].astype(f32), c_im[d].astype(f32), power

    hp = dict(precision=HIGHEST)
    tabs = []
    for d in range(2):
        bb_re, bb_im, cr, ci, power = disc(d)
        pr, pi = power(lags)
        cb_rr = jnp.einsum('gon,lgn,gni->lgoi', cr, pr, bb_re, **hp)
        cb_ii = jnp.einsum('gon,lgn,gni->lgoi', cr, pi, bb_im, **hp)
        cb_ri = jnp.einsum('gon,lgn,gni->lgoi', ci, pr, bb_im, **hp)
        cb_ir = jnp.einsum('gon,lgn,gni->lgoi', ci, pi, bb_re, **hp)
        kern = cb_rr - cb_ii - cb_ri - cb_ir
        qp = (T - 1.0 - lags) if d == 0 else lags
        qr, qi = power(qp)
        wp_re = jnp.einsum('tgn,gni->gitn', qr, bb_re) - jnp.einsum('tgn,gni->gitn', qi, bb_im)
        wp_im = jnp.einsum('tgn,gni->gitn', qr, bb_im) + jnp.einsum('tgn,gni->gitn', qi, bb_re)
        rp = (lags + 1.0) if d == 0 else (T - lags)
        rr, ri = power(rp)
        m_re = jnp.einsum('gon,tgn->gnot', cr, rr) - jnp.einsum('gon,tgn->gnot', ci, ri)
        m_im = jnp.einsum('gon,tgn->gnot', cr, ri) + jnp.einsum('gon,tgn->gnot', ci, rr)
        levels = 2.0 ** jnp.arange(16, dtype=f32) * T
        ar, ai = power(levels)
        tabs.append((kern, wp_re, wp_im, m_re, -m_im, ar, ai))

    kf, kb = tabs[0][0], tabs[1][0]
    skip = jnp.eye(S5_GROUP, dtype=f32)[None] * d_skip.astype(f32).reshape(S5_GROUPS, S5_GROUP, 1)
    k0 = kf[0] + kb[0] + skip
    kk = jnp.concatenate([k0[None], kf[1:], jnp.zeros_like(k0)[None], kb[:0:-1]], axis=0)
    kk = jnp.transpose(kk, (1, 3, 2, 0))
    g_, i_, o_ = kk.shape[:3]
    kk = kk.reshape(g_, i_ * o_, 2 * T)
    w_state = jnp.concatenate([tabs[0][1], tabs[0][2], tabs[1][1], tabs[1][2]], axis=-1)
    wp = w_state.reshape(g_, i_ * T, 4 * S5_STATE).astype(BF16)
    wc = jnp.concatenate([tabs[0][3], tabs[0][4], tabs[1][3], tabs[1][4]], axis=1)
    wc = wc.reshape(g_, 4 * S5_STATE, o_ * T).astype(BF16)
    mult = []
    for d in range(2):
        ar, ai = tabs[d][5], tabs[d][6]
        mult += [jnp.concatenate([ar, ar], axis=-1), jnp.concatenate([-ai, ai], axis=-1)]
    scan_mult = jnp.transpose(jnp.stack(mult, axis=2), (1, 0, 2, 3))
    return kk, wp, wc, scan_mult


def _s5_kernel(*refs, chunks_per_seq):
    nt = len(chunks_per_seq)
    u_refs, (kk_ref, wp_ref, wc_ref, mult_ref) = refs[:nt], refs[nt:nt + 4]
    o_refs, w_ref = refs[nt + 4:2 * nt + 4], refs[2 * nt + 4]
    T = S5_CHUNK

    def build(i, carry):
        for o in range(S5_GROUP):
            row = kk_ref[0, pl.ds(i * S5_GROUP + o, 1), :]
            toe = pltpu.roll(jnp.broadcast_to(row, (T, 2 * T)), 0, axis=1, stride=1, stride_axis=0)
            w_ref[pl.ds(pl.multiple_of(i * T, T), T), o * T:(o + 1) * T] = toe[:, :T].astype(BF16)
        return carry

    lax.fori_loop(0, S5_GROUP, build, 0)
    for u_ref, o_ref, cps in zip(u_refs, o_refs, chunks_per_seq):
        _s5_apply(u_ref, o_ref, w_ref, wp_ref, wc_ref, mult_ref, cps)


def _s5_apply(u_ref, o_ref, w_ref, wp_ref, wc_ref, mult_ref, chunks_per_seq):
    T = S5_CHUNK
    ns = 2 * S5_STATE
    x = jnp.concatenate([u_ref[i] for i in range(S5_GROUP)], axis=1)
    y = jnp.dot(x, w_ref[...], preferred_element_type=F32)
    pst = jnp.dot(x, wp_ref[0], preferred_element_type=F32)
    nc = y.shape[0]
    cidx = lax.broadcasted_iota(jnp.int32, (nc, ns), 0) % chunks_per_seq

    def cmul(s, mre, mim):
        return s * mre + pltpu.roll(s, S5_STATE, axis=1) * mim

    def scan(p, d):
        s = p
        k, step = 0, 1
        while step < chunks_per_seq:
            mre = mult_ref[0, k, 2 * d:2 * d + 1, :]
            mim = mult_ref[0, k, 2 * d + 1:2 * d + 2, :]
            if d == 0:
                sh = jnp.where(cidx >= step, pltpu.roll(s, step, axis=0), 0.0)
            else:
                sh = jnp.where(cidx < chunks_per_seq - step, pltpu.roll(s, nc - step, axis=0), 0.0)
            s = s + cmul(sh, mre, mim)
            k, step = k + 1, step * 2
        if d == 0:
            return jnp.where(cidx >= 1, pltpu.roll(s, 1, axis=0), 0.0)
        return jnp.where(cidx < chunks_per_seq - 1, pltpu.roll(s, nc - 1, axis=0), 0.0)

    sf = scan(pst[:, :ns], 0)
    sb = scan(pst[:, ns:], 1)
    carry = jnp.concatenate([sf, sb], axis=1).astype(BF16)
    ytot = y + jnp.dot(carry, wc_ref[0], preferred_element_type=F32)
    for o in range(S5_GROUP):
        o_ref[o] = ytot[:, o * T:(o + 1) * T]


def s5_scan(uts, seqs, kk, wp, wc, scan_mult):
    T = S5_CHUNK
    u3 = [ut.reshape(W_SSM, ut.shape[1] // T, T) for ut in uts]
    blk = [pl.BlockSpec((S5_GROUP,) + u.shape[1:], lambda g: (g, 0, 0)) for u in u3]
    grp = lambda a: pl.BlockSpec((1,) + a.shape[1:], lambda g: (g,) + (0,) * (a.ndim - 1))
    outs = pl.pallas_call(
        functools.partial(_s5_kernel, chunks_per_seq=tuple(s // T for s in seqs)),
        grid=(S5_GROUPS,),
        in_specs=blk + [grp(kk), grp(wp), grp(wc), grp(scan_mult)],
        out_specs=blk,
        out_shape=[jax.ShapeDtypeStruct(u.shape, F32) for u in u3],
        scratch_shapes=[pltpu.VMEM((S5_GROUP * T, S5_GROUP * T), BF16)],
        compiler_params=_cparams(("parallel",)),
        name="s5_scan",
    )(*u3, kk, wp, wc, scan_mult)
    return [o.reshape(W_SSM, -1) for o in outs]


def _s5_post_kernel(yt_ref, w_ref, o_ref):
    y = yt_ref[...].T
    y = 0.5 * y * (1.0 + jnp.tanh(math.sqrt(2.0 / math.pi) * (y + 0.044715 * (y * y * y))))
    z = jnp.dot(y.astype(BF16), w_ref[...], preferred_element_type=F32)
    o_ref[...] = (y * jax.nn.sigmoid(z)).astype(BF16)


def s5_post(yt, glu_w_bf, tm=512):
    n = yt.shape[1]
    return pl.pallas_call(
        _s5_post_kernel,
        grid=(n // tm,),
        in_specs=[pl.BlockSpec((W_SSM, tm), lambda i: (0, i)),
                  pl.BlockSpec((W_SSM, W_SSM), lambda i: (0, 0))],
        out_specs=pl.BlockSpec((tm, W_SSM), lambda i: (i, 0)),
        out_shape=jax.ShapeDtypeStruct((n, W_SSM), BF16),
        compiler_params=_cparams(("parallel",)),
        name="s5_post",
    )(yt, glu_w_bf)


SHORT_HALO = 16


def _short_kernel(prev_ref, cur_ref, next_ref, w_ref, b_ref, v_o, x1_o, x2_o, *, tiles_per_seq):
    i = pl.program_id(0)
    t = i % tiles_per_seq
    cur = cur_ref[...].astype(F32)
    tt = cur.shape[0]
    before = jnp.where(t == 0, 0.0, prev_ref[SHORT_HALO - 1:SHORT_HALO, :].astype(F32))
    after = jnp.where(t == tiles_per_seq - 1, 0.0, next_ref[0:1, :].astype(F32))
    ridx = lax.broadcasted_iota(jnp.int32, cur.shape, 0)
    left = jnp.where(ridx == 0, before, pltpu.roll(cur, 1, axis=0))
    right = jnp.where(ridx == tt - 1, after, pltpu.roll(cur, tt - 1, axis=0))
    w = w_ref[...]
    p = left * w[0:1] + cur * w[1:2] + right * w[2:3] + b_ref[...]
    v_o[...] = p[:, 0:W_HYENA].astype(BF16)
    x1_o[...] = p[:, W_HYENA:2 * W_HYENA].astype(BF16)
    x2_o[...] = p[:, 2 * W_HYENA:].astype(BF16)


def hyena_short_conv(hz, short_w, short_b, seq, tt=512):
    n, c3 = hz.shape
    tps = seq // tt
    hb = tt // SHORT_HALO
    nhb = n // SHORT_HALO
    fixed = lambda i: (0, 0)
    osd = jax.ShapeDtypeStruct((n, W_HYENA), BF16)
    return pl.pallas_call(
        functools.partial(_short_kernel, tiles_per_seq=tps),
        grid=(n // tt,),
        in_specs=[
            pl.BlockSpec((SHORT_HALO, c3), lambda i: (jnp.maximum(i * hb - 1, 0), 0)),
            pl.BlockSpec((tt, c3), lambda i: (i, 0)),
            pl.BlockSpec((SHORT_HALO, c3), lambda i: (jnp.minimum((i + 1) * hb, nhb - 1), 0)),
            pl.BlockSpec((3, c3), fixed),
            pl.BlockSpec((1, c3), fixed),
        ],
        out_specs=[pl.BlockSpec((tt, W_HYENA), lambda i: (i, 0))] * 3,
        out_shape=[osd] * 3,
        compiler_params=_cparams(("parallel",)),
        name="hyena_short_conv",
    )(hz, hz, hz, short_w.astype(F32), short_b.astype(F32).reshape(1, c3))


def _filter_kernel(band_ref, w1_ref, b1_ref, w2_ref, b2_ref, w3_ref, freq_ref, decay_ref, k_o, sum_o, *, l):
    i = pl.program_id(0)
    tr = k_o.shape[0]
    m = i * tr + lax.broadcasted_iota(jnp.int32, (tr, LANES), 0)
    pos = jnp.where(m > l, 2 * l - m, m).astype(F32)
    t = pos / (l - 1)
    ang = ((2.0 * math.pi / l) * pos) * band_ref[...]
    lane = lax.broadcasted_iota(jnp.int32, (tr, LANES), 1)
    feat = jnp.where(lane == 0, t, jnp.where(lane <= HY_BANDS, jnp.cos(ang),
                                              jnp.where(lane <= 2 * HY_BANDS, -jnp.sin(ang), 0.0)))
    fr = freq_ref[...]
    hdn = jnp.sin(fr * (_dot3(feat, w1_ref[...]) + b1_ref[...]))
    hdn = jnp.sin(fr * (_dot3(hdn, w2_ref[...]) + b2_ref[...]))
    f = _dot3(hdn, w3_ref[0])
    reps = f.shape[1] // LANES
    t_all = jnp.concatenate([t] * reps, axis=1)
    m_all = jnp.concatenate([m] * reps, axis=1)
    f = jnp.where(m_all == l, 0.0, f * jnp.exp(-t_all * jnp.abs(decay_ref[0])))
    k_o[...] = f.astype(BF16)

    @pl.when(i == 0)
    def _():
        sum_o[...] = jnp.zeros_like(sum_o)

    sum_o[...] += jnp.sum(jnp.abs(f), axis=0, keepdims=True)


def hyena_filters(l, w1, b1, w2, b2, w3, freq, decay, tr=512):
    f32 = F32
    n_ord = decay.shape[0]
    hid = w1.shape[1]
    cols = n_ord * W_HYENA
    bands = jnp.linspace(1e-4, HY_BANDS - 1, HY_BANDS, dtype=f32)
    band_row = jnp.zeros((1, LANES), f32).at[0, 1:1 + HY_BANDS].set(bands).at[0, 1 + HY_BANDS:1 + 2 * HY_BANDS].set(bands)
    pad2 = lambda a, r, c: jnp.pad(a.astype(f32), ((0, r - a.shape[0]), (0, c - a.shape[1])))
    row = lambda a: pad2(a.reshape(1, -1), 1, LANES)
    w3h = jnp.transpose(w3.astype(f32).reshape(hid, n_ord, 2, W_HYENA), (2, 0, 1, 3)).reshape(2, hid, cols)
    w3h = jnp.pad(w3h, ((0, 0), (0, LANES - hid), (0, 0)))
    dech = jnp.transpose(decay.astype(f32), (1, 0, 2)).reshape(2, 1, cols)
    nt = 2 * l // tr
    fixed = lambda i: (0, 0)
    half = lambda i: (i // (nt // 2), 0, 0)
    k_un, ksum = pl.pallas_call(
        functools.partial(_filter_kernel, l=l),
        grid=(nt,),
        in_specs=[pl.BlockSpec((1, LANES), fixed), pl.BlockSpec((LANES, LANES), fixed), pl.BlockSpec((1, LANES), fixed),
                  pl.BlockSpec((LANES, LANES), fixed), pl.BlockSpec((1, LANES), fixed),
                  pl.BlockSpec((1, LANES, cols), half), pl.BlockSpec((1, LANES), fixed),
                  pl.BlockSpec((1, 1, cols), half)],
        out_specs=[pl.BlockSpec((tr, cols), lambda i: (i, 0)), pl.BlockSpec((8, cols), fixed)],
        out_shape=[jax.ShapeDtypeStruct((2 * l, cols), BF16), jax.ShapeDtypeStruct((8, cols), f32)],
        compiler_params=_cparams(("arbitrary",)),
        name="hyena_filter",
    )(band_row, pad2(w1, LANES, LANES), row(b1), pad2(w2, LANES, LANES), row(b2), w3h, row(freq), dech)
    return k_un, 1.0 / ksum[0:1]


FFT_KB = 4


def _num_k1(n1):
    return n1 // 2 + FFT_KB


def _dft_tables(n1, rows_in):
    nk = _num_k1(n1)
    nh = n1 // 2
    k1 = np.arange(nk)[:, None].astype(np.float64)
    r = np.arange(rows_in)[None, :].astype(np.float64)
    ang = 2.0 * np.pi * k1 * r / n1
    keep = (k1 <= nh).astype(np.float64)
    fwd = np.concatenate([np.cos(ang) * keep, -np.sin(ang) * keep], axis=0)
    wgt = np.where(k1[:nh] == 0, 1.0, 2.0)
    inv = np.concatenate([(np.cos(ang[:nh]) * wgt).T, (-np.sin(ang[:nh]) * wgt).T], axis=1)
    sign = np.cos(np.pi * r).T
    return jnp.asarray(fwd, BF16), jnp.asarray(inv, BF16), jnp.asarray(sign, F32)


def _mid_tables(n1):
    n2 = FFT_N2
    n = n1 * n2
    k = (np.arange(_num_k1(n1))[:, None, None] + n1 * np.arange(n2)[None, :, None]).astype(np.float64)
    m = np.arange(n2)[None, None, :].astype(np.float64)
    ang = 2.0 * np.pi * ((k * m) % n) / n
    gr, gi = np.cos(ang), -np.sin(ang)
    fwd = np.concatenate([np.concatenate([gr, -gi], axis=2), np.concatenate([gi, gr], axis=2)], axis=1)
    hr, hi = np.transpose(gr, (0, 2, 1)), -np.transpose(gi, (0, 2, 1))
    inv = np.concatenate([np.concatenate([hr, -hi], axis=2), np.concatenate([hi, hr], axis=2)], axis=1)
    return jnp.asarray(fwd, BF16), jnp.asarray(inv, BF16)


def _stage1_kernel(z_ref, f_ref, a_ref):
    a_ref[0] = jnp.dot(f_ref[...], z_ref[0], preferred_element_type=F32).astype(BF16)


def fft_stage1(z2, fwd, tc=2048):
    bsz, r, cols = z2.shape
    return pl.pallas_call(
        _stage1_kernel,
        grid=(bsz, cols // tc),
        in_specs=[pl.BlockSpec((1, r, tc), lambda b, j: (b, 0, j)),
                  pl.BlockSpec(fwd.shape, lambda b, j: (0, 0))],
        out_specs=pl.BlockSpec((1, fwd.shape[0], tc), lambda b, j: (b, 0, j)),
        out_shape=jax.ShapeDtypeStruct((bsz, fwd.shape[0], cols), BF16),
        compiler_params=_cparams(("parallel", "parallel")),
        name="fft_stage1",
    )(z2, fwd)


def _mid_kernel(a_ref, g_ref, h_ref, kr_ref, ki_ref, d_ref):
    n2 = FFT_N2
    for j in range(a_ref.shape[2]):
        ab = jnp.concatenate([a_ref[0, 0, j], a_ref[0, 1, j]], axis=0)
        z = jnp.dot(g_ref[j], ab, preferred_element_type=F32)
        zr, zi = z[:n2], z[n2:]
        kr, ki = kr_ref[j], ki_ref[j]
        yb = jnp.concatenate([zr * kr - zi * ki, zr * ki + zi * kr], axis=0).astype(BF16)
        d = jnp.dot(h_ref[j], yb, preferred_element_type=F32)
        d_ref[0, 0, j] = d[:n2].astype(BF16)
        d_ref[0, 1, j] = d[n2:].astype(BF16)


def _mid_fwd_kernel(a_ref, g_ref, s_ref, zr_ref, zi_ref):
    n2 = FFT_N2
    for j in range(a_ref.shape[2]):
        ab = jnp.concatenate([a_ref[0, 0, j], a_ref[0, 1, j]], axis=0)
        z = jnp.dot(g_ref[j], ab, preferred_element_type=F32) * s_ref[...]
        zr_ref[j] = z[:n2]
        zi_ref[j] = z[n2:]


def fft_mid(a5, g, h, kf_re, kf_im, order, kb=FFT_KB):
    bsz, _, n1, n2, c = a5.shape
    kb = min(kb, n1)
    blk = pl.BlockSpec((1, 2, kb, n2, c), lambda k, b: (b, 0, k, 0, 0))
    mat = pl.BlockSpec((kb, 2 * n2, 2 * n2), lambda k, b: (k, 0, 0))
    spec = pl.BlockSpec((kb, n2, c), lambda k, b: (k, 0, order))
    return pl.pallas_call(
        _mid_kernel,
        grid=(n1 // kb, bsz),
        in_specs=[blk, mat, mat, spec, spec],
        out_specs=blk,
        out_shape=jax.ShapeDtypeStruct(a5.shape, BF16),
        compiler_params=_cparams(("parallel", "arbitrary")),
        name="fft_mid",
    )(a5, g, h, kf_re, kf_im)


def fft_mid_fwd(a5, g, col_scale, kb=FFT_KB):
    _, _, n1, n2, c = a5.shape
    kb = min(kb, n1)
    spec = pl.BlockSpec((kb, n2, c), lambda k: (k, 0, 0))
    osd = jax.ShapeDtypeStruct((n1, n2, c), F32)
    return pl.pallas_call(
        _mid_fwd_kernel,
        grid=(n1 // kb,),
        in_specs=[pl.BlockSpec((1, 2, kb, n2, c), lambda k: (0, 0, k, 0, 0)),
                  pl.BlockSpec((kb, 2 * n2, 2 * n2), lambda k: (k, 0, 0)),
                  pl.BlockSpec((1, c), lambda k: (0, 0))],
        out_specs=[spec, spec],
        out_shape=[osd, osd],
        compiler_params=_cparams(("parallel",)),
        name="fft_mid_fwd",
    )(a5, g, col_scale)


def _fin_kernel(d_ref, inv_ref, sign_ref, z_ref, gate_ref, bias_ref, *rest, scale, chain):
    nk = d_ref.shape[1] // 2
    nh = inv_ref.shape[1] // 2
    half = jnp.concatenate([d_ref[0, 0:nh, :], d_ref[0, nk:nk + nh, :]], axis=0)
    nyq = d_ref[0, nh:nh + 1, :].astype(F32)
    conv = (jnp.dot(inv_ref[...], half, preferred_element_type=F32) + sign_ref[...] * nyq) * scale
    z = gate_ref[0].astype(F32) * (conv + z_ref[0].astype(F32) * bias_ref[...])
    zb = z.astype(BF16)
    if chain:
        f_ref, z_o, a_o = rest
        z_o[0] = zb
        a_o[0] = jnp.dot(f_ref[...], zb, preferred_element_type=F32).astype(BF16)
    else:
        (z_o,) = rest
        z_o[0] = zb


def fft_final(d3, inv, sign, z2, gate2, bias_cols, scale, fwd=None, tc=2048):
    bsz, r, cols = z2.shape
    chain = fwd is not None
    col = lambda b, j: (b, 0, j)
    in_specs = [pl.BlockSpec((1, d3.shape[1], tc), col),
                pl.BlockSpec(inv.shape, lambda b, j: (0, 0)),
                pl.BlockSpec(sign.shape, lambda b, j: (0, 0)),
                pl.BlockSpec((1, r, tc), col),
                pl.BlockSpec((1, r, tc), col),
                pl.BlockSpec((1, tc), lambda b, j: (0, j))]
    out_specs = [pl.BlockSpec((1, r, tc), col)]
    out_shape = [jax.ShapeDtypeStruct(z2.shape, BF16)]
    args = [d3, inv, sign, z2, gate2, bias_cols]
    if chain:
        in_specs.append(pl.BlockSpec(fwd.shape, lambda b, j: (0, 0)))
        out_specs.append(pl.BlockSpec((1, fwd.shape[0], tc), col))
        out_shape.append(jax.ShapeDtypeStruct((bsz, fwd.shape[0], cols), BF16))
        args.append(fwd)
    return pl.pallas_call(
        functools.partial(_fin_kernel, scale=scale, chain=chain),
        grid=(bsz, cols // tc),
        in_specs=in_specs,
        out_specs=out_specs,
        out_shape=out_shape,
        compiler_params=_cparams(("parallel", "parallel")),
        name="fft_final",
    )(*args)


def hyena_mixer(hz, bsz, seq, short_w, short_b, filt, filt_scale, bias):
    c = W_HYENA
    n2 = FFT_N2
    n = 2 * seq
    n1 = n // n2
    r = n1 // 2
    nk = _num_k1(n1)
    fwd_half, inv_half, sign_half = _dft_tables(n1, r)
    fwd_full = _dft_tables(n1, n1)[0]
    g, h = _mid_tables(n1)
    ka = fft_stage1(filt.reshape(1, n1, n2 * 2 * c), fwd_full)
    kf_re, kf_im = fft_mid_fwd(ka.reshape(1, 2, nk, n2, 2 * c), g, filt_scale)
    v, x1, x2 = hyena_short_conv(hz, short_w, short_b, seq)
    as2 = lambda t: t.reshape(bsz, r, n2 * c)
    z2 = as2(v)
    a = fft_stage1(z2, fwd_half)
    for order, gate in enumerate((x1, x2)):
        d = fft_mid(a.reshape(bsz, 2, nk, n2, c), g, h, kf_re, kf_im, order)
        bias_cols = jnp.tile(bias[order].astype(F32), n2).reshape(1, n2 * c)
        res = fft_final(d.reshape(bsz, 2 * nk, n2 * c), inv_half, sign_half, z2, as2(gate), bias_cols, 1.0 / n,
                        fwd=fwd_half if order == 0 else None)
        if order == 0:
            z2, a = res
        else:
            (z2,) = res
    return z2.reshape(bsz * seq, c)


def _trunk_layer0(x, c, wts):
    bsz, seq, _ = x.shape
    n = bsz * seq
    mod = adaln_mod(c, wts['ada_w'], wts['ada_b'])
    xf = x.reshape(n, D_MODEL)
    q, k, v, u = in_proj0(xf, mod[0], wts['norm_mix_g'][0], wts['ab_w_in'], wts['na_q_g'], wts['na_k_g'], seq)
    att = neighbourhood_attention(q, k, v, wts['att_bias'], bsz, seq)
    cnv = conv_module(u, wts['cv_dw_w'], wts['cv_dw_b'], wts['cv_ln_g'], wts['cv_ln_b'], seq)
    xf, h, gate, gate_t = out_proj_router(att, cnv, xf, mod[0], wts['ab_w_out'], wts['norm_ffn_g'][0],
                                          wts['router_w'], wts['router_b'], seq)
    xf = moe_ffn(h, gate, gate_t, xf, mod[0], wts['moe_wg'][0], wts['moe_wu'][0], wts['moe_wd'][0], seq)
    ut, hz = in_proj1(xf, mod[1], wts['norm_mix_g'][1], wts['cd_w_u_t'], wts['cd_w_z'], seq)
    return xf, mod, ut, hz


def _trunk_layer1(xf, mod, yt, hz, bsz, seq, wts):
    ssm = s5_post(yt, wts['s5_glu_w'])
    filt, filt_scale = hyena_filters(seq, *wts['hy_mlp'])
    hy = hyena_mixer(hz, bsz, seq, wts['hy_short_w'], wts['hy_short_b'], filt, filt_scale, wts['hy_bias'])
    xf, h, gate, gate_t = out_proj_router(ssm, hy, xf, mod[1], wts['cd_w_out'], wts['norm_ffn_g'][1],
                                          wts['router_w'], wts['router_b'], seq)
    xf = moe_ffn(h, gate, gate_t, xf, mod[1], wts['moe_wg'][1], wts['moe_wu'][1], wts['moe_wd'][1], seq)
    return xf.reshape(bsz, seq, D_MODEL)


def kernel(x_prompt, x_sample, c_prompt, c_sample, ada_w, ada_b, norm_mix_g, norm_ffn_g, router_w, router_b, moe_w_gate, moe_w_up, moe_w_down, ab_w_in, ab_w_out, na_q_g, na_k_g, na_rpb, cv_dw_w, cv_dw_b, cv_ln_g, cv_ln_b, cd_w_in, cd_w_out, s5_lam_re, s5_lam_im, s5_log_dt, s5_b_re, s5_b_im, s5_c_re, s5_c_im, s5_d, s5_glu_w, hy_short_w, hy_short_b, hy_w1, hy_b1, hy_w2, hy_b2, hy_w3, hy_freq, hy_decay, hy_bias):
    s5_tabs = s5_tables(s5_lam_re[0], s5_lam_im[0], s5_log_dt[0], s5_b_re[0], s5_b_im[0],
                        s5_c_re[0], s5_c_im[0], s5_d[0])
    wts = dict(
        ada_w=ada_w, ada_b=ada_b, norm_mix_g=norm_mix_g.astype(F32), norm_ffn_g=norm_ffn_g.astype(F32),
        router_w=router_w, router_b=router_b,
        moe_wg=jnp.swapaxes(moe_w_gate, -1, -2).astype(BF16), moe_wu=jnp.swapaxes(moe_w_up, -1, -2).astype(BF16),
        moe_wd=jnp.swapaxes(moe_w_down, -1, -2).astype(BF16),
        ab_w_in=ab_w_in[0].astype(BF16), ab_w_out=ab_w_out[0].astype(BF16),
        na_q_g=na_q_g[0], na_k_g=na_k_g[0], att_bias=_att_bias_table(na_rpb[0]),
        cv_dw_w=cv_dw_w[0], cv_dw_b=cv_dw_b[0], cv_ln_g=cv_ln_g[0], cv_ln_b=cv_ln_b[0],
        cd_w_u_t=cd_w_in[0][:, :W_SSM].T.astype(BF16), cd_w_z=cd_w_in[0][:, W_SSM:].astype(BF16),
        cd_w_out=cd_w_out[0].astype(BF16),
        s5_glu_w=s5_glu_w[0].astype(BF16),
        hy_short_w=hy_short_w[0], hy_short_b=hy_short_b[0],
        hy_mlp=(hy_w1[0], hy_b1[0], hy_w2[0], hy_b2[0], hy_w3[0], hy_freq[0], hy_decay[0]),
        hy_bias=hy_bias[0],
    )
    xs = (x_prompt, x_sample)
    mids = [_trunk_layer0(x, c, wts) for x, c in zip(xs, (c_prompt, c_sample))]
    yts = s5_scan([m[2] for m in mids], [x.shape[1] for x in xs], *s5_tabs)
    return tuple(_trunk_layer1(m[0], m[1], yt, m[3], x.shape[0], x.shape[1], wts)
                 for m, yt, x in zip(mids, yts, xs))
```

```python
import functools
import math

import numpy as np
import jax
import jax.numpy as jnp
from jax import lax
from jax.experimental import pallas as pl
from jax.experimental.pallas import tpu as pltpu

F32 = jnp.float32
BF16 = jnp.bfloat16
HIGHEST = lax.Precision.HIGHEST

D_MODEL = 1024
DEPTH = 2
GRID_W = 64
W_ATT = 512
W_CONV = 512
W_SSM = 512
W_HYENA = 512
HEAD_DIM = 64
N_HEADS = 8
WIN_R = 8
WIN_C = 16
CONV_W = 31
S5_GROUP = 16
S5_GROUPS = 32
S5_STATE = 64
HY_BANDS = 8
N_EXPERTS = 16
N_GROUPS = 4
D_FF = 512
EPS = 1e-6
NEG_INF = -1e30

VMEM_LIMIT_BYTES = 56 * 1024 * 1024
LANES = 128
SUBLANES = 8

ATT_QROWS = 4
ATT_KROWS = 12
S5_CHUNK = LANES
FFT_N2 = 128


def _cparams(sem):
    return pltpu.CompilerParams(dimension_semantics=sem, vmem_limit_bytes=VMEM_LIMIT_BYTES)


def _mod_kernel(c_ref, w_ref, b_ref, o_ref):
    c = c_ref[...]
    s = c * jax.nn.sigmoid(c)
    o_ref[0] = jnp.dot(s, w_ref[0], precision=HIGHEST, preferred_element_type=F32) + b_ref[0]


def adaln_mod(c, ada_w, ada_b):
    bsz = c.shape[0]
    tn = D_MODEL
    out = pl.pallas_call(
        _mod_kernel,
        grid=(DEPTH, 6 * D_MODEL // tn),
        in_specs=[
            pl.BlockSpec((bsz, D_MODEL), lambda i, j: (0, 0)),
            pl.BlockSpec((1, D_MODEL, tn), lambda i, j: (i, 0, j)),
            pl.BlockSpec((1, 1, tn), lambda i, j: (i, 0, j)),
        ],
        out_specs=pl.BlockSpec((1, bsz, tn), lambda i, j: (i, 0, j)),
        out_shape=jax.ShapeDtypeStruct((DEPTH, bsz, 6 * D_MODEL), F32),
        compiler_params=_cparams(("arbitrary", "arbitrary")),
        name="adaln_mod",
    )(c, ada_w, ada_b.reshape(DEPTH, 1, 6 * D_MODEL))
    return out.reshape(DEPTH, bsz, 6, D_MODEL)


def _dot3(a, b):
    a_hi, b_hi = a.astype(BF16), b.astype(BF16)
    a_lo = (a - a_hi.astype(F32)).astype(BF16)
    b_lo = (b - b_hi.astype(F32)).astype(BF16)
    return (jnp.dot(a_hi, b_hi, preferred_element_type=F32) + jnp.dot(a_lo, b_hi, preferred_element_type=F32)
            + jnp.dot(a_hi, b_lo, preferred_element_type=F32))


def _norm_mod(x, g, shift, scale):
    ms = jnp.mean(x * x, axis=-1, keepdims=True)
    return x * lax.rsqrt(ms + EPS) * g * (1.0 + scale) + shift


def _in0_kernel(x_ref, mod_ref, g_ref, w_ref, hm_ref, qg_ref, kg_ref, q_o, k_o, v_o, u_o):
    m = mod_ref[0]
    h = _norm_mod(x_ref[...], g_ref[...], m[0:1], m[1:2])
    p = jnp.dot(h.astype(BF16), w_ref[...], preferred_element_type=F32)
    q = p[:, 0:W_ATT]
    k = p[:, W_ATT:2 * W_ATT]
    qms = jnp.dot((q * q).astype(BF16), hm_ref[...], preferred_element_type=F32)
    kms = jnp.dot((k * k).astype(BF16), hm_ref[...], preferred_element_type=F32)
    q_o[...] = (q * lax.rsqrt(qms + EPS) * qg_ref[...]).astype(BF16)
    k_o[...] = (k * lax.rsqrt(kms + EPS) * kg_ref[...]).astype(BF16)
    v_o[...] = p[:, 2 * W_ATT:3 * W_ATT].astype(BF16)
    val = p[:, 3 * W_ATT:3 * W_ATT + W_CONV]
    gate = p[:, 3 * W_ATT + W_CONV:]
    u_o[...] = (val * jax.nn.sigmoid(gate)).astype(BF16)


def in_proj0(x, mod, g, w_in_bf, q_gain, k_gain, seq, tm=512):
    n = x.shape[0]
    head_mean = jnp.asarray(np.kron(np.eye(N_HEADS), np.full((HEAD_DIM, HEAD_DIM), 1.0 / HEAD_DIM)), BF16)
    qg = (jnp.tile(q_gain.astype(F32), N_HEADS) * (HEAD_DIM ** -0.5)).reshape(1, W_ATT)
    kg = jnp.tile(k_gain.astype(F32), N_HEADS).reshape(1, W_ATT)
    tok = lambda i: (i, 0)
    fixed = lambda i: (0, 0)
    osd = jax.ShapeDtypeStruct((n, W_ATT), BF16)
    return pl.pallas_call(
        _in0_kernel,
        grid=(n // tm,),
        in_specs=[
            pl.BlockSpec((tm, D_MODEL), tok),
            pl.BlockSpec((1, 6, D_MODEL), lambda i: ((i * tm) // seq, 0, 0)),
            pl.BlockSpec((1, D_MODEL), fixed),
            pl.BlockSpec(w_in_bf.shape, fixed),
            pl.BlockSpec((W_ATT, W_ATT), fixed),
            pl.BlockSpec((1, W_ATT), fixed),
            pl.BlockSpec((1, W_ATT), fixed),
        ],
        out_specs=[pl.BlockSpec((tm, W_ATT), tok)] * 4,
        out_shape=[osd] * 4,
        compiler_params=_cparams(("parallel",)),
        name="in_proj0",
    )(x, mod, g.reshape(1, D_MODEL), w_in_bf, head_mean, qg, kg)


def _att_bias_table(rpb):
    a = np.arange(ATT_QROWS)[:, None, None, None]
    c = np.arange(GRID_W)[None, :, None, None]
    e = np.arange(ATT_KROWS)[None, None, :, None]
    kc = np.arange(GRID_W)[None, None, None, :]
    c0 = np.clip(c - WIN_C // 2, 0, GRID_W - WIN_C)
    col_ok = (kc >= c0) & (kc < c0 + WIN_C)
    rpb = rpb.astype(F32)
    per = 2 * GRID_W
    vrow = jnp.concatenate([rpb[..., WIN_C - 1:],
                            jnp.zeros(rpb.shape[:2] + (per - (2 * WIN_C - 1),), F32),
                            rpb[..., :WIN_C - 1]], axis=-1)
    tcol = jnp.tile(vrow, (1, 1, GRID_W))[..., :GRID_W * (per - 1)]
    tcol = tcol.reshape(rpb.shape[:2] + (GRID_W, per - 1))[..., :GRID_W]
    tables = []
    for case in range(3):
        if case == 0:
            dr = e - a
            row_ok = (e >= 0) & (e < WIN_R)
        elif case == 1:
            dr = e - a - WIN_R // 2
            row_ok = (dr >= -(WIN_R // 2)) & (dr < WIN_R // 2)
        else:
            dr = e - a - (ATT_KROWS - ATT_QROWS)
            row_ok = (e >= ATT_KROWS - WIN_R) & (e < ATT_KROWS)
        ok = np.broadcast_to(row_ok & col_ok, (ATT_QROWS, GRID_W, ATT_KROWS, GRID_W))
        dri = np.clip(dr + WIN_R - 1, 0, 2 * WIN_R - 2)[:, 0, :, 0]
        nq, nk = ATT_QROWS * GRID_W, ATT_KROWS * GRID_W
        t = jnp.concatenate([jnp.concatenate([tcol[:, int(dri[qa, ke])] for ke in range(ATT_KROWS)], axis=-1)
                             for qa in range(ATT_QROWS)], axis=-2)
        tables.append(jnp.where(jnp.asarray(ok.reshape(nq, nk)), t, NEG_INF))
    return jnp.stack(tables).astype(BF16)


def _att_kernel(q_ref, k0, k1, k2, v0, v1, v2, bias_ref, o_ref):
    kt = [k0, k1, k2]
    vt = [v0, v1, v2]
    nkb = len(kt)
    kw = k0.shape[0]
    tq = q_ref.shape[0]
    first = lax.broadcasted_iota(jnp.int32, (tq, LANES), 1) < HEAD_DIM
    for hp in range(N_HEADS // 2):
        ps = slice(hp * LANES, (hp + 1) * LANES)
        qp = q_ref[:, ps]
        res = []
        for sub in range(2):
            h = 2 * hp + sub
            qm = jnp.where(first if sub == 0 else jnp.logical_not(first), qp, jnp.zeros_like(qp))
            s = [lax.dot_general(qm, kt[j][:, ps], (((1,), (1,)), ((), ())), preferred_element_type=F32)
                 + bias_ref[0, h, :, j * kw:(j + 1) * kw].astype(F32) for j in range(nkb)]
            m = s[0].max(axis=-1, keepdims=True)
            for j in range(1, nkb):
                m = jnp.maximum(m, s[j].max(axis=-1, keepdims=True))
            p = [jnp.exp(sj - m) for sj in s]
            l = p[0].sum(axis=-1, keepdims=True)
            for j in range(1, nkb):
                l = l + p[j].sum(axis=-1, keepdims=True)
            o = jnp.dot(p[0].astype(BF16), vt[0][:, ps], preferred_element_type=F32)
            for j in range(1, nkb):
                o = o + jnp.dot(p[j].astype(BF16), vt[j][:, ps], preferred_element_type=F32)
            res.append(o / l)
        o_ref[:, ps] = jnp.where(first, res[0], res[1]).astype(BF16)


def neighbourhood_attention(q, k, v, bias, bsz, seq):
    rows = seq // GRID_W
    nqb = rows // ATT_QROWS
    tq = ATT_QROWS * GRID_W
    nkb = ATT_KROWS // ATT_QROWS
    assert rows % ATT_QROWS == 0 and nqb >= nkb

    def kmap(j):
        def f(i):
            b, r = i // nqb, i % nqb
            return (b * nqb + jnp.clip(r - 1, 0, nqb - nkb) + j, 0)
        return f

    def bias_map(i):
        r = i % nqb
        return (jnp.where(r == 0, 0, jnp.where(r == nqb - 1, 2, 1)), 0, 0, 0)

    kv_specs = [pl.BlockSpec((tq, W_ATT), kmap(j)) for j in range(nkb)]
    return pl.pallas_call(
        _att_kernel,
        grid=(bsz * nqb,),
        in_specs=[pl.BlockSpec((tq, W_ATT), lambda i: (i, 0))] + kv_specs + kv_specs
        + [pl.BlockSpec((1,) + bias.shape[1:], bias_map)],
        out_specs=pl.BlockSpec((tq, W_ATT), lambda i: (i, 0)),
        out_shape=jax.ShapeDtypeStruct(q.shape, BF16),
        compiler_params=_cparams(("parallel",)),
        name="neighbourhood_attention",
    )(q, k, k, k, v, v, v, bias)


CONV_HALO = 16


def _conv_kernel(prev_ref, cur_ref, next_ref, w_ref, b_ref, g_ref, be_ref, o_ref, win_ref, sh_ref, *, tiles_per_seq):
    i = pl.program_id(0)
    t = i % tiles_per_seq
    tt = cur_ref.shape[0]
    half = CONV_W // 2
    prev = prev_ref[...].astype(F32)
    nxt = next_ref[...].astype(F32)
    win_ref[0:CONV_HALO, :] = jnp.where(t == 0, 0.0, prev)
    win_ref[CONV_HALO:CONV_HALO + tt, :] = cur_ref[...].astype(F32)
    win_ref[CONV_HALO + tt:, :] = jnp.where(t == tiles_per_seq - 1, 0.0, nxt)
    span = tt + 2 * CONV_HALO - SUBLANES
    for ph in range(1, SUBLANES):
        sh_ref[ph, 0:span, :] = win_ref[ph:ph + span, :]
    w = w_ref[...]
    acc = jnp.zeros((tt, W_CONV), F32) + b_ref[...]
    for kk in range(CONV_W):
        off = CONV_HALO - half + kk
        ph, base = off % SUBLANES, (off // SUBLANES) * SUBLANES
        tap = win_ref[base:base + tt, :] if ph == 0 else sh_ref[ph, base:base + tt, :]
        acc = acc + tap * w[kk:kk + 1, :]
    mu = jnp.mean(acc, axis=-1, keepdims=True)
    d = acc - mu
    var = jnp.mean(d * d, axis=-1, keepdims=True)
    y = d * lax.rsqrt(var + EPS) * g_ref[...] + be_ref[...]
    o_ref[...] = (y * jax.nn.sigmoid(y)).astype(BF16)


def conv_module(u, dw_w, dw_b, ln_g, ln_b, seq, tt=512):
    n = u.shape[0]
    tps = seq // tt
    hb = tt // CONV_HALO
    nhb = n // CONV_HALO
    row = lambda a: a.astype(F32).reshape(1, W_CONV)
    fixed = lambda i: (0, 0)
    return pl.pallas_call(
        functools.partial(_conv_kernel, tiles_per_seq=tps),
        grid=(n // tt,),
        in_specs=[
            pl.BlockSpec((CONV_HALO, W_CONV), lambda i: (jnp.maximum(i * hb - 1, 0), 0)),
            pl.BlockSpec((tt, W_CONV), lambda i: (i, 0)),
            pl.BlockSpec((CONV_HALO, W_CONV), lambda i: (jnp.minimum((i + 1) * hb, nhb - 1), 0)),
            pl.BlockSpec((CONV_W, W_CONV), fixed),
            pl.BlockSpec((1, W_CONV), fixed),
            pl.BlockSpec((1, W_CONV), fixed),
            pl.BlockSpec((1, W_CONV), fixed),
        ],
        out_specs=pl.BlockSpec((tt, W_CONV), lambda i: (i, 0)),
        out_shape=jax.ShapeDtypeStruct((n, W_CONV), BF16),
        scratch_shapes=[pltpu.VMEM((tt + 2 * CONV_HALO, W_CONV), F32),
                        pltpu.VMEM((SUBLANES, tt + 2 * CONV_HALO, W_CONV), F32)],
        compiler_params=_cparams(("parallel",)),
        name="conv_module",
    )(u, u, u, dw_w.astype(F32), row(dw_b), row(ln_g), row(ln_b))


def _top2_sum(a, b, c, d):
    hi1, lo1 = jnp.maximum(a, b), jnp.minimum(a, b)
    hi2, lo2 = jnp.maximum(c, d), jnp.minimum(c, d)
    return jnp.maximum(hi1, hi2) + jnp.maximum(jnp.minimum(hi1, hi2), jnp.maximum(lo1, lo2))


def _router_gates(scores_t, bias_t):
    per = N_EXPERTS // N_GROUPS
    rows = [scores_t[e:e + 1, :] + bias_t[e:e + 1, :] for e in range(N_EXPERTS)]
    gscore = [_top2_sum(*rows[g * per:(g + 1) * per]) for g in range(N_GROUPS)]
    best = gscore[0]
    best_idx = jnp.zeros_like(best, dtype=jnp.int32)
    for g in range(1, N_GROUPS):
        better = gscore[g] > best
        best = jnp.where(better, gscore[g], best)
        best_idx = jnp.where(better, g, best_idx)
    out_row = lax.broadcasted_iota(jnp.int32, (LANES, scores_t.shape[1]), 0)
    gates = jnp.zeros((LANES, scores_t.shape[1]), F32)
    total = jnp.zeros_like(best)
    for e in range(N_EXPERTS):
        g = e // per
        rank = jnp.zeros_like(best_idx)
        for e2 in range(g * per, (g + 1) * per):
            if e2 == e:
                continue
            if e2 < e:
                ahead = rows[e2] >= rows[e]
            else:
                ahead = rows[e2] > rows[e]
            rank = rank + jnp.where(ahead, 1, 0)
        chosen = jnp.where(best_idx == g, rank, 2) < 2
        gated = jnp.where(chosen, scores_t[e:e + 1, :], 0.0)
        total = total + gated
        gates = jnp.where(out_row == e, gated, gates)
    return gates / total


def _out_kernel(a_ref, b_ref, x_ref, mod_ref, wa_ref, wb_ref, g_ref, rwc_ref, rb_ref, x_o, h_o, gate_o,
                gate_t_o):
    m = mod_ref[0]
    mix = (jnp.dot(a_ref[...], wa_ref[...], preferred_element_type=F32)
           + jnp.dot(b_ref[...], wb_ref[...], preferred_element_type=F32))
    x = x_ref[...] + m[2:3] * mix
    x_o[...] = x
    h = _norm_mod(x, g_ref[...], m[3:4], m[4:5])
    h_hi = h.astype(BF16)
    h_o[...] = h_hi
    h_lo = (h - h_hi.astype(F32)).astype(BF16)
    both = jnp.dot(h_hi, rwc_ref[...], preferred_element_type=F32)
    logits = both[:, :LANES] + both[:, LANES:] + jnp.dot(h_lo, rwc_ref[:, :LANES], preferred_element_type=F32)
    scores_t = jax.nn.sigmoid(logits).T
    gates_t = _router_gates(scores_t, rb_ref[...])
    gate_t_o[...] = gates_t
    gate_o[...] = gates_t.T


def out_proj_router(a, b, x, mod, w_out_bf, g_ffn, router_w, router_b, seq, tm=512):
    n = x.shape[0]
    half = a.shape[1]
    tok = lambda i: (i, 0)
    fixed = lambda i: (0, 0)
    rw = jnp.pad(router_w.astype(F32), ((0, 0), (0, LANES - N_EXPERTS)))
    rw_hi = rw.astype(BF16)
    rw_cat = jnp.concatenate([rw_hi, (rw - rw_hi.astype(F32)).astype(BF16)], axis=1)
    return pl.pallas_call(
        _out_kernel,
        grid=(n // tm,),
        in_specs=[
            pl.BlockSpec((tm, half), tok),
            pl.BlockSpec((tm, half), tok),
            pl.BlockSpec((tm, D_MODEL), tok),
            pl.BlockSpec((1, 6, D_MODEL), lambda i: ((i * tm) // seq, 0, 0)),
            pl.BlockSpec((half, D_MODEL), lambda i: (0, 0)),
            pl.BlockSpec((half, D_MODEL), lambda i: (1, 0)),
            pl.BlockSpec((1, D_MODEL), fixed),
            pl.BlockSpec((D_MODEL, 2 * LANES), fixed),
            pl.BlockSpec((N_EXPERTS, 1), fixed),
        ],
        out_specs=[pl.BlockSpec((tm, D_MODEL), tok), pl.BlockSpec((tm, D_MODEL), tok),
                   pl.BlockSpec((tm, LANES), tok), pl.BlockSpec((LANES, tm), lambda i: (0, i))],
        out_shape=[jax.ShapeDtypeStruct((n, D_MODEL), F32), jax.ShapeDtypeStruct((n, D_MODEL), BF16),
                   jax.ShapeDtypeStruct((n, LANES), F32), jax.ShapeDtypeStruct((LANES, n), F32)],
        compiler_params=_cparams(("parallel",)),
        name="out_proj_router",
    )(a, b, x, mod, w_out_bf, w_out_bf, g_ffn.reshape(1, D_MODEL), rw_cat,
      router_b.astype(F32).reshape(N_EXPERTS, 1))


MOE_COLS = 256
EXPERTS_PER_GROUP = N_EXPERTS // N_GROUPS


def _moe_kernel(h_ref, gate_ref, gate_t_ref, x_ref, mod_ref, tri_ref, wg_ref, wu_ref, wd_ref, o_ref,
                rankc_ref, rankr_ref, ht_ref, xg_ref, gg_ref, yg_ref, acc_ref, cnt_ref):
    e = pl.program_id(1)
    g = e // EXPERTS_PER_GROUP
    j = e % EXPERTS_PER_GROUP
    T = h_ref.shape[0]
    R = MOE_COLS

    @pl.when(e == 0)
    def _():
        er = lax.broadcasted_iota(jnp.int32, (LANES, LANES), 0)
        ec = lax.broadcasted_iota(jnp.int32, (LANES, LANES), 1)
        sel_c = jnp.where((er < N_EXPERTS) & (er // EXPERTS_PER_GROUP == ec), 1.0, 0.0).astype(BF16)
        sel_r = jnp.where((ec < N_EXPERTS) & (ec // EXPERTS_PER_GROUP == er), 1.0, 0.0).astype(BF16)
        chosen_c = jnp.where(gate_ref[...] > 0.0, 1.0, 0.0).astype(BF16)
        memb_c = jnp.dot(chosen_c, sel_c, preferred_element_type=F32) > 0.0
        rank_c = jnp.dot(tri_ref[...], jnp.where(memb_c, 1.0, 0.0).astype(BF16), preferred_element_type=F32)
        rankc_ref[...] = jnp.where(memb_c, rank_c, -1.0)
        chosen_r = jnp.where(gate_t_ref[...] > 0.0, 1.0, 0.0).astype(BF16)
        memb_r = jnp.dot(sel_r, chosen_r, preferred_element_type=F32) > 0.0
        ones_r = jnp.where(memb_r, 1.0, 0.0)
        rank_r = lax.dot_general(ones_r.astype(BF16), tri_ref[...], (((1,), (1,)), ((), ())),
                                 preferred_element_type=F32)
        rankr_ref[...] = jnp.where(memb_r, rank_r, -1.0)
        for gi in range(N_GROUPS):
            cnt_ref[gi] = jnp.sum(ones_r[gi:gi + 1, :]).astype(jnp.int32)
        ht_ref[...] = h_ref[...].astype(F32).T.astype(BF16)
        acc_ref[...] = jnp.zeros_like(acc_ref)

    nch = (cnt_ref[g] + (R - 1)) // R

    @pl.when(j == 0)
    def _():
        lane_t = lax.broadcasted_iota(jnp.int32, (T, LANES), 1)
        rc = jnp.sum(jnp.where(lane_t == g, rankc_ref[...], 0.0), axis=1, keepdims=True)
        col = lax.broadcasted_iota(jnp.int32, (T, R), 1).astype(F32)
        gt = gate_t_ref[...]
        g_hi = gt.astype(BF16)
        g_lo = (gt - g_hi.astype(F32)).astype(BF16)

        def gather(c, carry):
            pt = jnp.where(rc - (c * R).astype(F32) == col, 1.0, 0.0).astype(BF16)
            xg_ref[c] = jnp.dot(ht_ref[...], pt, preferred_element_type=F32).astype(BF16)
            gg_ref[c] = (jnp.dot(g_hi, pt, preferred_element_type=F32)
                         + jnp.dot(g_lo, pt, preferred_element_type=F32))
            yg_ref[c] = jnp.zeros((D_MODEL, R), F32)
            return carry

        lax.fori_loop(0, nch, gather, 0)

    def ffn(c, carry):
        xc = xg_ref[c]
        a = jnp.dot(wg_ref[0], xc, preferred_element_type=F32)
        u = jnp.dot(wu_ref[0], xc, preferred_element_type=F32)
        hid = (a * jax.nn.sigmoid(a)) * u * gg_ref[c, pl.ds(e, 1), :]
        yg_ref[c] += jnp.dot(wd_ref[0], hid.astype(BF16), preferred_element_type=F32)
        return carry

    lax.fori_loop(0, nch, ffn, 0)

    @pl.when(j == EXPERTS_PER_GROUP - 1)
    def _():
        rr = rankr_ref[pl.ds(g, 1), :]
        row = lax.broadcasted_iota(jnp.int32, (R, T), 0).astype(F32)

        def scatter(c, carry):
            p = jnp.where(rr - (c * R).astype(F32) == row, 1.0, 0.0).astype(BF16)
            acc_ref[...] += jnp.dot(yg_ref[c].astype(BF16), p, preferred_element_type=F32)
            return carry

        lax.fori_loop(0, nch, scatter, 0)

    @pl.when(e == N_EXPERTS - 1)
    def _():
        o_ref[...] = x_ref[...] + mod_ref[0][5:6] * acc_ref[...].T


def moe_ffn(h, gate, gate_t, x, mod, wgt_bf, wut_bf, wdt_bf, seq, tm=1024):
    n = x.shape[0]
    tm = min(tm, seq)
    nch = tm // MOE_COLS
    tok = lambda i, e: (i, 0)
    tri = jnp.asarray(np.tril(np.ones((tm, tm), np.float32), -1), BF16)
    return pl.pallas_call(
        _moe_kernel,
        grid=(n // tm, N_EXPERTS),
        in_specs=[
            pl.BlockSpec((tm, D_MODEL), tok),
            pl.BlockSpec((tm, LANES), tok),
            pl.BlockSpec((LANES, tm), lambda i, e: (0, i)),
            pl.BlockSpec((tm, D_MODEL), tok),
            pl.BlockSpec((1, 6, D_MODEL), lambda i, e: ((i * tm) // seq, 0, 0)),
            pl.BlockSpec((tm, tm), lambda i, e: (0, 0)),
            pl.BlockSpec((1, D_FF, D_MODEL), lambda i, e: (e, 0, 0)),
            pl.BlockSpec((1, D_FF, D_MODEL), lambda i, e: (e, 0, 0)),
            pl.BlockSpec((1, D_MODEL, D_FF), lambda i, e: (e, 0, 0)),
        ],
        out_specs=pl.BlockSpec((tm, D_MODEL), tok),
        out_shape=jax.ShapeDtypeStruct((n, D_MODEL), F32),
        scratch_shapes=[pltpu.VMEM((tm, LANES), F32), pltpu.VMEM((LANES, tm), F32),
                        pltpu.VMEM((D_MODEL, tm), BF16), pltpu.VMEM((nch, D_MODEL, MOE_COLS), BF16),
                        pltpu.VMEM((nch, LANES, MOE_COLS), F32), pltpu.VMEM((nch, D_MODEL, MOE_COLS), F32),
                        pltpu.VMEM((D_MODEL, tm), F32), pltpu.SMEM((N_GROUPS,), jnp.int32)],
        compiler_params=_cparams(("parallel", "arbitrary")),
        name="moe_ffn",
    )(h, gate, gate_t, x, mod, tri, wgt_bf, wut_bf, wdt_bf)


def _in1_kernel(x_ref, mod_ref, g_ref, wut_ref, wz_ref, ut_o, hz_o):
    m = mod_ref[0]
    h = _norm_mod(x_ref[...], g_ref[...], m[0:1], m[1:2]).astype(BF16)
    ut_o[...] = lax.dot_general(wut_ref[...], h, (((1,), (1,)), ((), ())),
                                preferred_element_type=F32).astype(BF16)
    hz_o[...] = jnp.dot(h, wz_ref[...], preferred_element_type=F32).astype(BF16)


def in_proj1(x, mod, g, w_u_t_bf, w_z_bf, seq, tm=512):
    n = x.shape[0]
    fixed = lambda i: (0, 0)
    return pl.pallas_call(
        _in1_kernel,
        grid=(n // tm,),
        in_specs=[
            pl.BlockSpec((tm, D_MODEL), lambda i: (i, 0)),
            pl.BlockSpec((1, 6, D_MODEL), lambda i: ((i * tm) // seq, 0, 0)),
            pl.BlockSpec((1, D_MODEL), fixed),
            pl.BlockSpec(w_u_t_bf.shape, fixed),
            pl.BlockSpec(w_z_bf.shape, fixed),
        ],
        out_specs=[pl.BlockSpec((W_SSM, tm), lambda i: (0, i)),
                   pl.BlockSpec((tm, 3 * W_HYENA), lambda i: (i, 0))],
        out_shape=[jax.ShapeDtypeStruct((W_SSM, n), BF16), jax.ShapeDtypeStruct((n, 3 * W_HYENA), BF16)],
        compiler_params=_cparams(("parallel",)),
        name="in_proj1",
    )(x, mod, g.reshape(1, D_MODEL), w_u_t_bf, w_z_bf)


def s5_tables(lam_re, lam_im, log_dt, b_re, b_im, c_re, c_im, d_skip):
    T = S5_CHUNK
    f32 = F32
    lags = jnp.arange(T, dtype=f32)

    def disc(d):
        lr, li = lam_re[d].astype(f32), lam_im[d].astype(f32)
        dt = jnp.exp(log_dt[d].astype(f32))[:, None]
        mag = jnp.exp(lr * dt)
        ab_re, ab_im = mag * jnp.cos(li * dt), mag * jnp.sin(li * dt)
        den = lr * lr + li * li
        f_re = ((ab_re - 1.0) * lr + ab_im * li) / den
        f_im = (ab_im * lr - (ab_re - 1.0) * li) / den
        br, bi = b_re[d].astype(f32), b_im[d].astype(f32)
        bb_re = f_re[..., None] * br - f_im[..., None] * bi
        bb_im = f_re[..., None] * bi + f_im[..., None] * br

        def power(p):
            ang = li * dt
            mg = jnp.exp(p[:, None, None] * (lr * dt)[None])
            return mg * jnp.cos(p[:, None, None] * ang[None]), mg * jnp.sin(p[:, None, None] * ang[None])
        return bb_re, bb_im, c_re[d].astype(f32), c_im[d].astype(f32), power

    hp = dict(precision=HIGHEST)
    tabs = []
    for d in range(2):
        bb_re, bb_im, cr, ci, power = disc(d)
        pr, pi = power(lags)
        cb_rr = jnp.einsum('gon,lgn,gni->lgoi', cr, pr, bb_re, **hp)
        cb_ii = jnp.einsum('gon,lgn,gni->lgoi', cr, pi, bb_im, **hp)
        cb_ri = jnp.einsum('gon,lgn,gni->lgoi', ci, pr, bb_im, **hp)
        cb_ir = jnp.einsum('gon,lgn,gni->lgoi', ci, pi, bb_re, **hp)
        kern = cb_rr - cb_ii - cb_ri - cb_ir
        qp = (T - 1.0 - lags) if d == 0 else lags
        qr, qi = power(qp)
        wp_re = jnp.einsum('tgn,gni->gitn', qr, bb_re) - jnp.einsum('tgn,gni->gitn', qi, bb_im)
        wp_im = jnp.einsum('tgn,gni->gitn', qr, bb_im) + jnp.einsum('tgn,gni->gitn', qi, bb_re)
        rp = (lags + 1.0) if d == 0 else (T - lags)
        rr, ri = power(rp)
        m_re = jnp.einsum('gon,tgn->gnot', cr, rr) - jnp.einsum('gon,tgn->gnot', ci, ri)
        m_im = jnp.einsum('gon,tgn->gnot', cr, ri) + jnp.einsum('gon,tgn->gnot', ci, rr)
        levels = 2.0 ** jnp.arange(16, dtype=f32) * T
        ar, ai = power(levels)
        tabs.append((kern, wp_re, wp_im, m_re, -m_im, ar, ai))

    kf, kb = tabs[0][0], tabs[1][0]
    skip = jnp.eye(S5_GROUP, dtype=f32)[None] * d_skip.astype(f32).reshape(S5_GROUPS, S5_GROUP, 1)
    k0 = kf[0] + kb[0] + skip
    kk = jnp.concatenate([k0[None], kf[1:], jnp.zeros_like(k0)[None], kb[:0:-1]], axis=0)
    kk = jnp.transpose(kk, (1, 3, 2, 0))
    g_, i_, o_ = kk.shape[:3]
    kk = kk.reshape(g_, i_ * o_, 2 * T)
    w_state = jnp.concatenate([tabs[0][1], tabs[0][2], tabs[1][1], tabs[1][2]], axis=-1)
    wp = w_state.reshape(g_, i_ * T, 4 * S5_STATE).astype(BF16)
    wc = jnp.concatenate([tabs[0][3], tabs[0][4], tabs[1][3], tabs[1][4]], axis=1)
    wc = wc.reshape(g_, 4 * S5_STATE, o_ * T).astype(BF16)
    mult = []
    for d in range(2):
        ar, ai = tabs[d][5], tabs[d][6]
        mult += [jnp.concatenate([ar, ar], axis=-1), jnp.concatenate([-ai, ai], axis=-1)]
    scan_mult = jnp.transpose(jnp.stack(mult, axis=2), (1, 0, 2, 3))
    return kk, wp, wc, scan_mult


def _s5_kernel(*refs, chunks_per_seq):
    nt = len(chunks_per_seq)
    u_refs, (kk_ref, wp_ref, wc_ref, mult_ref) = refs[:nt], refs[nt:nt + 4]
    o_refs, w_ref = refs[nt + 4:2 * nt + 4], refs[2 * nt + 4]
    T = S5_CHUNK

    def build(i, carry):
        for o in range(S5_GROUP):
            row = kk_ref[0, pl.ds(i * S5_GROUP + o, 1), :]
            toe = pltpu.roll(jnp.broadcast_to(row, (T, 2 * T)), 0, axis=1, stride=1, stride_axis=0)
            w_ref[pl.ds(pl.multiple_of(i * T, T), T), o * T:(o + 1) * T] = toe[:, :T].astype(BF16)
        return carry

    lax.fori_loop(0, S5_GROUP, build, 0)
    for u_ref, o_ref, cps in zip(u_refs, o_refs, chunks_per_seq):
        _s5_apply(u_ref, o_ref, w_ref, wp_ref, wc_ref, mult_ref, cps)


def _s5_apply(u_ref, o_ref, w_ref, wp_ref, wc_ref, mult_ref, chunks_per_seq):
    T = S5_CHUNK
    ns = 2 * S5_STATE
    x = jnp.concatenate([u_ref[i] for i in range(S5_GROUP)], axis=1)
    y = jnp.dot(x, w_ref[...], preferred_element_type=F32)
    pst = jnp.dot(x, wp_ref[0], preferred_element_type=F32)
    nc = y.shape[0]
    cidx = lax.broadcasted_iota(jnp.int32, (nc, ns), 0) % chunks_per_seq

    def cmul(s, mre, mim):
        return s * mre + pltpu.roll(s, S5_STATE, axis=1) * mim

    def scan(p, d):
        s = p
        k, step = 0, 1
        while step < chunks_per_seq:
            mre = mult_ref[0, k, 2 * d:2 * d + 1, :]
            mim = mult_ref[0, k, 2 * d + 1:2 * d + 2, :]
            if d == 0:
                sh = jnp.where(cidx >= step, pltpu.roll(s, step, axis=0), 0.0)
            else:
                sh = jnp.where(cidx < chunks_per_seq - step, pltpu.roll(s, nc - step, axis=0), 0.0)
            s = s + cmul(sh, mre, mim)
            k, step = k + 1, step * 2
        if d == 0:
            return jnp.where(cidx >= 1, pltpu.roll(s, 1, axis=0), 0.0)
        return jnp.where(cidx < chunks_per_seq - 1, pltpu.roll(s, nc - 1, axis=0), 0.0)

    sf = scan(pst[:, :ns], 0)
    sb = scan(pst[:, ns:], 1)
    carry = jnp.concatenate([sf, sb], axis=1).astype(BF16)
    ytot = y + jnp.dot(carry, wc_ref[0], preferred_element_type=F32)
    for o in range(S5_GROUP):
        o_ref[o] = ytot[:, o * T:(o + 1) * T]


def s5_scan(uts, seqs, kk, wp, wc, scan_mult):
    T = S5_CHUNK
    u3 = [ut.reshape(W_SSM, ut.shape[1] // T, T) for ut in uts]
    blk = [pl.BlockSpec((S5_GROUP,) + u.shape[1:], lambda g: (g, 0, 0)) for u in u3]
    grp = lambda a: pl.BlockSpec((1,) + a.shape[1:], lambda g: (g,) + (0,) * (a.ndim - 1))
    outs = pl.pallas_call(
        functools.partial(_s5_kernel, chunks_per_seq=tuple(s // T for s in seqs)),
        grid=(S5_GROUPS,),
        in_specs=blk + [grp(kk), grp(wp), grp(wc), grp(scan_mult)],
        out_specs=blk,
        out_shape=[jax.ShapeDtypeStruct(u.shape, F32) for u in u3],
        scratch_shapes=[pltpu.VMEM((S5_GROUP * T, S5_GROUP * T), BF16)],
        compiler_params=_cparams(("parallel",)),
        name="s5_scan",
    )(*u3, kk, wp, wc, scan_mult)
    return [o.reshape(W_SSM, -1) for o in outs]


def _s5_post_kernel(yt_ref, w_ref, o_ref):
    y = yt_ref[...].T
    y = 0.5 * y * (1.0 + jnp.tanh(math.sqrt(2.0 / math.pi) * (y + 0.044715 * (y * y * y))))
    z = jnp.dot(y.astype(BF16), w_ref[...], preferred_element_type=F32)
    o_ref[...] = (y * jax.nn.sigmoid(z)).astype(BF16)


def s5_post(yt, glu_w_bf, tm=512):
    n = yt.shape[1]
    return pl.pallas_call(
        _s5_post_kernel,
        grid=(n // tm,),
        in_specs=[pl.BlockSpec((W_SSM, tm), lambda i: (0, i)),
                  pl.BlockSpec((W_SSM, W_SSM), lambda i: (0, 0))],
        out_specs=pl.BlockSpec((tm, W_SSM), lambda i: (i, 0)),
        out_shape=jax.ShapeDtypeStruct((n, W_SSM), BF16),
        compiler_params=_cparams(("parallel",)),
        name="s5_post",
    )(yt, glu_w_bf)


SHORT_HALO = 16


def _short_kernel(prev_ref, cur_ref, next_ref, w_ref, b_ref, v_o, x1_o, x2_o, *, tiles_per_seq):
    i = pl.program_id(0)
    t = i % tiles_per_seq
    cur = cur_ref[...].astype(F32)
    tt = cur.shape[0]
    before = jnp.where(t == 0, 0.0, prev_ref[SHORT_HALO - 1:SHORT_HALO, :].astype(F32))
    after = jnp.where(t == tiles_per_seq - 1, 0.0, next_ref[0:1, :].astype(F32))
    ridx = lax.broadcasted_iota(jnp.int32, cur.shape, 0)
    left = jnp.where(ridx == 0, before, pltpu.roll(cur, 1, axis=0))
    right = jnp.where(ridx == tt - 1, after, pltpu.roll(cur, tt - 1, axis=0))
    w = w_ref[...]
    p = left * w[0:1] + cur * w[1:2] + right * w[2:3] + b_ref[...]
    v_o[...] = p[:, 0:W_HYENA].astype(BF16)
    x1_o[...] = p[:, W_HYENA:2 * W_HYENA].astype(BF16)
    x2_o[...] = p[:, 2 * W_HYENA:].astype(BF16)


def hyena_short_conv(hz, short_w, short_b, seq, tt=512):
    n, c3 = hz.shape
    tps = seq // tt
    hb = tt // SHORT_HALO
    nhb = n // SHORT_HALO
    fixed = lambda i: (0, 0)
    osd = jax.ShapeDtypeStruct((n, W_HYENA), BF16)
    return pl.pallas_call(
        functools.partial(_short_kernel, tiles_per_seq=tps),
        grid=(n // tt,),
        in_specs=[
            pl.BlockSpec((SHORT_HALO, c3), lambda i: (jnp.maximum(i * hb - 1, 0), 0)),
            pl.BlockSpec((tt, c3), lambda i: (i, 0)),
            pl.BlockSpec((SHORT_HALO, c3), lambda i: (jnp.minimum((i + 1) * hb, nhb - 1), 0)),
            pl.BlockSpec((3, c3), fixed),
            pl.BlockSpec((1, c3), fixed),
        ],
        out_specs=[pl.BlockSpec((tt, W_HYENA), lambda i: (i, 0))] * 3,
        out_shape=[osd] * 3,
        compiler_params=_cparams(("parallel",)),
        name="hyena_short_conv",
    )(hz, hz, hz, short_w.astype(F32), short_b.astype(F32).reshape(1, c3))


def _filter_kernel(band_ref, w1_ref, b1_ref, w2_ref, b2_ref, w3_ref, freq_ref, decay_ref, k_o, sum_o, *, l):
    i = pl.program_id(0)
    tr = k_o.shape[0]
    m = i * tr + lax.broadcasted_iota(jnp.int32, (tr, LANES), 0)
    pos = jnp.where(m > l, 2 * l - m, m).astype(F32)
    t = pos / (l - 1)
    ang = ((2.0 * math.pi / l) * pos) * band_ref[...]
    lane = lax.broadcasted_iota(jnp.int32, (tr, LANES), 1)
    feat = jnp.where(lane == 0, t, jnp.where(lane <= HY_BANDS, jnp.cos(ang),
                                              jnp.where(lane <= 2 * HY_BANDS, -jnp.sin(ang), 0.0)))
    fr = freq_ref[...]
    hdn = jnp.sin(fr * (_dot3(feat, w1_ref[...]) + b1_ref[...]))
    hdn = jnp.sin(fr * (_dot3(hdn, w2_ref[...]) + b2_ref[...]))
    f = _dot3(hdn, w3_ref[0])
    reps = f.shape[1] // LANES
    t_all = jnp.concatenate([t] * reps, axis=1)
    m_all = jnp.concatenate([m] * reps, axis=1)
    f = jnp.where(m_all == l, 0.0, f * jnp.exp(-t_all * jnp.abs(decay_ref[0])))
    k_o[...] = f.astype(BF16)

    @pl.when(i == 0)
    def _():
        sum_o[...] = jnp.zeros_like(sum_o)

    sum_o[...] += jnp.sum(jnp.abs(f), axis=0, keepdims=True)


def hyena_filters(l, w1, b1, w2, b2, w3, freq, decay, tr=512):
    f32 = F32
    n_ord = decay.shape[0]
    hid = w1.shape[1]
    cols = n_ord * W_HYENA
    bands = jnp.linspace(1e-4, HY_BANDS - 1, HY_BANDS, dtype=f32)
    band_row = jnp.zeros((1, LANES), f32).at[0, 1:1 + HY_BANDS].set(bands).at[0, 1 + HY_BANDS:1 + 2 * HY_BANDS].set(bands)
    pad2 = lambda a, r, c: jnp.pad(a.astype(f32), ((0, r - a.shape[0]), (0, c - a.shape[1])))
    row = lambda a: pad2(a.reshape(1, -1), 1, LANES)
    w3h = jnp.transpose(w3.astype(f32).reshape(hid, n_ord, 2, W_HYENA), (2, 0, 1, 3)).reshape(2, hid, cols)
    w3h = jnp.pad(w3h, ((0, 0), (0, LANES - hid), (0, 0)))
    dech = jnp.transpose(decay.astype(f32), (1, 0, 2)).reshape(2, 1, cols)
    nt = 2 * l // tr
    fixed = lambda i: (0, 0)
    half = lambda i: (i // (nt // 2), 0, 0)
    k_un, ksum = pl.pallas_call(
        functools.partial(_filter_kernel, l=l),
        grid=(nt,),
        in_specs=[pl.BlockSpec((1, LANES), fixed), pl.BlockSpec((LANES, LANES), fixed), pl.BlockSpec((1, LANES), fixed),
                  pl.BlockSpec((LANES, LANES), fixed), pl.BlockSpec((1, LANES), fixed),
                  pl.BlockSpec((1, LANES, cols), half), pl.BlockSpec((1, LANES), fixed),
                  pl.BlockSpec((1, 1, cols), half)],
        out_specs=[pl.BlockSpec((tr, cols), lambda i: (i, 0)), pl.BlockSpec((8, cols), fixed)],
        out_shape=[jax.ShapeDtypeStruct((2 * l, cols), BF16), jax.ShapeDtypeStruct((8, cols), f32)],
        compiler_params=_cparams(("arbitrary",)),
        name="hyena_filter",
    )(band_row, pad2(w1, LANES, LANES), row(b1), pad2(w2, LANES, LANES), row(b2), w3h, row(freq), dech)
    return k_un, 1.0 / ksum[0:1]


FFT_KB = 4


def _num_k1(n1):
    return n1 // 2 + FFT_KB


def _dft_tables(n1, rows_in):
    nk = _num_k1(n1)
    nh = n1 // 2
    k1 = np.arange(nk)[:, None].astype(np.float64)
    r = np.arange(rows_in)[None, :].astype(np.float64)
    ang = 2.0 * np.pi * k1 * r / n1
    keep = (k1 <= nh).astype(np.float64)
    fwd = np.concatenate([np.cos(ang) * keep, -np.sin(ang) * keep], axis=0)
    wgt = np.where(k1[:nh] == 0, 1.0, 2.0)
    inv = np.concatenate([(np.cos(ang[:nh]) * wgt).T, (-np.sin(ang[:nh]) * wgt).T], axis=1)
    sign = np.cos(np.pi * r).T
    return jnp.asarray(fwd, BF16), jnp.asarray(inv, BF16), jnp.asarray(sign, F32)


def _mid_tables(n1):
    n2 = FFT_N2
    n = n1 * n2
    k = (np.arange(_num_k1(n1))[:, None, None] + n1 * np.arange(n2)[None, :, None]).astype(np.float64)
    m = np.arange(n2)[None, None, :].astype(np.float64)
    ang = 2.0 * np.pi * ((k * m) % n) / n
    gr, gi = np.cos(ang), -np.sin(ang)
    fwd = np.concatenate([np.concatenate([gr, -gi], axis=2), np.concatenate([gi, gr], axis=2)], axis=1)
    hr, hi = np.transpose(gr, (0, 2, 1)), -np.transpose(gi, (0, 2, 1))
    inv = np.concatenate([np.concatenate([hr, -hi], axis=2), np.concatenate([hi, hr], axis=2)], axis=1)
    return jnp.asarray(fwd, BF16), jnp.asarray(inv, BF16)


def _stage1_kernel(z_ref, f_ref, a_ref):
    a_ref[0] = jnp.dot(f_ref[...], z_ref[0], preferred_element_type=F32).astype(BF16)


def fft_stage1(z2, fwd, tc=2048):
    bsz, r, cols = z2.shape
    return pl.pallas_call(
        _stage1_kernel,
        grid=(bsz, cols // tc),
        in_specs=[pl.BlockSpec((1, r, tc), lambda b, j: (b, 0, j)),
                  pl.BlockSpec(fwd.shape, lambda b, j: (0, 0))],
        out_specs=pl.BlockSpec((1, fwd.shape[0], tc), lambda b, j: (b, 0, j)),
        out_shape=jax.ShapeDtypeStruct((bsz, fwd.shape[0], cols), BF16),
        compiler_params=_cparams(("parallel", "parallel")),
        name="fft_stage1",
    )(z2, fwd)


def _mid_kernel(a_ref, g_ref, h_ref, kr_ref, ki_ref, d_ref):
    n2 = FFT_N2
    for j in range(a_ref.shape[2]):
        ab = jnp.concatenate([a_ref[0, 0, j], a_ref[0, 1, j]], axis=0)
        z = jnp.dot(g_ref[j], ab, preferred_element_type=F32)
        zr, zi = z[:n2], z[n2:]
        kr, ki = kr_ref[j], ki_ref[j]
        yb = jnp.concatenate([zr * kr - zi * ki, zr * ki + zi * kr], axis=0).astype(BF16)
        d = jnp.dot(h_ref[j], yb, preferred_element_type=F32)
        d_ref[0, 0, j] = d[:n2].astype(BF16)
        d_ref[0, 1, j] = d[n2:].astype(BF16)


def _mid_fwd_kernel(a_ref, g_ref, s_ref, zr_ref, zi_ref):
    n2 = FFT_N2
    for j in range(a_ref.shape[2]):
        ab = jnp.concatenate([a_ref[0, 0, j], a_ref[0, 1, j]], axis=0)
        z = jnp.dot(g_ref[j], ab, preferred_element_type=F32) * s_ref[...]
        zr_ref[j] = z[:n2]
        zi_ref[j] = z[n2:]


def fft_mid(a5, g, h, kf_re, kf_im, order, kb=FFT_KB):
    bsz, _, n1, n2, c = a5.shape
    kb = min(kb, n1)
    blk = pl.BlockSpec((1, 2, kb, n2, c), lambda k, b: (b, 0, k, 0, 0))
    mat = pl.BlockSpec((kb, 2 * n2, 2 * n2), lambda k, b: (k, 0, 0))
    spec = pl.BlockSpec((kb, n2, c), lambda k, b: (k, 0, order))
    return pl.pallas_call(
        _mid_kernel,
        grid=(n1 // kb, bsz),
        in_specs=[blk, mat, mat, spec, spec],
        out_specs=blk,
        out_shape=jax.ShapeDtypeStruct(a5.shape, BF16),
        compiler_params=_cparams(("parallel", "arbitrary")),
        name="fft_mid",
    )(a5, g, h, kf_re, kf_im)


def fft_mid_fwd(a5, g, col_scale, kb=FFT_KB):
    _, _, n1, n2, c = a5.shape
    kb = min(kb, n1)
    spec = pl.BlockSpec((kb, n2, c), lambda k: (k, 0, 0))
    osd = jax.ShapeDtypeStruct((n1, n2, c), F32)
    return pl.pallas_call(
        _mid_fwd_kernel,
        grid=(n1 // kb,),
        in_specs=[pl.BlockSpec((1, 2, kb, n2, c), lambda k: (0, 0, k, 0, 0)),
                  pl.BlockSpec((kb, 2 * n2, 2 * n2), lambda k: (k, 0, 0)),
                  pl.BlockSpec((1, c), lambda k: (0, 0))],
        out_specs=[spec, spec],
        out_shape=[osd, osd],
        compiler_params=_cparams(("parallel",)),
        name="fft_mid_fwd",
    )(a5, g, col_scale)


def _fin_kernel(d_ref, inv_ref, sign_ref, z_ref, gate_ref, bias_ref, *rest, scale, chain):
    nk = d_ref.shape[1] // 2
    nh = inv_ref.shape[1] // 2
    half = jnp.concatenate([d_ref[0, 0:nh, :], d_ref[0, nk:nk + nh, :]], axis=0)
    nyq = d_ref[0, nh:nh + 1, :].astype(F32)
    conv = (jnp.dot(inv_ref[...], half, preferred_element_type=F32) + sign_ref[...] * nyq) * scale
    z = gate_ref[0].astype(F32) * (conv + z_ref[0].astype(F32) * bias_ref[...])
    zb = z.astype(BF16)
    if chain:
        f_ref, z_o, a_o = rest
        z_o[0] = zb
        a_o[0] = jnp.dot(f_ref[...], zb, preferred_element_type=F32).astype(BF16)
    else:
        (z_o,) = rest
        z_o[0] = zb


def fft_final(d3, inv, sign, z2, gate2, bias_cols, scale, fwd=None, tc=2048):
    bsz, r, cols = z2.shape
    chain = fwd is not None
    col = lambda b, j: (b, 0, j)
    in_specs = [pl.BlockSpec((1, d3.shape[1], tc), col),
                pl.BlockSpec(inv.shape, lambda b, j: (0, 0)),
                pl.BlockSpec(sign.shape, lambda b, j: (0, 0)),
                pl.BlockSpec((1, r, tc), col),
                pl.BlockSpec((1, r, tc), col),
                pl.BlockSpec((1, tc), lambda b, j: (0, j))]
    out_specs = [pl.BlockSpec((1, r, tc), col)]
    out_shape = [jax.ShapeDtypeStruct(z2.shape, BF16)]
    args = [d3, inv, sign, z2, gate2, bias_cols]
    if chain:
        in_specs.append(pl.BlockSpec(fwd.shape, lambda b, j: (0, 0)))
        out_specs.append(pl.BlockSpec((1, fwd.shape[0], tc), col))
        out_shape.append(jax.ShapeDtypeStruct((bsz, fwd.shape[0], cols), BF16))
        args.append(fwd)
    return pl.pallas_call(
        functools.partial(_fin_kernel, scale=scale, chain=chain),
        grid=(bsz, cols // tc),
        in_specs=in_specs,
        out_specs=out_specs,
        out_shape=out_shape,
        compiler_params=_cparams(("parallel", "parallel")),
        name="fft_final",
    )(*args)


def hyena_mixer(hz, bsz, seq, short_w, short_b, filt, filt_scale, bias):
    c = W_HYENA
    n2 = FFT_N2
    n = 2 * seq
    n1 = n // n2
    r = n1 // 2
    nk = _num_k1(n1)
    fwd_half, inv_half, sign_half = _dft_tables(n1, r)
    fwd_full = _dft_tables(n1, n1)[0]
    g, h = _mid_tables(n1)
    ka = fft_stage1(filt.reshape(1, n1, n2 * 2 * c), fwd_full)
    kf_re, kf_im = fft_mid_fwd(ka.reshape(1, 2, nk, n2, 2 * c), g, filt_scale)
    v, x1, x2 = hyena_short_conv(hz, short_w, short_b, seq)
    as2 = lambda t: t.reshape(bsz, r, n2 * c)
    z2 = as2(v)
    tc_first = min(max((1 << 20) // r, 2048), 16384)
    tc_last = min(max((1 << 19) // r, 2048), 8192)
    a = fft_stage1(z2, fwd_half, tc=tc_first)
    for order, gate in enumerate((x1, x2)):
        d = fft_mid(a.reshape(bsz, 2, nk, n2, c), g, h, kf_re, kf_im, order)
        bias_cols = jnp.tile(bias[order].astype(F32), n2).reshape(1, n2 * c)
        res = fft_final(d.reshape(bsz, 2 * nk, n2 * c), inv_half, sign_half, z2, as2(gate), bias_cols, 1.0 / n,
                        fwd=fwd_half if order == 0 else None, tc=tc_last)
        if order == 0:
            z2, a = res
        else:
            (z2,) = res
    return z2.reshape(bsz * seq, c)


def _trunk_layer0(x, c, wts):
    bsz, seq, _ = x.shape
    n = bsz * seq
    mod = adaln_mod(c, wts['ada_w'], wts['ada_b'])
    xf = x.reshape(n, D_MODEL)
    q, k, v, u = in_proj0(xf, mod[0], wts['norm_mix_g'][0], wts['ab_w_in'], wts['na_q_g'], wts['na_k_g'], seq)
    att = neighbourhood_attention(q, k, v, wts['att_bias'], bsz, seq)
    cnv = conv_module(u, wts['cv_dw_w'], wts['cv_dw_b'], wts['cv_ln_g'], wts['cv_ln_b'], seq)
    xf, h, gate, gate_t = out_proj_router(att, cnv, xf, mod[0], wts['ab_w_out'], wts['norm_ffn_g'][0],
                                          wts['router_w'], wts['router_b'], seq)
    xf = moe_ffn(h, gate, gate_t, xf, mod[0], wts['moe_wg'][0], wts['moe_wu'][0], wts['moe_wd'][0], seq)
    ut, hz = in_proj1(xf, mod[1], wts['norm_mix_g'][1], wts['cd_w_u_t'], wts['cd_w_z'], seq)
    return xf, mod, ut, hz


def _trunk_layer1(xf, mod, yt, hz, bsz, seq, wts):
    ssm = s5_post(yt, wts['s5_glu_w'])
    filt, filt_scale = hyena_filters(seq, *wts['hy_mlp'])
    hy = hyena_mixer(hz, bsz, seq, wts['hy_short_w'], wts['hy_short_b'], filt, filt_scale, wts['hy_bias'])
    xf, h, gate, gate_t = out_proj_router(ssm, hy, xf, mod[1], wts['cd_w_out'], wts['norm_ffn_g'][1],
                                          wts['router_w'], wts['router_b'], seq)
    xf = moe_ffn(h, gate, gate_t, xf, mod[1], wts['moe_wg'][1], wts['moe_wu'][1], wts['moe_wd'][1], seq)
    return xf.reshape(bsz, seq, D_MODEL)


def kernel(x_prompt, x_sample, c_prompt, c_sample, ada_w, ada_b, norm_mix_g, norm_ffn_g, router_w, router_b, moe_w_gate, moe_w_up, moe_w_down, ab_w_in, ab_w_out, na_q_g, na_k_g, na_rpb, cv_dw_w, cv_dw_b, cv_ln_g, cv_ln_b, cd_w_in, cd_w_out, s5_lam_re, s5_lam_im, s5_log_dt, s5_b_re, s5_b_im, s5_c_re, s5_c_im, s5_d, s5_glu_w, hy_short_w, hy_short_b, hy_w1, hy_b1, hy_w2, hy_b2, hy_w3, hy_freq, hy_decay, hy_bias):
    s5_tabs = s5_tables(s5_lam_re[0], s5_lam_im[0], s5_log_dt[0], s5_b_re[0], s5_b_im[0],
                        s5_c_re[0], s5_c_im[0], s5_d[0])
    wts = dict(
        ada_w=ada_w, ada_b=ada_b, norm_mix_g=norm_mix_g.astype(F32), norm_ffn_g=norm_ffn_g.astype(F32),
        router_w=router_w, router_b=router_b,
        moe_wg=jnp.swapaxes(moe_w_gate, -1, -2).astype(BF16), moe_wu=jnp.swapaxes(moe_w_up, -1, -2).astype(BF16),
        moe_wd=jnp.swapaxes(moe_w_down, -1, -2).astype(BF16),
        ab_w_in=ab_w_in[0].astype(BF16), ab_w_out=ab_w_out[0].astype(BF16),
        na_q_g=na_q_g[0], na_k_g=na_k_g[0], att_bias=_att_bias_table(na_rpb[0]),
        cv_dw_w=cv_dw_w[0], cv_dw_b=cv_dw_b[0], cv_ln_g=cv_ln_g[0], cv_ln_b=cv_ln_b[0],
        cd_w_u_t=cd_w_in[0][:, :W_SSM].T.astype(BF16), cd_w_z=cd_w_in[0][:, W_SSM:].astype(BF16),
        cd_w_out=cd_w_out[0].astype(BF16),
        s5_glu_w=s5_glu_w[0].astype(BF16),
        hy_short_w=hy_short_w[0], hy_short_b=hy_short_b[0],
        hy_mlp=(hy_w1[0], hy_b1[0], hy_w2[0], hy_b2[0], hy_w3[0], hy_freq[0], hy_decay[0]),
        hy_bias=hy_bias[0],
    )
    xs = (x_prompt, x_sample)
    mids = [_trunk_layer0(x, c, wts) for x, c in zip(xs, (c_prompt, c_sample))]
    yts = s5_scan([m[2] for m in mids], [x.shape[1] for x in xs], *s5_tabs)
    return tuple(_trunk_layer1(m[0], m[1], yt, m[3], x.shape[0], x.shape[1], wts)
                 for m, yt, x in zip(mids, yts, xs))
```

```python
import functools
import math

import numpy as np
import jax
import jax.numpy as jnp
from jax import lax
from jax.experimental import pallas as pl
from jax.experimental.pallas import tpu as pltpu

F32 = jnp.float32
BF16 = jnp.bfloat16
HIGHEST = lax.Precision.HIGHEST

D_MODEL = 1024
DEPTH = 2
GRID_W = 64
W_ATT = 512
W_CONV = 512
W_SSM = 512
W_HYENA = 512
HEAD_DIM = 64
N_HEADS = 8
WIN_R = 8
WIN_C = 16
CONV_W = 31
S5_GROUP = 16
S5_GROUPS = 32
S5_STATE = 64
HY_BANDS = 8
N_EXPERTS = 16
N_GROUPS = 4
D_FF = 512
EPS = 1e-6
NEG_INF = -1e30

VMEM_LIMIT_BYTES = 56 * 1024 * 1024
LANES = 128
SUBLANES = 8

ATT_QROWS = 4
ATT_KROWS = 12
S5_CHUNK = LANES
FFT_N2 = 128


def _cparams(sem):
    return pltpu.CompilerParams(dimension_semantics=sem, vmem_limit_bytes=VMEM_LIMIT_BYTES)


def _mod_kernel(c_ref, w_ref, b_ref, o_ref):
    c = c_ref[...]
    s = c * jax.nn.sigmoid(c)
    o_ref[0] = jnp.dot(s, w_ref[0], precision=HIGHEST, preferred_element_type=F32) + b_ref[0]


def adaln_mod(c, ada_w, ada_b):
    bsz = c.shape[0]
    tn = D_MODEL
    out = pl.pallas_call(
        _mod_kernel,
        grid=(DEPTH, 6 * D_MODEL // tn),
        in_specs=[
            pl.BlockSpec((bsz, D_MODEL), lambda i, j: (0, 0)),
            pl.BlockSpec((1, D_MODEL, tn), lambda i, j: (i, 0, j)),
            pl.BlockSpec((1, 1, tn), lambda i, j: (i, 0, j)),
        ],
        out_specs=pl.BlockSpec((1, bsz, tn), lambda i, j: (i, 0, j)),
        out_shape=jax.ShapeDtypeStruct((DEPTH, bsz, 6 * D_MODEL), F32),
        compiler_params=_cparams(("arbitrary", "arbitrary")),
        name="adaln_mod",
    )(c, ada_w, ada_b.reshape(DEPTH, 1, 6 * D_MODEL))
    return out.reshape(DEPTH, bsz, 6, D_MODEL)


def _dot3(a, b):
    a_hi, b_hi = a.astype(BF16), b.astype(BF16)
    a_lo = (a - a_hi.astype(F32)).astype(BF16)
    b_lo = (b - b_hi.astype(F32)).astype(BF16)
    return (jnp.dot(a_hi, b_hi, preferred_element_type=F32) + jnp.dot(a_lo, b_hi, preferred_element_type=F32)
            + jnp.dot(a_hi, b_lo, preferred_element_type=F32))


def _norm_mod(x, g, shift, scale):
    ms = jnp.mean(x * x, axis=-1, keepdims=True)
    return x * lax.rsqrt(ms + EPS) * g * (1.0 + scale) + shift


def _in0_kernel(x_ref, mod_ref, g_ref, w_ref, hm_ref, qg_ref, kg_ref, q_o, k_o, v_o, u_o):
    m = mod_ref[0]
    h = _norm_mod(x_ref[...], g_ref[...], m[0:1], m[1:2])
    p = jnp.dot(h.astype(BF16), w_ref[...], preferred_element_type=F32)
    q = p[:, 0:W_ATT]
    k = p[:, W_ATT:2 * W_ATT]
    qms = jnp.dot((q * q).astype(BF16), hm_ref[...], preferred_element_type=F32)
    kms = jnp.dot((k * k).astype(BF16), hm_ref[...], preferred_element_type=F32)
    q_o[...] = (q * lax.rsqrt(qms + EPS) * qg_ref[...]).astype(BF16)
    k_o[...] = (k * lax.rsqrt(kms + EPS) * kg_ref[...]).astype(BF16)
    v_o[...] = p[:, 2 * W_ATT:3 * W_ATT].astype(BF16)
    val = p[:, 3 * W_ATT:3 * W_ATT + W_CONV]
    gate = p[:, 3 * W_ATT + W_CONV:]
    u_o[...] = (val * jax.nn.sigmoid(gate)).astype(BF16)


def in_proj0(x, mod, g, w_in_bf, q_gain, k_gain, seq, tm=512):
    n = x.shape[0]
    head_mean = jnp.asarray(np.kron(np.eye(N_HEADS), np.full((HEAD_DIM, HEAD_DIM), 1.0 / HEAD_DIM)), BF16)
    qg = (jnp.tile(q_gain.astype(F32), N_HEADS) * (HEAD_DIM ** -0.5)).reshape(1, W_ATT)
    kg = jnp.tile(k_gain.astype(F32), N_HEADS).reshape(1, W_ATT)
    tok = lambda i: (i, 0)
    fixed = lambda i: (0, 0)
    osd = jax.ShapeDtypeStruct((n, W_ATT), BF16)
    return pl.pallas_call(
        _in0_kernel,
        grid=(n // tm,),
        in_specs=[
            pl.BlockSpec((tm, D_MODEL), tok),
            pl.BlockSpec((1, 6, D_MODEL), lambda i: ((i * tm) // seq, 0, 0)),
            pl.BlockSpec((1, D_MODEL), fixed),
            pl.BlockSpec(w_in_bf.shape, fixed),
            pl.BlockSpec((W_ATT, W_ATT), fixed),
            pl.BlockSpec((1, W_ATT), fixed),
            pl.BlockSpec((1, W_ATT), fixed),
        ],
        out_specs=[pl.BlockSpec((tm, W_ATT), tok)] * 4,
        out_shape=[osd] * 4,
        compiler_params=_cparams(("parallel",)),
        name="in_proj0",
    )(x, mod, g.reshape(1, D_MODEL), w_in_bf, head_mean, qg, kg)


def _att_bias_table(rpb):
    a = np.arange(ATT_QROWS)[:, None, None, None]
    c = np.arange(GRID_W)[None, :, None, None]
    e = np.arange(ATT_KROWS)[None, None, :, None]
    kc = np.arange(GRID_W)[None, None, None, :]
    c0 = np.clip(c - WIN_C // 2, 0, GRID_W - WIN_C)
    col_ok = (kc >= c0) & (kc < c0 + WIN_C)
    rpb = rpb.astype(F32)
    per = 2 * GRID_W
    vrow = jnp.concatenate([rpb[..., WIN_C - 1:],
                            jnp.zeros(rpb.shape[:2] + (per - (2 * WIN_C - 1),), F32),
                            rpb[..., :WIN_C - 1]], axis=-1)
    tcol = jnp.tile(vrow, (1, 1, GRID_W))[..., :GRID_W * (per - 1)]
    tcol = tcol.reshape(rpb.shape[:2] + (GRID_W, per - 1))[..., :GRID_W]
    tables = []
    for case in range(3):
        if case == 0:
            dr = e - a
            row_ok = (e >= 0) & (e < WIN_R)
        elif case == 1:
            dr = e - a - WIN_R // 2
            row_ok = (dr >= -(WIN_R // 2)) & (dr < WIN_R // 2)
        else:
            dr = e - a - (ATT_KROWS - ATT_QROWS)
            row_ok = (e >= ATT_KROWS - WIN_R) & (e < ATT_KROWS)
        ok = np.broadcast_to(row_ok & col_ok, (ATT_QROWS, GRID_W, ATT_KROWS, GRID_W))
        dri = np.clip(dr + WIN_R - 1, 0, 2 * WIN_R - 2)[:, 0, :, 0]
        nq, nk = ATT_QROWS * GRID_W, ATT_KROWS * GRID_W
        t = jnp.concatenate([jnp.concatenate([tcol[:, int(dri[qa, ke])] for ke in range(ATT_KROWS)], axis=-1)
                             for qa in range(ATT_QROWS)], axis=-2)
        tables.append(jnp.where(jnp.asarray(ok.reshape(nq, nk)), t, NEG_INF))
    return jnp.stack(tables).astype(BF16)


def _att_kernel(q_ref, k0, k1, k2, v0, v1, v2, bias_ref, o_ref):
    kt = [k0, k1, k2]
    vt = [v0, v1, v2]
    nkb = len(kt)
    kw = k0.shape[0]
    tq = q_ref.shape[0]
    first = lax.broadcasted_iota(jnp.int32, (tq, LANES), 1) < HEAD_DIM
    for hp in range(N_HEADS // 2):
        ps = slice(hp * LANES, (hp + 1) * LANES)
        qp = q_ref[:, ps]
        res = []
        for sub in range(2):
            h = 2 * hp + sub
            qm = jnp.where(first if sub == 0 else jnp.logical_not(first), qp, jnp.zeros_like(qp))
            s = [lax.dot_general(qm, kt[j][:, ps], (((1,), (1,)), ((), ())), preferred_element_type=F32)
                 + bias_ref[0, h, :, j * kw:(j + 1) * kw].astype(F32) for j in range(nkb)]
            m = s[0].max(axis=-1, keepdims=True)
            for j in range(1, nkb):
                m = jnp.maximum(m, s[j].max(axis=-1, keepdims=True))
            p = [jnp.exp(sj - m) for sj in s]
            l = p[0].sum(axis=-1, keepdims=True)
            for j in range(1, nkb):
                l = l + p[j].sum(axis=-1, keepdims=True)
            o = jnp.dot(p[0].astype(BF16), vt[0][:, ps], preferred_element_type=F32)
            for j in range(1, nkb):
                o = o + jnp.dot(p[j].astype(BF16), vt[j][:, ps], preferred_element_type=F32)
            res.append(o / l)
        o_ref[:, ps] = jnp.where(first, res[0], res[1]).astype(BF16)


def neighbourhood_attention(q, k, v, bias, bsz, seq):
    rows = seq // GRID_W
    nqb = rows // ATT_QROWS
    tq = ATT_QROWS * GRID_W
    nkb = ATT_KROWS // ATT_QROWS
    assert rows % ATT_QROWS == 0 and nqb >= nkb

    def kmap(j):
        def f(i):
            b, r = i // nqb, i % nqb
            return (b * nqb + jnp.clip(r - 1, 0, nqb - nkb) + j, 0)
        return f

    def bias_map(i):
        r = i % nqb
        return (jnp.where(r == 0, 0, jnp.where(r == nqb - 1, 2, 1)), 0, 0, 0)

    kv_specs = [pl.BlockSpec((tq, W_ATT), kmap(j)) for j in range(nkb)]
    return pl.pallas_call(
        _att_kernel,
        grid=(bsz * nqb,),
        in_specs=[pl.BlockSpec((tq, W_ATT), lambda i: (i, 0))] + kv_specs + kv_specs
        + [pl.BlockSpec((1,) + bias.shape[1:], bias_map)],
        out_specs=pl.BlockSpec((tq, W_ATT), lambda i: (i, 0)),
        out_shape=jax.ShapeDtypeStruct(q.shape, BF16),
        compiler_params=_cparams(("parallel",)),
        name="neighbourhood_attention",
    )(q, k, k, k, v, v, v, bias)


CONV_HALO = 16


def _conv_kernel(prev_ref, cur_ref, next_ref, w_ref, b_ref, g_ref, be_ref, o_ref, win_ref, sh_ref, *, tiles_per_seq):
    i = pl.program_id(0)
    t = i % tiles_per_seq
    tt = cur_ref.shape[0]
    half = CONV_W // 2
    prev = prev_ref[...].astype(F32)
    nxt = next_ref[...].astype(F32)
    win_ref[0:CONV_HALO, :] = jnp.where(t == 0, 0.0, prev)
    win_ref[CONV_HALO:CONV_HALO + tt, :] = cur_ref[...].astype(F32)
    win_ref[CONV_HALO + tt:, :] = jnp.where(t == tiles_per_seq - 1, 0.0, nxt)
    span = tt + 2 * CONV_HALO - SUBLANES
    for ph in range(1, SUBLANES):
        sh_ref[ph, 0:span, :] = win_ref[ph:ph + span, :]
    w = w_ref[...]
    acc = jnp.zeros((tt, W_CONV), F32) + b_ref[...]
    for kk in range(CONV_W):
        off = CONV_HALO - half + kk
        ph, base = off % SUBLANES, (off // SUBLANES) * SUBLANES
        tap = win_ref[base:base + tt, :] if ph == 0 else sh_ref[ph, base:base + tt, :]
        acc = acc + tap * w[kk:kk + 1, :]
    mu = jnp.mean(acc, axis=-1, keepdims=True)
    d = acc - mu
    var = jnp.mean(d * d, axis=-1, keepdims=True)
    y = d * lax.rsqrt(var + EPS) * g_ref[...] + be_ref[...]
    o_ref[...] = (y * jax.nn.sigmoid(y)).astype(BF16)


def conv_module(u, dw_w, dw_b, ln_g, ln_b, seq, tt=512):
    n = u.shape[0]
    tps = seq // tt
    hb = tt // CONV_HALO
    nhb = n // CONV_HALO
    row = lambda a: a.astype(F32).reshape(1, W_CONV)
    fixed = lambda i: (0, 0)
    return pl.pallas_call(
        functools.partial(_conv_kernel, tiles_per_seq=tps),
        grid=(n // tt,),
        in_specs=[
            pl.BlockSpec((CONV_HALO, W_CONV), lambda i: (jnp.maximum(i * hb - 1, 0), 0)),
            pl.BlockSpec((tt, W_CONV), lambda i: (i, 0)),
            pl.BlockSpec((CONV_HALO, W_CONV), lambda i: (jnp.minimum((i + 1) * hb, nhb - 1), 0)),
            pl.BlockSpec((CONV_W, W_CONV), fixed),
            pl.BlockSpec((1, W_CONV), fixed),
            pl.BlockSpec((1, W_CONV), fixed),
            pl.BlockSpec((1, W_CONV), fixed),
        ],
        out_specs=pl.BlockSpec((tt, W_CONV), lambda i: (i, 0)),
        out_shape=jax.ShapeDtypeStruct((n, W_CONV), BF16),
        scratch_shapes=[pltpu.VMEM((tt + 2 * CONV_HALO, W_CONV), F32),
                        pltpu.VMEM((SUBLANES, tt + 2 * CONV_HALO, W_CONV), F32)],
        compiler_params=_cparams(("parallel",)),
        name="conv_module",
    )(u, u, u, dw_w.astype(F32), row(dw_b), row(ln_g), row(ln_b))


def _top2_sum(a, b, c, d):
    hi1, lo1 = jnp.maximum(a, b), jnp.minimum(a, b)
    hi2, lo2 = jnp.maximum(c, d), jnp.minimum(c, d)
    return jnp.maximum(hi1, hi2) + jnp.maximum(jnp.minimum(hi1, hi2), jnp.maximum(lo1, lo2))


def _router_gates(scores_t, bias_t):
    per = N_EXPERTS // N_GROUPS
    rows = [scores_t[e:e + 1, :] + bias_t[e:e + 1, :] for e in range(N_EXPERTS)]
    gscore = [_top2_sum(*rows[g * per:(g + 1) * per]) for g in range(N_GROUPS)]
    best = gscore[0]
    best_idx = jnp.zeros_like(best, dtype=jnp.int32)
    for g in range(1, N_GROUPS):
        better = gscore[g] > best
        best = jnp.where(better, gscore[g], best)
        best_idx = jnp.where(better, g, best_idx)
    out_row = lax.broadcasted_iota(jnp.int32, (LANES, scores_t.shape[1]), 0)
    gates = jnp.zeros((LANES, scores_t.shape[1]), F32)
    total = jnp.zeros_like(best)
    for e in range(N_EXPERTS):
        g = e // per
        rank = jnp.zeros_like(best_idx)
        for e2 in range(g * per, (g + 1) * per):
            if e2 == e:
                continue
            if e2 < e:
                ahead = rows[e2] >= rows[e]
            else:
                ahead = rows[e2] > rows[e]
            rank = rank + jnp.where(ahead, 1, 0)
        chosen = jnp.where(best_idx == g, rank, 2) < 2
        gated = jnp.where(chosen, scores_t[e:e + 1, :], 0.0)
        total = total + gated
        gates = jnp.where(out_row == e, gated, gates)
    return gates / total


def _out_kernel(a_ref, b_ref, x_ref, mod_ref, wa_ref, wb_ref, g_ref, rwc_ref, rb_ref, x_o, h_o, gate_o,
                gate_t_o):
    m = mod_ref[0]
    mix = (jnp.dot(a_ref[...], wa_ref[...], preferred_element_type=F32)
           + jnp.dot(b_ref[...], wb_ref[...], preferred_element_type=F32))
    x = x_ref[...] + m[2:3] * mix
    x_o[...] = x
    h = _norm_mod(x, g_ref[...], m[3:4], m[4:5])
    h_hi = h.astype(BF16)
    h_o[...] = h_hi
    h_lo = (h - h_hi.astype(F32)).astype(BF16)
    both = jnp.dot(h_hi, rwc_ref[...], preferred_element_type=F32)
    logits = both[:, :LANES] + both[:, LANES:] + jnp.dot(h_lo, rwc_ref[:, :LANES], preferred_element_type=F32)
    scores_t = jax.nn.sigmoid(logits).T
    gates_t = _router_gates(scores_t, rb_ref[...])
    gate_t_o[...] = gates_t
    gate_o[...] = gates_t.T


def out_proj_router(a, b, x, mod, w_out_bf, g_ffn, router_w, router_b, seq, tm=512):
    n = x.shape[0]
    half = a.shape[1]
    tok = lambda i: (i, 0)
    fixed = lambda i: (0, 0)
    rw = jnp.pad(router_w.astype(F32), ((0, 0), (0, LANES - N_EXPERTS)))
    rw_hi = rw.astype(BF16)
    rw_cat = jnp.concatenate([rw_hi, (rw - rw_hi.astype(F32)).astype(BF16)], axis=1)
    return pl.pallas_call(
        _out_kernel,
        grid=(n // tm,),
        in_specs=[
            pl.BlockSpec((tm, half), tok),
            pl.BlockSpec((tm, half), tok),
            pl.BlockSpec((tm, D_MODEL), tok),
            pl.BlockSpec((1, 6, D_MODEL), lambda i: ((i * tm) // seq, 0, 0)),
            pl.BlockSpec((half, D_MODEL), lambda i: (0, 0)),
            pl.BlockSpec((half, D_MODEL), lambda i: (1, 0)),
            pl.BlockSpec((1, D_MODEL), fixed),
            pl.BlockSpec((D_MODEL, 2 * LANES), fixed),
            pl.BlockSpec((N_EXPERTS, 1), fixed),
        ],
        out_specs=[pl.BlockSpec((tm, D_MODEL), tok), pl.BlockSpec((tm, D_MODEL), tok),
                   pl.BlockSpec((tm, LANES), tok), pl.BlockSpec((LANES, tm), lambda i: (0, i))],
        out_shape=[jax.ShapeDtypeStruct((n, D_MODEL), F32), jax.ShapeDtypeStruct((n, D_MODEL), BF16),
                   jax.ShapeDtypeStruct((n, LANES), F32), jax.ShapeDtypeStruct((LANES, n), F32)],
        compiler_params=_cparams(("parallel",)),
        name="out_proj_router",
    )(a, b, x, mod, w_out_bf, w_out_bf, g_ffn.reshape(1, D_MODEL), rw_cat,
      router_b.astype(F32).reshape(N_EXPERTS, 1))


MOE_COLS = 256
EXPERTS_PER_GROUP = N_EXPERTS // N_GROUPS


def _moe_kernel(h_ref, gate_ref, gate_t_ref, x_ref, mod_ref, tri_ref, wg_ref, wu_ref, wd_ref, o_ref,
                rankc_ref, rankr_ref, ht_ref, xg_ref, gg_ref, yg_ref, acc_ref, cnt_ref):
    e = pl.program_id(1)
    g = e // EXPERTS_PER_GROUP
    j = e % EXPERTS_PER_GROUP
    T = h_ref.shape[0]
    R = MOE_COLS

    @pl.when(e == 0)
    def _():
        er = lax.broadcasted_iota(jnp.int32, (LANES, LANES), 0)
        ec = lax.broadcasted_iota(jnp.int32, (LANES, LANES), 1)
        sel_c = jnp.where((er < N_EXPERTS) & (er // EXPERTS_PER_GROUP == ec), 1.0, 0.0).astype(BF16)
        sel_r = jnp.where((ec < N_EXPERTS) & (ec // EXPERTS_PER_GROUP == er), 1.0, 0.0).astype(BF16)
        chosen_c = jnp.where(gate_ref[...] > 0.0, 1.0, 0.0).astype(BF16)
        memb_c = jnp.dot(chosen_c, sel_c, preferred_element_type=F32) > 0.0
        rank_c = jnp.dot(tri_ref[...], jnp.where(memb_c, 1.0, 0.0).astype(BF16), preferred_element_type=F32)
        rankc_ref[...] = jnp.where(memb_c, rank_c, -1.0)
        chosen_r = jnp.where(gate_t_ref[...] > 0.0, 1.0, 0.0).astype(BF16)
        memb_r = jnp.dot(sel_r, chosen_r, preferred_element_type=F32) > 0.0
        ones_r = jnp.where(memb_r, 1.0, 0.0)
        rank_r = lax.dot_general(ones_r.astype(BF16), tri_ref[...], (((1,), (1,)), ((), ())),
                                 preferred_element_type=F32)
        rankr_ref[...] = jnp.where(memb_r, rank_r, -1.0)
        for gi in range(N_GROUPS):
            cnt_ref[gi] = jnp.sum(ones_r[gi:gi + 1, :]).astype(jnp.int32)
        ht_ref[...] = h_ref[...].astype(F32).T.astype(BF16)
        acc_ref[...] = jnp.zeros_like(acc_ref)

    nch = (cnt_ref[g] + (R - 1)) // R

    @pl.when(j == 0)
    def _():
        lane_t = lax.broadcasted_iota(jnp.int32, (T, LANES), 1)
        rc = jnp.sum(jnp.where(lane_t == g, rankc_ref[...], 0.0), axis=1, keepdims=True)
        col = lax.broadcasted_iota(jnp.int32, (T, R), 1).astype(F32)
        gt = gate_t_ref[...]
        g_hi = gt.astype(BF16)
        g_lo = (gt - g_hi.astype(F32)).astype(BF16)

        def gather(c, carry):
            pt = jnp.where(rc - (c * R).astype(F32) == col, 1.0, 0.0).astype(BF16)
            xg_ref[c] = jnp.dot(ht_ref[...], pt, preferred_element_type=F32).astype(BF16)
            gg_ref[c] = (jnp.dot(g_hi, pt, preferred_element_type=F32)
                         + jnp.dot(g_lo, pt, preferred_element_type=F32))
            yg_ref[c] = jnp.zeros((D_MODEL, R), F32)
            return carry

        lax.fori_loop(0, nch, gather, 0)

    def ffn(c, carry):
        xc = xg_ref[c]
        a = jnp.dot(wg_ref[0], xc, preferred_element_type=F32)
        u = jnp.dot(wu_ref[0], xc, preferred_element_type=F32)
        hid = (a * jax.nn.sigmoid(a)) * u * gg_ref[c, pl.ds(e, 1), :]
        yg_ref[c] += jnp.dot(wd_ref[0], hid.astype(BF16), preferred_element_type=F32)
        return carry

    lax.fori_loop(0, nch, ffn, 0)

    @pl.when(j == EXPERTS_PER_GROUP - 1)
    def _():
        rr = rankr_ref[pl.ds(g, 1), :]
        row = lax.broadcasted_iota(jnp.int32, (R, T), 0).astype(F32)

        def scatter(c, carry):
            p = jnp.where(rr - (c * R).astype(F32) == row, 1.0, 0.0).astype(BF16)
            acc_ref[...] += jnp.dot(yg_ref[c].astype(BF16), p, preferred_element_type=F32)
            return carry

        lax.fori_loop(0, nch, scatter, 0)

    @pl.when(e == N_EXPERTS - 1)
    def _():
        o_ref[...] = x_ref[...] + mod_ref[0][5:6] * acc_ref[...].T


def moe_ffn(h, gate, gate_t, x, mod, wgt_bf, wut_bf, wdt_bf, seq, tm=1024):
    n = x.shape[0]
    tm = min(tm, seq)
    nch = tm // MOE_COLS
    tok = lambda i, e: (i, 0)
    tri = jnp.asarray(np.tril(np.ones((tm, tm), np.float32), -1), BF16)
    return pl.pallas_call(
        _moe_kernel,
        grid=(n // tm, N_EXPERTS),
        in_specs=[
            pl.BlockSpec((tm, D_MODEL), tok),
            pl.BlockSpec((tm, LANES), tok),
            pl.BlockSpec((LANES, tm), lambda i, e: (0, i)),
            pl.BlockSpec((tm, D_MODEL), tok),
            pl.BlockSpec((1, 6, D_MODEL), lambda i, e: ((i * tm) // seq, 0, 0)),
            pl.BlockSpec((tm, tm), lambda i, e: (0, 0)),
            pl.BlockSpec((1, D_FF, D_MODEL), lambda i, e: (e, 0, 0)),
            pl.BlockSpec((1, D_FF, D_MODEL), lambda i, e: (e, 0, 0)),
            pl.BlockSpec((1, D_MODEL, D_FF), lambda i, e: (e, 0, 0)),
        ],
        out_specs=pl.BlockSpec((tm, D_MODEL), tok),
        out_shape=jax.ShapeDtypeStruct((n, D_MODEL), F32),
        scratch_shapes=[pltpu.VMEM((tm, LANES), F32), pltpu.VMEM((LANES, tm), F32),
                        pltpu.VMEM((D_MODEL, tm), BF16), pltpu.VMEM((nch, D_MODEL, MOE_COLS), BF16),
                        pltpu.VMEM((nch, LANES, MOE_COLS), F32), pltpu.VMEM((nch, D_MODEL, MOE_COLS), F32),
                        pltpu.VMEM((D_MODEL, tm), F32), pltpu.SMEM((N_GROUPS,), jnp.int32)],
        compiler_params=_cparams(("parallel", "arbitrary")),
        name="moe_ffn",
    )(h, gate, gate_t, x, mod, tri, wgt_bf, wut_bf, wdt_bf)


def _in1_kernel(x_ref, mod_ref, g_ref, wut_ref, wz_ref, ut_o, hz_o):
    m = mod_ref[0]
    h = _norm_mod(x_ref[...], g_ref[...], m[0:1], m[1:2]).astype(BF16)
    ut_o[...] = lax.dot_general(wut_ref[...], h, (((1,), (1,)), ((), ())),
                                preferred_element_type=F32).astype(BF16)
    hz_o[...] = jnp.dot(h, wz_ref[...], preferred_element_type=F32).astype(BF16)


def in_proj1(x, mod, g, w_u_t_bf, w_z_bf, seq, tm=512):
    n = x.shape[0]
    fixed = lambda i: (0, 0)
    return pl.pallas_call(
        _in1_kernel,
        grid=(n // tm,),
        in_specs=[
            pl.BlockSpec((tm, D_MODEL), lambda i: (i, 0)),
            pl.BlockSpec((1, 6, D_MODEL), lambda i: ((i * tm) // seq, 0, 0)),
            pl.BlockSpec((1, D_MODEL), fixed),
            pl.BlockSpec(w_u_t_bf.shape, fixed),
            pl.BlockSpec(w_z_bf.shape, fixed),
        ],
        out_specs=[pl.BlockSpec((W_SSM, tm), lambda i: (0, i)),
                   pl.BlockSpec((tm, 3 * W_HYENA), lambda i: (i, 0))],
        out_shape=[jax.ShapeDtypeStruct((W_SSM, n), BF16), jax.ShapeDtypeStruct((n, 3 * W_HYENA), BF16)],
        compiler_params=_cparams(("parallel",)),
        name="in_proj1",
    )(x, mod, g.reshape(1, D_MODEL), w_u_t_bf, w_z_bf)


def s5_tables(lam_re, lam_im, log_dt, b_re, b_im, c_re, c_im, d_skip):
    T = S5_CHUNK
    f32 = F32
    lags = jnp.arange(T, dtype=f32)

    def disc(d):
        lr, li = lam_re[d].astype(f32), lam_im[d].astype(f32)
        dt = jnp.exp(log_dt[d].astype(f32))[:, None]
        mag = jnp.exp(lr * dt)
        ab_re, ab_im = mag * jnp.cos(li * dt), mag * jnp.sin(li * dt)
        den = lr * lr + li * li
        f_re = ((ab_re - 1.0) * lr + ab_im * li) / den
        f_im = (ab_im * lr - (ab_re - 1.0) * li) / den
        br, bi = b_re[d].astype(f32), b_im[d].astype(f32)
        bb_re = f_re[..., None] * br - f_im[..., None] * bi
        bb_im = f_re[..., None] * bi + f_im[..., None] * br

        def power(p):
            ang = li * dt
            mg = jnp.exp(p[:, None, None] * (lr * dt)[None])
            return mg * jnp.cos(p[:, None, None] * ang[None]), mg * jnp.sin(p[:, None, None] * ang[None])
        return bb_re, bb_im, c_re[d].astype(f32), c_im[d].astype(f32), power

    hp = dict(precision=HIGHEST)
    tabs = []
    for d in range(2):
        bb_re, bb_im, cr, ci, power = disc(d)
        pr, pi = power(lags)
        cb_rr = jnp.einsum('gon,lgn,gni->lgoi', cr, pr, bb_re, **hp)
        cb_ii = jnp.einsum('gon,lgn,gni->lgoi', cr, pi, bb_im, **hp)
        cb_ri = jnp.einsum('gon,lgn,gni->lgoi', ci, pr, bb_im, **hp)
        cb_ir = jnp.einsum('gon,lgn,gni->lgoi', ci, pi, bb_re, **hp)
        kern = cb_rr - cb_ii - cb_ri - cb_ir
        qp = (T - 1.0 - lags) if d == 0 else lags
        qr, qi = power(qp)
        wp_re = jnp.einsum('tgn,gni->gitn', qr, bb_re) - jnp.einsum('tgn,gni->gitn', qi, bb_im)
        wp_im = jnp.einsum('tgn,gni->gitn', qr, bb_im) + jnp.einsum('tgn,gni->gitn', qi, bb_re)
        rp = (lags + 1.0) if d == 0 else (T - lags)
        rr, ri = power(rp)
        m_re = jnp.einsum('gon,tgn->gnot', cr, rr) - jnp.einsum('gon,tgn->gnot', ci, ri)
        m_im = jnp.einsum('gon,tgn->gnot', cr, ri) + jnp.einsum('gon,tgn->gnot', ci, rr)
        levels = 2.0 ** jnp.arange(16, dtype=f32) * T
        ar, ai = power(levels)
        tabs.append((kern, wp_re, wp_im, m_re, -m_im, ar, ai))

    kf, kb = tabs[0][0], tabs[1][0]
    skip = jnp.eye(S5_GROUP, dtype=f32)[None] * d_skip.astype(f32).reshape(S5_GROUPS, S5_GROUP, 1)
    k0 = kf[0] + kb[0] + skip
    kk = jnp.concatenate([k0[None], kf[1:], jnp.zeros_like(k0)[None], kb[:0:-1]], axis=0)
    kk = jnp.transpose(kk, (1, 3, 2, 0))
    g_, i_, o_ = kk.shape[:3]
    kk = kk.reshape(g_, i_ * o_, 2 * T)
    w_state = jnp.concatenate([tabs[0][1], tabs[0][2], tabs[1][1], tabs[1][2]], axis=-1)
    wp = w_state.reshape(g_, i_ * T, 4 * S5_STATE).astype(BF16)
    wc = jnp.concatenate([tabs[0][3], tabs[0][4], tabs[1][3], tabs[1][4]], axis=1)
    wc = wc.reshape(g_, 4 * S5_STATE, o_ * T).astype(BF16)
    mult = []
    for d in range(2):
        ar, ai = tabs[d][5], tabs[d][6]
        mult += [jnp.concatenate([ar, ar], axis=-1), jnp.concatenate([-ai, ai], axis=-1)]
    scan_mult = jnp.transpose(jnp.stack(mult, axis=2), (1, 0, 2, 3))
    return kk, wp, wc, scan_mult


def _s5_kernel(*refs, chunks_per_seq):
    nt = len(chunks_per_seq)
    u_refs, (kk_ref, wp_ref, wc_ref, mult_ref) = refs[:nt], refs[nt:nt + 4]
    o_refs, w_ref = refs[nt + 4:2 * nt + 4], refs[2 * nt + 4]
    T = S5_CHUNK

    def build(i, carry):
        for o in range(S5_GROUP):
            row = kk_ref[0, pl.ds(i * S5_GROUP + o, 1), :]
            toe = pltpu.roll(jnp.broadcast_to(row, (T, 2 * T)), 0, axis=1, stride=1, stride_axis=0)
            w_ref[pl.ds(pl.multiple_of(i * T, T), T), o * T:(o + 1) * T] = toe[:, :T].astype(BF16)
        return carry

    lax.fori_loop(0, S5_GROUP, build, 0)
    for u_ref, o_ref, cps in zip(u_refs, o_refs, chunks_per_seq):
        _s5_apply(u_ref, o_ref, w_ref, wp_ref, wc_ref, mult_ref, cps)


def _s5_apply(u_ref, o_ref, w_ref, wp_ref, wc_ref, mult_ref, chunks_per_seq):
    T = S5_CHUNK
    ns = 2 * S5_STATE
    x = jnp.concatenate([u_ref[i] for i in range(S5_GROUP)], axis=1)
    y = jnp.dot(x, w_ref[...], preferred_element_type=F32)
    pst = jnp.dot(x, wp_ref[0], preferred_element_type=F32)
    nc = y.shape[0]
    cidx = lax.broadcasted_iota(jnp.int32, (nc, ns), 0) % chunks_per_seq

    def cmul(s, mre, mim):
        return s * mre + pltpu.roll(s, S5_STATE, axis=1) * mim

    def scan(p, d):
        s = p
        k, step = 0, 1
        while step < chunks_per_seq:
            mre = mult_ref[0, k, 2 * d:2 * d + 1, :]
            mim = mult_ref[0, k, 2 * d + 1:2 * d + 2, :]
            if d == 0:
                sh = jnp.where(cidx >= step, pltpu.roll(s, step, axis=0), 0.0)
            else:
                sh = jnp.where(cidx < chunks_per_seq - step, pltpu.roll(s, nc - step, axis=0), 0.0)
            s = s + cmul(sh, mre, mim)
            k, step = k + 1, step * 2
        if d == 0:
            return jnp.where(cidx >= 1, pltpu.roll(s, 1, axis=0), 0.0)
        return jnp.where(cidx < chunks_per_seq - 1, pltpu.roll(s, nc - 1, axis=0), 0.0)

    sf = scan(pst[:, :ns], 0)
    sb = scan(pst[:, ns:], 1)
    carry = jnp.concatenate([sf, sb], axis=1).astype(BF16)
    ytot = y + jnp.dot(carry, wc_ref[0], preferred_element_type=F32)
    for o in range(S5_GROUP):
        o_ref[o] = ytot[:, o * T:(o + 1) * T]


def s5_scan(uts, seqs, kk, wp, wc, scan_mult):
    T = S5_CHUNK
    u3 = [ut.reshape(W_SSM, ut.shape[1] // T, T) for ut in uts]
    blk = [pl.BlockSpec((S5_GROUP,) + u.shape[1:], lambda g: (g, 0, 0)) for u in u3]
    grp = lambda a: pl.BlockSpec((1,) + a.shape[1:], lambda g: (g,) + (0,) * (a.ndim - 1))
    outs = pl.pallas_call(
        functools.partial(_s5_kernel, chunks_per_seq=tuple(s // T for s in seqs)),
        grid=(S5_GROUPS,),
        in_specs=blk + [grp(kk), grp(wp), grp(wc), grp(scan_mult)],
        out_specs=blk,
        out_shape=[jax.ShapeDtypeStruct(u.shape, F32) for u in u3],
        scratch_shapes=[pltpu.VMEM((S5_GROUP * T, S5_GROUP * T), BF16)],
        compiler_params=_cparams(("parallel",)),
        name="s5_scan",
    )(*u3, kk, wp, wc, scan_mult)
    return [o.reshape(W_SSM, -1) for o in outs]


def _s5_post_kernel(yt_ref, w_ref, o_ref):
    y = yt_ref[...].T
    y = 0.5 * y * (1.0 + jnp.tanh(math.sqrt(2.0 / math.pi) * (y + 0.044715 * (y * y * y))))
    z = jnp.dot(y.astype(BF16), w_ref[...], preferred_element_type=F32)
    o_ref[...] = (y * jax.nn.sigmoid(z)).astype(BF16)


def s5_post(yt, glu_w_bf, tm=2048):
    n = yt.shape[1]
    return pl.pallas_call(
        _s5_post_kernel,
        grid=(n // tm,),
        in_specs=[pl.BlockSpec((W_SSM, tm), lambda i: (0, i)),
                  pl.BlockSpec((W_SSM, W_SSM), lambda i: (0, 0))],
        out_specs=pl.BlockSpec((tm, W_SSM), lambda i: (i, 0)),
        out_shape=jax.ShapeDtypeStruct((n, W_SSM), BF16),
        compiler_params=_cparams(("parallel",)),
        name="s5_post",
    )(yt, glu_w_bf)


SHORT_HALO = 16


def _short_kernel(prev_ref, cur_ref, next_ref, w_ref, b_ref, v_o, x1_o, x2_o, *, tiles_per_seq):
    i = pl.program_id(0)
    t = i % tiles_per_seq
    cur = cur_ref[...].astype(F32)
    tt = cur.shape[0]
    before = jnp.where(t == 0, 0.0, prev_ref[SHORT_HALO - 1:SHORT_HALO, :].astype(F32))
    after = jnp.where(t == tiles_per_seq - 1, 0.0, next_ref[0:1, :].astype(F32))
    ridx = lax.broadcasted_iota(jnp.int32, cur.shape, 0)
    left = jnp.where(ridx == 0, before, pltpu.roll(cur, 1, axis=0))
    right = jnp.where(ridx == tt - 1, after, pltpu.roll(cur, tt - 1, axis=0))
    w = w_ref[...]
    p = left * w[0:1] + cur * w[1:2] + right * w[2:3] + b_ref[...]
    v_o[...] = p[:, 0:W_HYENA].astype(BF16)
    x1_o[...] = p[:, W_HYENA:2 * W_HYENA].astype(BF16)
    x2_o[...] = p[:, 2 * W_HYENA:].astype(BF16)


def hyena_short_conv(hz, short_w, short_b, seq, tt=512):
    n, c3 = hz.shape
    tps = seq // tt
    hb = tt // SHORT_HALO
    nhb = n // SHORT_HALO
    fixed = lambda i: (0, 0)
    osd = jax.ShapeDtypeStruct((n, W_HYENA), BF16)
    return pl.pallas_call(
        functools.partial(_short_kernel, tiles_per_seq=tps),
        grid=(n // tt,),
        in_specs=[
            pl.BlockSpec((SHORT_HALO, c3), lambda i: (jnp.maximum(i * hb - 1, 0), 0)),
            pl.BlockSpec((tt, c3), lambda i: (i, 0)),
            pl.BlockSpec((SHORT_HALO, c3), lambda i: (jnp.minimum((i + 1) * hb, nhb - 1), 0)),
            pl.BlockSpec((3, c3), fixed),
            pl.BlockSpec((1, c3), fixed),
        ],
        out_specs=[pl.BlockSpec((tt, W_HYENA), lambda i: (i, 0))] * 3,
        out_shape=[osd] * 3,
        compiler_params=_cparams(("parallel",)),
        name="hyena_short_conv",
    )(hz, hz, hz, short_w.astype(F32), short_b.astype(F32).reshape(1, c3))


def _filter_kernel(band_ref, w1_ref, b1_ref, w2_ref, b2_ref, w3_ref, freq_ref, decay_ref, k_o, sum_o, *, l):
    i = pl.program_id(0)
    tr = k_o.shape[0]
    m = i * tr + lax.broadcasted_iota(jnp.int32, (tr, LANES), 0)
    pos = jnp.where(m > l, 2 * l - m, m).astype(F32)
    t = pos / (l - 1)
    ang = ((2.0 * math.pi / l) * pos) * band_ref[...]
    lane = lax.broadcasted_iota(jnp.int32, (tr, LANES), 1)
    feat = jnp.where(lane == 0, t, jnp.where(lane <= HY_BANDS, jnp.cos(ang),
                                              jnp.where(lane <= 2 * HY_BANDS, -jnp.sin(ang), 0.0)))
    fr = freq_ref[...]
    hdn = jnp.sin(fr * (_dot3(feat, w1_ref[...]) + b1_ref[...]))
    hdn = jnp.sin(fr * (_dot3(hdn, w2_ref[...]) + b2_ref[...]))
    f = _dot3(hdn, w3_ref[0])
    reps = f.shape[1] // LANES
    t_all = jnp.concatenate([t] * reps, axis=1)
    m_all = jnp.concatenate([m] * reps, axis=1)
    f = jnp.where(m_all == l, 0.0, f * jnp.exp(-t_all * jnp.abs(decay_ref[0])))
    k_o[...] = f.astype(BF16)

    @pl.when(i == 0)
    def _():
        sum_o[...] = jnp.zeros_like(sum_o)

    sum_o[...] += jnp.sum(jnp.abs(f), axis=0, keepdims=True)


def hyena_filters(l, w1, b1, w2, b2, w3, freq, decay, tr=512):
    f32 = F32
    n_ord = decay.shape[0]
    hid = w1.shape[1]
    cols = n_ord * W_HYENA
    bands = jnp.linspace(1e-4, HY_BANDS - 1, HY_BANDS, dtype=f32)
    band_row = jnp.zeros((1, LANES), f32).at[0, 1:1 + HY_BANDS].set(bands).at[0, 1 + HY_BANDS:1 + 2 * HY_BANDS].set(bands)
    pad2 = lambda a, r, c: jnp.pad(a.astype(f32), ((0, r - a.shape[0]), (0, c - a.shape[1])))
    row = lambda a: pad2(a.reshape(1, -1), 1, LANES)
    w3h = jnp.transpose(w3.astype(f32).reshape(hid, n_ord, 2, W_HYENA), (2, 0, 1, 3)).reshape(2, hid, cols)
    w3h = jnp.pad(w3h, ((0, 0), (0, LANES - hid), (0, 0)))
    dech = jnp.transpose(decay.astype(f32), (1, 0, 2)).reshape(2, 1, cols)
    nt = 2 * l // tr
    fixed = lambda i: (0, 0)
    half = lambda i: (i // (nt // 2), 0, 0)
    k_un, ksum = pl.pallas_call(
        functools.partial(_filter_kernel, l=l),
        grid=(nt,),
        in_specs=[pl.BlockSpec((1, LANES), fixed), pl.BlockSpec((LANES, LANES), fixed), pl.BlockSpec((1, LANES), fixed),
                  pl.BlockSpec((LANES, LANES), fixed), pl.BlockSpec((1, LANES), fixed),
                  pl.BlockSpec((1, LANES, cols), half), pl.BlockSpec((1, LANES), fixed),
                  pl.BlockSpec((1, 1, cols), half)],
        out_specs=[pl.BlockSpec((tr, cols), lambda i: (i, 0)), pl.BlockSpec((8, cols), fixed)],
        out_shape=[jax.ShapeDtypeStruct((2 * l, cols), BF16), jax.ShapeDtypeStruct((8, cols), f32)],
        compiler_params=_cparams(("arbitrary",)),
        name="hyena_filter",
    )(band_row, pad2(w1, LANES, LANES), row(b1), pad2(w2, LANES, LANES), row(b2), w3h, row(freq), dech)
    return k_un, 1.0 / ksum[0:1]


FFT_KB = 4


def _num_k1(n1):
    return n1 // 2 + FFT_KB


def _dft_tables(n1, rows_in):
    nk = _num_k1(n1)
    nh = n1 // 2
    k1 = np.arange(nk)[:, None].astype(np.float64)
    r = np.arange(rows_in)[None, :].astype(np.float64)
    ang = 2.0 * np.pi * k1 * r / n1
    keep = (k1 <= nh).astype(np.float64)
    fwd = np.concatenate([np.cos(ang) * keep, -np.sin(ang) * keep], axis=0)
    wgt = np.where(k1[:nh] == 0, 1.0, 2.0)
    inv = np.concatenate([(np.cos(ang[:nh]) * wgt).T, (-np.sin(ang[:nh]) * wgt).T], axis=1)
    sign = np.cos(np.pi * r).T
    return jnp.asarray(fwd, BF16), jnp.asarray(inv, BF16), jnp.asarray(sign, F32)


def _mid_tables(n1):
    n2 = FFT_N2
    n = n1 * n2
    k = (np.arange(_num_k1(n1))[:, None, None] + n1 * np.arange(n2)[None, :, None]).astype(np.float64)
    m = np.arange(n2)[None, None, :].astype(np.float64)
    ang = 2.0 * np.pi * ((k * m) % n) / n
    gr, gi = np.cos(ang), -np.sin(ang)
    fwd = np.concatenate([np.concatenate([gr, -gi], axis=2), np.concatenate([gi, gr], axis=2)], axis=1)
    hr, hi = np.transpose(gr, (0, 2, 1)), -np.transpose(gi, (0, 2, 1))
    inv = np.concatenate([np.concatenate([hr, -hi], axis=2), np.concatenate([hi, hr], axis=2)], axis=1)
    return jnp.asarray(fwd, BF16), jnp.asarray(inv, BF16)


FFT_MB = 16


def _stage1_kernel(z_ref, f_ref, a_ref):
    nk = a_ref.shape[2]
    x = pltpu.einshape("rmc->mrc", z_ref[0])
    res = jnp.stack([jnp.dot(f_ref[...], x[m], preferred_element_type=F32).astype(BF16)
                     for m in range(x.shape[0])])
    res = pltpu.einshape("mkc->kmc", res)
    a_ref[0, 0] = res[:nk]
    a_ref[0, 1] = res[nk:]


def fft_stage1(z4, fwd, cb=512):
    bsz, r, n2, c = z4.shape
    nk = fwd.shape[0] // 2
    return pl.pallas_call(
        _stage1_kernel,
        grid=(bsz, n2 // FFT_MB, c // cb),
        in_specs=[pl.BlockSpec((1, r, FFT_MB, cb), lambda b, j, q: (b, 0, j, q)),
                  pl.BlockSpec(fwd.shape, lambda b, j, q: (0, 0))],
        out_specs=pl.BlockSpec((1, 2, nk, FFT_MB, cb), lambda b, j, q: (b, 0, 0, j, q)),
        out_shape=jax.ShapeDtypeStruct((bsz, 2, nk, n2, c), BF16),
        compiler_params=_cparams(("parallel", "parallel", "parallel")),
        name="fft_stage1",
    )(z4, fwd)


def _mid_kernel(a_ref, g_ref, h_ref, kr_ref, ki_ref, d_ref):
    n2 = FFT_N2
    for j in range(a_ref.shape[2]):
        ab = jnp.concatenate([a_ref[0, 0, j], a_ref[0, 1, j]], axis=0)
        z = jnp.dot(g_ref[j], ab, preferred_element_type=F32)
        zr, zi = z[:n2], z[n2:]
        kr, ki = kr_ref[j], ki_ref[j]
        yb = jnp.concatenate([zr * kr - zi * ki, zr * ki + zi * kr], axis=0).astype(BF16)
        d = jnp.dot(h_ref[j], yb, preferred_element_type=F32)
        d_ref[0, 0, j] = d[:n2].astype(BF16)
        d_ref[0, 1, j] = d[n2:].astype(BF16)


def _mid_fwd_kernel(a_ref, g_ref, s_ref, zr_ref, zi_ref):
    n2 = FFT_N2
    for j in range(a_ref.shape[2]):
        ab = jnp.concatenate([a_ref[0, 0, j], a_ref[0, 1, j]], axis=0)
        z = jnp.dot(g_ref[j], ab, preferred_element_type=F32) * s_ref[...]
        zr_ref[j] = z[:n2]
        zi_ref[j] = z[n2:]


def fft_mid(a5, g, h, kf_re, kf_im, order, kb=FFT_KB):
    bsz, _, n1, n2, c = a5.shape
    kb = min(kb, n1)
    blk = pl.BlockSpec((1, 2, kb, n2, c), lambda k, b: (b, 0, k, 0, 0))
    mat = pl.BlockSpec((kb, 2 * n2, 2 * n2), lambda k, b: (k, 0, 0))
    spec = pl.BlockSpec((kb, n2, c), lambda k, b: (k, 0, order))
    return pl.pallas_call(
        _mid_kernel,
        grid=(n1 // kb, bsz),
        in_specs=[blk, mat, mat, spec, spec],
        out_specs=blk,
        out_shape=jax.ShapeDtypeStruct(a5.shape, BF16),
        compiler_params=_cparams(("parallel", "arbitrary")),
        name="fft_mid",
    )(a5, g, h, kf_re, kf_im)


def fft_mid_fwd(a5, g, col_scale, kb=FFT_KB):
    _, _, n1, n2, c = a5.shape
    kb = min(kb, n1)
    spec = pl.BlockSpec((kb, n2, c), lambda k: (k, 0, 0))
    osd = jax.ShapeDtypeStruct((n1, n2, c), F32)
    return pl.pallas_call(
        _mid_fwd_kernel,
        grid=(n1 // kb,),
        in_specs=[pl.BlockSpec((1, 2, kb, n2, c), lambda k: (0, 0, k, 0, 0)),
                  pl.BlockSpec((kb, 2 * n2, 2 * n2), lambda k: (k, 0, 0)),
                  pl.BlockSpec((1, c), lambda k: (0, 0))],
        out_specs=[spec, spec],
        out_shape=[osd, osd],
        compiler_params=_cparams(("parallel",)),
        name="fft_mid_fwd",
    )(a5, g, col_scale)


def _fin_kernel(d_ref, inv_ref, sign_ref, z_ref, gate_ref, bias_ref, *rest, scale, chain):
    nk = d_ref.shape[2]
    nh = inv_ref.shape[1] // 2
    d_re = pltpu.einshape("kmc->mkc", d_ref[0, 0])
    d_im = pltpu.einshape("kmc->mkc", d_ref[0, 1])
    zt = pltpu.einshape("rmc->mrc", z_ref[0])
    gt = pltpu.einshape("rmc->mrc", gate_ref[0])
    z_new, a_new = [], []
    for m in range(zt.shape[0]):
        half = jnp.concatenate([d_re[m, :nh], d_im[m, :nh]], axis=0)
        nyq = d_re[m, nh:nh + 1].astype(F32)
        conv = (jnp.dot(inv_ref[...], half, preferred_element_type=F32) + sign_ref[...] * nyq) * scale
        zb = (gt[m].astype(F32) * (conv + zt[m].astype(F32) * bias_ref[...])).astype(BF16)
        z_new.append(zb)
        if chain:
            a_new.append(jnp.dot(rest[0][...], zb, preferred_element_type=F32).astype(BF16))
    if chain:
        _, z_o, a_o = rest
        a_all = pltpu.einshape("mkc->kmc", jnp.stack(a_new))
        a_o[0, 0] = a_all[:nk]
        a_o[0, 1] = a_all[nk:]
    else:
        (z_o,) = rest
    z_o[0] = pltpu.einshape("mrc->rmc", jnp.stack(z_new))


def fft_final(d5, inv, sign, z4, gate4, bias_row, scale, fwd=None):
    bsz, r, n2, c = z4.shape
    nk = d5.shape[2]
    chain = fwd is not None
    tok = pl.BlockSpec((1, r, FFT_MB, c), lambda b, j: (b, 0, j, 0))
    spec = pl.BlockSpec((1, 2, nk, FFT_MB, c), lambda b, j: (b, 0, 0, j, 0))
    fixed = lambda a: pl.BlockSpec(a.shape, lambda b, j: (0, 0))
    in_specs = [spec, fixed(inv), fixed(sign), tok, tok, fixed(bias_row)]
    out_specs = [tok]
    out_shape = [jax.ShapeDtypeStruct(z4.shape, BF16)]
    args = [d5, inv, sign, z4, gate4, bias_row]
    if chain:
        in_specs.append(fixed(fwd))
        out_specs.append(spec)
        out_shape.append(jax.ShapeDtypeStruct(d5.shape, BF16))
        args.append(fwd)
    return pl.pallas_call(
        functools.partial(_fin_kernel, scale=scale, chain=chain),
        grid=(bsz, n2 // FFT_MB),
        in_specs=in_specs,
        out_specs=out_specs,
        out_shape=out_shape,
        compiler_params=_cparams(("parallel", "parallel")),
        name="fft_final",
    )(*args)


def hyena_mixer(hz, bsz, seq, short_w, short_b, filt, filt_scale, bias):
    c = W_HYENA
    n2 = FFT_N2
    n = 2 * seq
    n1 = n // n2
    r = n1 // 2
    fwd_half, inv_half, sign_half = _dft_tables(n1, r)
    fwd_full = _dft_tables(n1, n1)[0]
    g, h = _mid_tables(n1)
    ka = fft_stage1(filt.reshape(1, n1, n2, 2 * c), fwd_full)
    kf_re, kf_im = fft_mid_fwd(ka, g, filt_scale)
    v, x1, x2 = hyena_short_conv(hz, short_w, short_b, seq)
    as4 = lambda t: t.reshape(bsz, r, n2, c)
    z4 = as4(v)
    a = fft_stage1(z4, fwd_half)
    for order, gate in enumerate((x1, x2)):
        d = fft_mid(a, g, h, kf_re, kf_im, order)
        bias_row = bias[order].astype(F32).reshape(1, c)
        res = fft_final(d, inv_half, sign_half, z4, as4(gate), bias_row, 1.0 / n,
                        fwd=fwd_half if order == 0 else None)
        if order == 0:
            z4, a = res
        else:
            (z4,) = res
    return z4.reshape(bsz * seq, c)


def _trunk_layer0(x, c, wts):
    bsz, seq, _ = x.shape
    n = bsz * seq
    mod = adaln_mod(c, wts['ada_w'], wts['ada_b'])
    xf = x.reshape(n, D_MODEL)
    q, k, v, u = in_proj0(xf, mod[0], wts['norm_mix_g'][0], wts['ab_w_in'], wts['na_q_g'], wts['na_k_g'], seq)
    att = neighbourhood_attention(q, k, v, wts['att_bias'], bsz, seq)
    cnv = conv_module(u, wts['cv_dw_w'], wts['cv_dw_b'], wts['cv_ln_g'], wts['cv_ln_b'], seq)
    xf, h, gate, gate_t = out_proj_router(att, cnv, xf, mod[0], wts['ab_w_out'], wts['norm_ffn_g'][0],
                                          wts['router_w'], wts['router_b'], seq)
    xf = moe_ffn(h, gate, gate_t, xf, mod[0], wts['moe_wg'][0], wts['moe_wu'][0], wts['moe_wd'][0], seq)
    ut, hz = in_proj1(xf, mod[1], wts['norm_mix_g'][1], wts['cd_w_u_t'], wts['cd_w_z'], seq)
    return xf, mod, ut, hz


def _trunk_layer1(xf, mod, yt, hz, bsz, seq, wts):
    ssm = s5_post(yt, wts['s5_glu_w'])
    filt, filt_scale = hyena_filters(seq, *wts['hy_mlp'])
    hy = hyena_mixer(hz, bsz, seq, wts['hy_short_w'], wts['hy_short_b'], filt, filt_scale, wts['hy_bias'])
    xf, h, gate, gate_t = out_proj_router(ssm, hy, xf, mod[1], wts['cd_w_out'], wts['norm_ffn_g'][1],
                                          wts['router_w'], wts['router_b'], seq)
    xf = moe_ffn(h, gate, gate_t, xf, mod[1], wts['moe_wg'][1], wts['moe_wu'][1], wts['moe_wd'][1], seq)
    return xf.reshape(bsz, seq, D_MODEL)


def kernel(x_prompt, x_sample, c_prompt, c_sample, ada_w, ada_b, norm_mix_g, norm_ffn_g, router_w, router_b, moe_w_gate, moe_w_up, moe_w_down, ab_w_in, ab_w_out, na_q_g, na_k_g, na_rpb, cv_dw_w, cv_dw_b, cv_ln_g, cv_ln_b, cd_w_in, cd_w_out, s5_lam_re, s5_lam_im, s5_log_dt, s5_b_re, s5_b_im, s5_c_re, s5_c_im, s5_d, s5_glu_w, hy_short_w, hy_short_b, hy_w1, hy_b1, hy_w2, hy_b2, hy_w3, hy_freq, hy_decay, hy_bias):
    s5_tabs = s5_tables(s5_lam_re[0], s5_lam_im[0], s5_log_dt[0], s5_b_re[0], s5_b_im[0],
                        s5_c_re[0], s5_c_im[0], s5_d[0])
    wts = dict(
        ada_w=ada_w, ada_b=ada_b, norm_mix_g=norm_mix_g.astype(F32), norm_ffn_g=norm_ffn_g.astype(F32),
        router_w=router_w, router_b=router_b,
        moe_wg=jnp.swapaxes(moe_w_gate, -1, -2).astype(BF16), moe_wu=jnp.swapaxes(moe_w_up, -1, -2).astype(BF16),
        moe_wd=jnp.swapaxes(moe_w_down, -1, -2).astype(BF16),
        ab_w_in=ab_w_in[0].astype(BF16), ab_w_out=ab_w_out[0].astype(BF16),
        na_q_g=na_q_g[0], na_k_g=na_k_g[0], att_bias=_att_bias_table(na_rpb[0]),
        cv_dw_w=cv_dw_w[0], cv_dw_b=cv_dw_b[0], cv_ln_g=cv_ln_g[0], cv_ln_b=cv_ln_b[0],
        cd_w_u_t=cd_w_in[0][:, :W_SSM].T.astype(BF16), cd_w_z=cd_w_in[0][:, W_SSM:].astype(BF16),
        cd_w_out=cd_w_out[0].astype(BF16),
        s5_glu_w=s5_glu_w[0].astype(BF16),
        hy_short_w=hy_short_w[0], hy_short_b=hy_short_b[0],
        hy_mlp=(hy_w1[0], hy_b1[0], hy_w2[0], hy_b2[0], hy_w3[0], hy_freq[0], hy_decay[0]),
        hy_bias=hy_bias[0],
    )
    xs = (x_prompt, x_sample)
    mids = [_trunk_layer0(x, c, wts) for x, c in zip(xs, (c_prompt, c_sample))]
    yts = s5_scan([m[2] for m in mids], [x.shape[1] for x in xs], *s5_tabs)
    return tuple(_trunk_layer1(m[0], m[1], yt, m[3], x.shape[0], x.shape[1], wts)
                 for m, yt, x in zip(mids, yts, xs))
```

```python
import functools
import math

import numpy as np
import jax
import jax.numpy as jnp
from jax import lax
from jax.experimental import pallas as pl
from jax.experimental.pallas import tpu as pltpu

F32 = jnp.float32
BF16 = jnp.bfloat16
HIGHEST = lax.Precision.HIGHEST

D_MODEL = 1024
DEPTH = 2
GRID_W = 64
W_ATT = 512
W_CONV = 512
W_SSM = 512
W_HYENA = 512
HEAD_DIM = 64
N_HEADS = 8
WIN_R = 8
WIN_C = 16
CONV_W = 31
S5_GROUP = 16
S5_GROUPS = 32
S5_STATE = 64
HY_BANDS = 8
N_EXPERTS = 16
N_GROUPS = 4
D_FF = 512
EPS = 1e-6
NEG_INF = -1e30

VMEM_LIMIT_BYTES = 56 * 1024 * 1024
LANES = 128
SUBLANES = 8

ATT_QROWS = 4
ATT_KROWS = 12
S5_CHUNK = LANES
FFT_N2 = 128


def _cparams(sem):
    return pltpu.CompilerParams(dimension_semantics=sem, vmem_limit_bytes=VMEM_LIMIT_BYTES)


def _mod_kernel(c_ref, w_ref, b_ref, o_ref):
    c = c_ref[...]
    s = c * jax.nn.sigmoid(c)
    o_ref[0] = jnp.dot(s, w_ref[0], precision=HIGHEST, preferred_element_type=F32) + b_ref[0]


def adaln_mod(c, ada_w, ada_b):
    bsz = c.shape[0]
    tn = D_MODEL
    out = pl.pallas_call(
        _mod_kernel,
        grid=(DEPTH, 6 * D_MODEL // tn),
        in_specs=[
            pl.BlockSpec((bsz, D_MODEL), lambda i, j: (0, 0)),
            pl.BlockSpec((1, D_MODEL, tn), lambda i, j: (i, 0, j)),
            pl.BlockSpec((1, 1, tn), lambda i, j: (i, 0, j)),
        ],
        out_specs=pl.BlockSpec((1, bsz, tn), lambda i, j: (i, 0, j)),
        out_shape=jax.ShapeDtypeStruct((DEPTH, bsz, 6 * D_MODEL), F32),
        compiler_params=_cparams(("arbitrary", "arbitrary")),
        name="adaln_mod",
    )(c, ada_w, ada_b.reshape(DEPTH, 1, 6 * D_MODEL))
    return out.reshape(DEPTH, bsz, 6, D_MODEL)


def _dot3(a, b):
    a_hi, b_hi = a.astype(BF16), b.astype(BF16)
    a_lo = (a - a_hi.astype(F32)).astype(BF16)
    b_lo = (b - b_hi.astype(F32)).astype(BF16)
    return (jnp.dot(a_hi, b_hi, preferred_element_type=F32) + jnp.dot(a_lo, b_hi, preferred_element_type=F32)
            + jnp.dot(a_hi, b_lo, preferred_element_type=F32))


def _norm_mod(x, g, shift, scale):
    ms = jnp.mean(x * x, axis=-1, keepdims=True)
    return x * lax.rsqrt(ms + EPS) * g * (1.0 + scale) + shift


def _in0_kernel(x_ref, mod_ref, g_ref, w_ref, hm_ref, qg_ref, kg_ref, q_o, k_o, v_o, u_o):
    m = mod_ref[0]
    h = _norm_mod(x_ref[...], g_ref[...], m[0:1], m[1:2])
    p = jnp.dot(h.astype(BF16), w_ref[...], preferred_element_type=F32)
    q = p[:, 0:W_ATT]
    k = p[:, W_ATT:2 * W_ATT]
    qms = jnp.dot((q * q).astype(BF16), hm_ref[...], preferred_element_type=F32)
    kms = jnp.dot((k * k).astype(BF16), hm_ref[...], preferred_element_type=F32)
    q_o[...] = (q * lax.rsqrt(qms + EPS) * qg_ref[...]).astype(BF16)
    k_o[...] = (k * lax.rsqrt(kms + EPS) * kg_ref[...]).astype(BF16)
    v_o[...] = p[:, 2 * W_ATT:3 * W_ATT].astype(BF16)
    val = p[:, 3 * W_ATT:3 * W_ATT + W_CONV]
    gate = p[:, 3 * W_ATT + W_CONV:]
    u_o[...] = (val * jax.nn.sigmoid(gate)).astype(BF16)


def in_proj0(x, mod, g, w_in_bf, q_gain, k_gain, seq, tm=512):
    n = x.shape[0]
    head_mean = jnp.asarray(np.kron(np.eye(N_HEADS), np.full((HEAD_DIM, HEAD_DIM), 1.0 / HEAD_DIM)), BF16)
    qg = (jnp.tile(q_gain.astype(F32), N_HEADS) * (HEAD_DIM ** -0.5)).reshape(1, W_ATT)
    kg = jnp.tile(k_gain.astype(F32), N_HEADS).reshape(1, W_ATT)
    tok = lambda i: (i, 0)
    fixed = lambda i: (0, 0)
    osd = jax.ShapeDtypeStruct((n, W_ATT), BF16)
    return pl.pallas_call(
        _in0_kernel,
        grid=(n // tm,),
        in_specs=[
            pl.BlockSpec((tm, D_MODEL), tok),
            pl.BlockSpec((1, 6, D_MODEL), lambda i: ((i * tm) // seq, 0, 0)),
            pl.BlockSpec((1, D_MODEL), fixed),
            pl.BlockSpec(w_in_bf.shape, fixed),
            pl.BlockSpec((W_ATT, W_ATT), fixed),
            pl.BlockSpec((1, W_ATT), fixed),
            pl.BlockSpec((1, W_ATT), fixed),
        ],
        out_specs=[pl.BlockSpec((tm, W_ATT), tok)] * 4,
        out_shape=[osd] * 4,
        compiler_params=_cparams(("parallel",)),
        name="in_proj0",
    )(x, mod, g.reshape(1, D_MODEL), w_in_bf, head_mean, qg, kg)


def _att_bias_table(rpb):
    a = np.arange(ATT_QROWS)[:, None, None, None]
    c = np.arange(GRID_W)[None, :, None, None]
    e = np.arange(ATT_KROWS)[None, None, :, None]
    kc = np.arange(GRID_W)[None, None, None, :]
    c0 = np.clip(c - WIN_C // 2, 0, GRID_W - WIN_C)
    col_ok = (kc >= c0) & (kc < c0 + WIN_C)
    rpb = rpb.astype(F32)
    per = 2 * GRID_W
    vrow = jnp.concatenate([rpb[..., WIN_C - 1:],
                            jnp.zeros(rpb.shape[:2] + (per - (2 * WIN_C - 1),), F32),
                            rpb[..., :WIN_C - 1]], axis=-1)
    tcol = jnp.tile(vrow, (1, 1, GRID_W))[..., :GRID_W * (per - 1)]
    tcol = tcol.reshape(rpb.shape[:2] + (GRID_W, per - 1))[..., :GRID_W]
    tables = []
    for case in range(3):
        if case == 0:
            dr = e - a
            row_ok = (e >= 0) & (e < WIN_R)
        elif case == 1:
            dr = e - a - WIN_R // 2
            row_ok = (dr >= -(WIN_R // 2)) & (dr < WIN_R // 2)
        else:
            dr = e - a - (ATT_KROWS - ATT_QROWS)
            row_ok = (e >= ATT_KROWS - WIN_R) & (e < ATT_KROWS)
        ok = np.broadcast_to(row_ok & col_ok, (ATT_QROWS, GRID_W, ATT_KROWS, GRID_W))
        dri = np.clip(dr + WIN_R - 1, 0, 2 * WIN_R - 2)[:, 0, :, 0]
        nq, nk = ATT_QROWS * GRID_W, ATT_KROWS * GRID_W
        t = jnp.concatenate([jnp.concatenate([tcol[:, int(dri[qa, ke])] for ke in range(ATT_KROWS)], axis=-1)
                             for qa in range(ATT_QROWS)], axis=-2)
        tables.append(jnp.where(jnp.asarray(ok.reshape(nq, nk)), t, NEG_INF))
    return jnp.stack(tables).astype(BF16)


def _att_kernel(q_ref, k0, k1, k2, v0, v1, v2, bias_ref, o_ref):
    kt = [k0, k1, k2]
    vt = [v0, v1, v2]
    nkb = len(kt)
    kw = k0.shape[0]
    tq = q_ref.shape[0]
    first = lax.broadcasted_iota(jnp.int32, (tq, LANES), 1) < HEAD_DIM
    for hp in range(N_HEADS // 2):
        ps = slice(hp * LANES, (hp + 1) * LANES)
        qp = q_ref[:, ps]
        res = []
        for sub in range(2):
            h = 2 * hp + sub
            qm = jnp.where(first if sub == 0 else jnp.logical_not(first), qp, jnp.zeros_like(qp))
            s = [lax.dot_general(qm, kt[j][:, ps], (((1,), (1,)), ((), ())), preferred_element_type=F32)
                 + bias_ref[0, h, :, j * kw:(j + 1) * kw].astype(F32) for j in range(nkb)]
            m = s[0].max(axis=-1, keepdims=True)
            for j in range(1, nkb):
                m = jnp.maximum(m, s[j].max(axis=-1, keepdims=True))
            p = [jnp.exp(sj - m) for sj in s]
            l = p[0].sum(axis=-1, keepdims=True)
            for j in range(1, nkb):
                l = l + p[j].sum(axis=-1, keepdims=True)
            o = jnp.dot(p[0].astype(BF16), vt[0][:, ps], preferred_element_type=F32)
            for j in range(1, nkb):
                o = o + jnp.dot(p[j].astype(BF16), vt[j][:, ps], preferred_element_type=F32)
            res.append(o / l)
        o_ref[:, ps] = jnp.where(first, res[0], res[1]).astype(BF16)


def neighbourhood_attention(q, k, v, bias, bsz, seq):
    rows = seq // GRID_W
    nqb = rows // ATT_QROWS
    tq = ATT_QROWS * GRID_W
    nkb = ATT_KROWS // ATT_QROWS
    assert rows % ATT_QROWS == 0 and nqb >= nkb

    def kmap(j):
        def f(i):
            b, r = i // nqb, i % nqb
            return (b * nqb + jnp.clip(r - 1, 0, nqb - nkb) + j, 0)
        return f

    def bias_map(i):
        r = i % nqb
        return (jnp.where(r == 0, 0, jnp.where(r == nqb - 1, 2, 1)), 0, 0, 0)

    kv_specs = [pl.BlockSpec((tq, W_ATT), kmap(j)) for j in range(nkb)]
    return pl.pallas_call(
        _att_kernel,
        grid=(bsz * nqb,),
        in_specs=[pl.BlockSpec((tq, W_ATT), lambda i: (i, 0))] + kv_specs + kv_specs
        + [pl.BlockSpec((1,) + bias.shape[1:], bias_map)],
        out_specs=pl.BlockSpec((tq, W_ATT), lambda i: (i, 0)),
        out_shape=jax.ShapeDtypeStruct(q.shape, BF16),
        compiler_params=_cparams(("parallel",)),
        name="neighbourhood_attention",
    )(q, k, k, k, v, v, v, bias)


CONV_HALO = 16


def _conv_kernel(prev_ref, cur_ref, next_ref, w_ref, b_ref, g_ref, be_ref, o_ref, win_ref, sh_ref, *, tiles_per_seq):
    i = pl.program_id(0)
    t = i % tiles_per_seq
    tt = cur_ref.shape[0]
    half = CONV_W // 2
    prev = prev_ref[...].astype(F32)
    nxt = next_ref[...].astype(F32)
    win_ref[0:CONV_HALO, :] = jnp.where(t == 0, 0.0, prev)
    win_ref[CONV_HALO:CONV_HALO + tt, :] = cur_ref[...].astype(F32)
    win_ref[CONV_HALO + tt:, :] = jnp.where(t == tiles_per_seq - 1, 0.0, nxt)
    span = tt + 2 * CONV_HALO - SUBLANES
    for ph in range(1, SUBLANES):
        sh_ref[ph, 0:span, :] = win_ref[ph:ph + span, :]
    w = w_ref[...]
    acc = jnp.zeros((tt, W_CONV), F32) + b_ref[...]
    for kk in range(CONV_W):
        off = CONV_HALO - half + kk
        ph, base = off % SUBLANES, (off // SUBLANES) * SUBLANES
        tap = win_ref[base:base + tt, :] if ph == 0 else sh_ref[ph, base:base + tt, :]
        acc = acc + tap * w[kk:kk + 1, :]
    mu = jnp.mean(acc, axis=-1, keepdims=True)
    d = acc - mu
    var = jnp.mean(d * d, axis=-1, keepdims=True)
    y = d * lax.rsqrt(var + EPS) * g_ref[...] + be_ref[...]
    o_ref[...] = (y * jax.nn.sigmoid(y)).astype(BF16)


def conv_module(u, dw_w, dw_b, ln_g, ln_b, seq, tt=512):
    n = u.shape[0]
    tps = seq // tt
    hb = tt // CONV_HALO
    nhb = n // CONV_HALO
    row = lambda a: a.astype(F32).reshape(1, W_CONV)
    fixed = lambda i: (0, 0)
    return pl.pallas_call(
        functools.partial(_conv_kernel, tiles_per_seq=tps),
        grid=(n // tt,),
        in_specs=[
            pl.BlockSpec((CONV_HALO, W_CONV), lambda i: (jnp.maximum(i * hb - 1, 0), 0)),
            pl.BlockSpec((tt, W_CONV), lambda i: (i, 0)),
            pl.BlockSpec((CONV_HALO, W_CONV), lambda i: (jnp.minimum((i + 1) * hb, nhb - 1), 0)),
            pl.BlockSpec((CONV_W, W_CONV), fixed),
            pl.BlockSpec((1, W_CONV), fixed),
            pl.BlockSpec((1, W_CONV), fixed),
            pl.BlockSpec((1, W_CONV), fixed),
        ],
        out_specs=pl.BlockSpec((tt, W_CONV), lambda i: (i, 0)),
        out_shape=jax.ShapeDtypeStruct((n, W_CONV), BF16),
        scratch_shapes=[pltpu.VMEM((tt + 2 * CONV_HALO, W_CONV), F32),
                        pltpu.VMEM((SUBLANES, tt + 2 * CONV_HALO, W_CONV), F32)],
        compiler_params=_cparams(("parallel",)),
        name="conv_module",
    )(u, u, u, dw_w.astype(F32), row(dw_b), row(ln_g), row(ln_b))


def _top2_sum(a, b, c, d):
    hi1, lo1 = jnp.maximum(a, b), jnp.minimum(a, b)
    hi2, lo2 = jnp.maximum(c, d), jnp.minimum(c, d)
    return jnp.maximum(hi1, hi2) + jnp.maximum(jnp.minimum(hi1, hi2), jnp.maximum(lo1, lo2))


def _router_gates(scores_t, bias_t):
    per = N_EXPERTS // N_GROUPS
    rows = [scores_t[e:e + 1, :] + bias_t[e:e + 1, :] for e in range(N_EXPERTS)]
    gscore = [_top2_sum(*rows[g * per:(g + 1) * per]) for g in range(N_GROUPS)]
    best = gscore[0]
    best_idx = jnp.zeros_like(best, dtype=jnp.int32)
    for g in range(1, N_GROUPS):
        better = gscore[g] > best
        best = jnp.where(better, gscore[g], best)
        best_idx = jnp.where(better, g, best_idx)
    out_row = lax.broadcasted_iota(jnp.int32, (LANES, scores_t.shape[1]), 0)
    gates = jnp.zeros((LANES, scores_t.shape[1]), F32)
    total = jnp.zeros_like(best)
    for e in range(N_EXPERTS):
        g = e // per
        rank = jnp.zeros_like(best_idx)
        for e2 in range(g * per, (g + 1) * per):
            if e2 == e:
                continue
            if e2 < e:
                ahead = rows[e2] >= rows[e]
            else:
                ahead = rows[e2] > rows[e]
            rank = rank + jnp.where(ahead, 1, 0)
        chosen = jnp.where(best_idx == g, rank, 2) < 2
        gated = jnp.where(chosen, scores_t[e:e + 1, :], 0.0)
        total = total + gated
        gates = jnp.where(out_row == e, gated, gates)
    return gates / total


def _out_kernel(a_ref, b_ref, x_ref, mod_ref, wa_ref, wb_ref, g_ref, rwc_ref, rb_ref, x_o, h_o, gate_o,
                gate_t_o):
    m = mod_ref[0]
    mix = (jnp.dot(a_ref[...], wa_ref[...], preferred_element_type=F32)
           + jnp.dot(b_ref[...], wb_ref[...], preferred_element_type=F32))
    x = x_ref[...] + m[2:3] * mix
    x_o[...] = x
    h = _norm_mod(x, g_ref[...], m[3:4], m[4:5])
    h_hi = h.astype(BF16)
    h_o[...] = h_hi
    h_lo = (h - h_hi.astype(F32)).astype(BF16)
    both = jnp.dot(h_hi, rwc_ref[...], preferred_element_type=F32)
    logits = both[:, :LANES] + both[:, LANES:] + jnp.dot(h_lo, rwc_ref[:, :LANES], preferred_element_type=F32)
    scores_t = jax.nn.sigmoid(logits).T
    gates_t = _router_gates(scores_t, rb_ref[...])
    gate_t_o[...] = gates_t
    gate_o[...] = gates_t.T


def out_proj_router(a, b, x, mod, w_out_bf, g_ffn, router_w, router_b, seq, tm=512):
    n = x.shape[0]
    half = a.shape[1]
    tok = lambda i: (i, 0)
    fixed = lambda i: (0, 0)
    rw = jnp.pad(router_w.astype(F32), ((0, 0), (0, LANES - N_EXPERTS)))
    rw_hi = rw.astype(BF16)
    rw_cat = jnp.concatenate([rw_hi, (rw - rw_hi.astype(F32)).astype(BF16)], axis=1)
    return pl.pallas_call(
        _out_kernel,
        grid=(n // tm,),
        in_specs=[
            pl.BlockSpec((tm, half), tok),
            pl.BlockSpec((tm, half), tok),
            pl.BlockSpec((tm, D_MODEL), tok),
            pl.BlockSpec((1, 6, D_MODEL), lambda i: ((i * tm) // seq, 0, 0)),
            pl.BlockSpec((half, D_MODEL), lambda i: (0, 0)),
            pl.BlockSpec((half, D_MODEL), lambda i: (1, 0)),
            pl.BlockSpec((1, D_MODEL), fixed),
            pl.BlockSpec((D_MODEL, 2 * LANES), fixed),
            pl.BlockSpec((N_EXPERTS, 1), fixed),
        ],
        out_specs=[pl.BlockSpec((tm, D_MODEL), tok), pl.BlockSpec((tm, D_MODEL), tok),
                   pl.BlockSpec((tm, LANES), tok), pl.BlockSpec((LANES, tm), lambda i: (0, i))],
        out_shape=[jax.ShapeDtypeStruct((n, D_MODEL), F32), jax.ShapeDtypeStruct((n, D_MODEL), BF16),
                   jax.ShapeDtypeStruct((n, LANES), F32), jax.ShapeDtypeStruct((LANES, n), F32)],
        compiler_params=_cparams(("parallel",)),
        name="out_proj_router",
    )(a, b, x, mod, w_out_bf, w_out_bf, g_ffn.reshape(1, D_MODEL), rw_cat,
      router_b.astype(F32).reshape(N_EXPERTS, 1))


MOE_COLS = 256
EXPERTS_PER_GROUP = N_EXPERTS // N_GROUPS


def _moe_kernel(h_ref, gate_ref, gate_t_ref, x_ref, mod_ref, tri_ref, wg_ref, wu_ref, wd_ref, o_ref,
                rankc_ref, rankr_ref, ht_ref, xg_ref, gg_ref, yg_ref, acc_ref, cnt_ref):
    e = pl.program_id(1)
    g = e // EXPERTS_PER_GROUP
    j = e % EXPERTS_PER_GROUP
    T = h_ref.shape[0]
    R = MOE_COLS

    @pl.when(e == 0)
    def _():
        er = lax.broadcasted_iota(jnp.int32, (LANES, LANES), 0)
        ec = lax.broadcasted_iota(jnp.int32, (LANES, LANES), 1)
        sel_c = jnp.where((er < N_EXPERTS) & (er // EXPERTS_PER_GROUP == ec), 1.0, 0.0).astype(BF16)
        sel_r = jnp.where((ec < N_EXPERTS) & (ec // EXPERTS_PER_GROUP == er), 1.0, 0.0).astype(BF16)
        chosen_c = jnp.where(gate_ref[...] > 0.0, 1.0, 0.0).astype(BF16)
        memb_c = jnp.dot(chosen_c, sel_c, preferred_element_type=F32) > 0.0
        rank_c = jnp.dot(tri_ref[...], jnp.where(memb_c, 1.0, 0.0).astype(BF16), preferred_element_type=F32)
        rankc_ref[...] = jnp.where(memb_c, rank_c, -1.0)
        chosen_r = jnp.where(gate_t_ref[...] > 0.0, 1.0, 0.0).astype(BF16)
        memb_r = jnp.dot(sel_r, chosen_r, preferred_element_type=F32) > 0.0
        ones_r = jnp.where(memb_r, 1.0, 0.0)
        rank_r = lax.dot_general(ones_r.astype(BF16), tri_ref[...], (((1,), (1,)), ((), ())),
                                 preferred_element_type=F32)
        rankr_ref[...] = jnp.where(memb_r, rank_r, -1.0)
        for gi in range(N_GROUPS):
            cnt_ref[gi] = jnp.sum(ones_r[gi:gi + 1, :]).astype(jnp.int32)
        ht_ref[...] = h_ref[...].astype(F32).T.astype(BF16)
        acc_ref[...] = jnp.zeros_like(acc_ref)

    nch = (cnt_ref[g] + (R - 1)) // R

    @pl.when(j == 0)
    def _():
        lane_t = lax.broadcasted_iota(jnp.int32, (T, LANES), 1)
        rc = jnp.sum(jnp.where(lane_t == g, rankc_ref[...], 0.0), axis=1, keepdims=True)
        col = lax.broadcasted_iota(jnp.int32, (T, R), 1).astype(F32)
        gt = gate_t_ref[...]
        g_hi = gt.astype(BF16)
        g_lo = (gt - g_hi.astype(F32)).astype(BF16)

        def gather(c, carry):
            pt = jnp.where(rc - (c * R).astype(F32) == col, 1.0, 0.0).astype(BF16)
            xg_ref[c] = jnp.dot(ht_ref[...], pt, preferred_element_type=F32).astype(BF16)
            gg_ref[c] = (jnp.dot(g_hi, pt, preferred_element_type=F32)
                         + jnp.dot(g_lo, pt, preferred_element_type=F32))
            yg_ref[c] = jnp.zeros((D_MODEL, R), F32)
            return carry

        lax.fori_loop(0, nch, gather, 0)

    def ffn(c, carry):
        xc = xg_ref[c]
        a = jnp.dot(wg_ref[0], xc, preferred_element_type=F32)
        u = jnp.dot(wu_ref[0], xc, preferred_element_type=F32)
        hid = (a * jax.nn.sigmoid(a)) * u * gg_ref[c, pl.ds(e, 1), :]
        yg_ref[c] += jnp.dot(wd_ref[0], hid.astype(BF16), preferred_element_type=F32)
        return carry

    lax.fori_loop(0, nch, ffn, 0)

    @pl.when(j == EXPERTS_PER_GROUP - 1)
    def _():
        rr = rankr_ref[pl.ds(g, 1), :]
        row = lax.broadcasted_iota(jnp.int32, (R, T), 0).astype(F32)

        def scatter(c, carry):
            p = jnp.where(rr - (c * R).astype(F32) == row, 1.0, 0.0).astype(BF16)
            acc_ref[...] += jnp.dot(yg_ref[c].astype(BF16), p, preferred_element_type=F32)
            return carry

        lax.fori_loop(0, nch, scatter, 0)

    @pl.when(e == N_EXPERTS - 1)
    def _():
        o_ref[...] = x_ref[...] + mod_ref[0][5:6] * acc_ref[...].T


def moe_ffn(h, gate, gate_t, x, mod, wgt_bf, wut_bf, wdt_bf, seq, tm=1024):
    n = x.shape[0]
    tm = min(tm, seq)
    nch = tm // MOE_COLS
    tok = lambda i, e: (i, 0)
    tri = jnp.asarray(np.tril(np.ones((tm, tm), np.float32), -1), BF16)
    return pl.pallas_call(
        _moe_kernel,
        grid=(n // tm, N_EXPERTS),
        in_specs=[
            pl.BlockSpec((tm, D_MODEL), tok),
            pl.BlockSpec((tm, LANES), tok),
            pl.BlockSpec((LANES, tm), lambda i, e: (0, i)),
            pl.BlockSpec((tm, D_MODEL), tok),
            pl.BlockSpec((1, 6, D_MODEL), lambda i, e: ((i * tm) // seq, 0, 0)),
            pl.BlockSpec((tm, tm), lambda i, e: (0, 0)),
            pl.BlockSpec((1, D_FF, D_MODEL), lambda i, e: (e, 0, 0)),
            pl.BlockSpec((1, D_FF, D_MODEL), lambda i, e: (e, 0, 0)),
            pl.BlockSpec((1, D_MODEL, D_FF), lambda i, e: (e, 0, 0)),
        ],
        out_specs=pl.BlockSpec((tm, D_MODEL), tok),
        out_shape=jax.ShapeDtypeStruct((n, D_MODEL), F32),
        scratch_shapes=[pltpu.VMEM((tm, LANES), F32), pltpu.VMEM((LANES, tm), F32),
                        pltpu.VMEM((D_MODEL, tm), BF16), pltpu.VMEM((nch, D_MODEL, MOE_COLS), BF16),
                        pltpu.VMEM((nch, LANES, MOE_COLS), F32), pltpu.VMEM((nch, D_MODEL, MOE_COLS), F32),
                        pltpu.VMEM((D_MODEL, tm), F32), pltpu.SMEM((N_GROUPS,), jnp.int32)],
        compiler_params=_cparams(("parallel", "arbitrary")),
        name="moe_ffn",
    )(h, gate, gate_t, x, mod, tri, wgt_bf, wut_bf, wdt_bf)


def _in1_kernel(x_ref, mod_ref, g_ref, wut_ref, wz_ref, ut_o, hz_o):
    m = mod_ref[0]
    h = _norm_mod(x_ref[...], g_ref[...], m[0:1], m[1:2]).astype(BF16)
    ut_o[...] = lax.dot_general(wut_ref[...], h, (((1,), (1,)), ((), ())),
                                preferred_element_type=F32).astype(BF16)
    hz_o[...] = jnp.dot(h, wz_ref[...], preferred_element_type=F32).astype(BF16)


def in_proj1(x, mod, g, w_u_t_bf, w_z_bf, seq, tm=512):
    n = x.shape[0]
    fixed = lambda i: (0, 0)
    return pl.pallas_call(
        _in1_kernel,
        grid=(n // tm,),
        in_specs=[
            pl.BlockSpec((tm, D_MODEL), lambda i: (i, 0)),
            pl.BlockSpec((1, 6, D_MODEL), lambda i: ((i * tm) // seq, 0, 0)),
            pl.BlockSpec((1, D_MODEL), fixed),
            pl.BlockSpec(w_u_t_bf.shape, fixed),
            pl.BlockSpec(w_z_bf.shape, fixed),
        ],
        out_specs=[pl.BlockSpec((W_SSM, tm), lambda i: (0, i)),
                   pl.BlockSpec((tm, 3 * W_HYENA), lambda i: (i, 0))],
        out_shape=[jax.ShapeDtypeStruct((W_SSM, n), BF16), jax.ShapeDtypeStruct((n, 3 * W_HYENA), BF16)],
        compiler_params=_cparams(("parallel",)),
        name="in_proj1",
    )(x, mod, g.reshape(1, D_MODEL), w_u_t_bf, w_z_bf)


def s5_tables(lam_re, lam_im, log_dt, b_re, b_im, c_re, c_im, d_skip):
    T = S5_CHUNK
    f32 = F32
    lags = jnp.arange(T, dtype=f32)

    def disc(d):
        lr, li = lam_re[d].astype(f32), lam_im[d].astype(f32)
        dt = jnp.exp(log_dt[d].astype(f32))[:, None]
        mag = jnp.exp(lr * dt)
        ab_re, ab_im = mag * jnp.cos(li * dt), mag * jnp.sin(li * dt)
        den = lr * lr + li * li
        f_re = ((ab_re - 1.0) * lr + ab_im * li) / den
        f_im = (ab_im * lr - (ab_re - 1.0) * li) / den
        br, bi = b_re[d].astype(f32), b_im[d].astype(f32)
        bb_re = f_re[..., None] * br - f_im[..., None] * bi
        bb_im = f_re[..., None] * bi + f_im[..., None] * br

        def power(p):
            ang = li * dt
            mg = jnp.exp(p[:, None, None] * (lr * dt)[None])
            return mg * jnp.cos(p[:, None, None] * ang[None]), mg * jnp.sin(p[:, None, None] * ang[None])
        return bb_re, bb_im, c_re[d].astype(f32), c_im[d].astype(f32), power

    hp = dict(precision=HIGHEST)
    tabs = []
    for d in range(2):
        bb_re, bb_im, cr, ci, power = disc(d)
        pr, pi = power(lags)
        cb_rr = jnp.einsum('gon,lgn,gni->lgoi', cr, pr, bb_re, **hp)
        cb_ii = jnp.einsum('gon,lgn,gni->lgoi', cr, pi, bb_im, **hp)
        cb_ri = jnp.einsum('gon,lgn,gni->lgoi', ci, pr, bb_im, **hp)
        cb_ir = jnp.einsum('gon,lgn,gni->lgoi', ci, pi, bb_re, **hp)
        kern = cb_rr - cb_ii - cb_ri - cb_ir
        qp = (T - 1.0 - lags) if d == 0 else lags
        qr, qi = power(qp)
        wp_re = jnp.einsum('tgn,gni->gitn', qr, bb_re) - jnp.einsum('tgn,gni->gitn', qi, bb_im)
        wp_im = jnp.einsum('tgn,gni->gitn', qr, bb_im) + jnp.einsum('tgn,gni->gitn', qi, bb_re)
        rp = (lags + 1.0) if d == 0 else (T - lags)
        rr, ri = power(rp)
        m_re = jnp.einsum('gon,tgn->gnot', cr, rr) - jnp.einsum('gon,tgn->gnot', ci, ri)
        m_im = jnp.einsum('gon,tgn->gnot', cr, ri) + jnp.einsum('gon,tgn->gnot', ci, rr)
        levels = 2.0 ** jnp.arange(16, dtype=f32) * T
        ar, ai = power(levels)
        tabs.append((kern, wp_re, wp_im, m_re, -m_im, ar, ai))

    kf, kb = tabs[0][0], tabs[1][0]
    skip = jnp.eye(S5_GROUP, dtype=f32)[None] * d_skip.astype(f32).reshape(S5_GROUPS, S5_GROUP, 1)
    k0 = kf[0] + kb[0] + skip
    kk = jnp.concatenate([k0[None], kf[1:], jnp.zeros_like(k0)[None], kb[:0:-1]], axis=0)
    kk = jnp.transpose(kk, (1, 3, 2, 0))
    g_, i_, o_ = kk.shape[:3]
    kk = kk.reshape(g_, i_ * o_, 2 * T)
    w_state = jnp.concatenate([tabs[0][1], tabs[0][2], tabs[1][1], tabs[1][2]], axis=-1)
    wp = w_state.reshape(g_, i_ * T, 4 * S5_STATE).astype(BF16)
    wc = jnp.concatenate([tabs[0][3], tabs[0][4], tabs[1][3], tabs[1][4]], axis=1)
    wc = wc.reshape(g_, 4 * S5_STATE, o_ * T).astype(BF16)
    mult = []
    for d in range(2):
        ar, ai = tabs[d][5], tabs[d][6]
        mult += [jnp.concatenate([ar, ar], axis=-1), jnp.concatenate([-ai, ai], axis=-1)]
    scan_mult = jnp.transpose(jnp.stack(mult, axis=2), (1, 0, 2, 3))
    return kk, wp, wc, scan_mult


def _s5_kernel(*refs, chunks_per_seq):
    nt = len(chunks_per_seq)
    u_refs, (kk_ref, wp_ref, wc_ref, mult_ref) = refs[:nt], refs[nt:nt + 4]
    o_refs, w_ref = refs[nt + 4:2 * nt + 4], refs[2 * nt + 4]
    T = S5_CHUNK

    def build(i, carry):
        for o in range(S5_GROUP):
            row = kk_ref[0, pl.ds(i * S5_GROUP + o, 1), :]
            toe = pltpu.roll(jnp.broadcast_to(row, (T, 2 * T)), 0, axis=1, stride=1, stride_axis=0)
            w_ref[pl.ds(pl.multiple_of(i * T, T), T), o * T:(o + 1) * T] = toe[:, :T].astype(BF16)
        return carry

    lax.fori_loop(0, S5_GROUP, build, 0)
    for u_ref, o_ref, cps in zip(u_refs, o_refs, chunks_per_seq):
        _s5_apply(u_ref, o_ref, w_ref, wp_ref, wc_ref, mult_ref, cps)


def _s5_apply(u_ref, o_ref, w_ref, wp_ref, wc_ref, mult_ref, chunks_per_seq):
    T = S5_CHUNK
    ns = 2 * S5_STATE
    x = jnp.concatenate([u_ref[i] for i in range(S5_GROUP)], axis=1)
    y = jnp.dot(x, w_ref[...], preferred_element_type=F32)
    pst = jnp.dot(x, wp_ref[0], preferred_element_type=F32)
    nc = y.shape[0]
    cidx = lax.broadcasted_iota(jnp.int32, (nc, ns), 0) % chunks_per_seq

    def cmul(s, mre, mim):
        return s * mre + pltpu.roll(s, S5_STATE, axis=1) * mim

    def scan(p, d):
        s = p
        k, step = 0, 1
        while step < chunks_per_seq:
            mre = mult_ref[0, k, 2 * d:2 * d + 1, :]
            mim = mult_ref[0, k, 2 * d + 1:2 * d + 2, :]
            if d == 0:
                sh = jnp.where(cidx >= step, pltpu.roll(s, step, axis=0), 0.0)
            else:
                sh = jnp.where(cidx < chunks_per_seq - step, pltpu.roll(s, nc - step, axis=0), 0.0)
            s = s + cmul(sh, mre, mim)
            k, step = k + 1, step * 2
        if d == 0:
            return jnp.where(cidx >= 1, pltpu.roll(s, 1, axis=0), 0.0)
        return jnp.where(cidx < chunks_per_seq - 1, pltpu.roll(s, nc - 1, axis=0), 0.0)

    sf = scan(pst[:, :ns], 0)
    sb = scan(pst[:, ns:], 1)
    carry = jnp.concatenate([sf, sb], axis=1).astype(BF16)
    ytot = y + jnp.dot(carry, wc_ref[0], preferred_element_type=F32)
    for o in range(S5_GROUP):
        o_ref[o] = ytot[:, o * T:(o + 1) * T]


def s5_scan(uts, seqs, kk, wp, wc, scan_mult):
    T = S5_CHUNK
    u3 = [ut.reshape(W_SSM, ut.shape[1] // T, T) for ut in uts]
    blk = [pl.BlockSpec((S5_GROUP,) + u.shape[1:], lambda g: (g, 0, 0)) for u in u3]
    grp = lambda a: pl.BlockSpec((1,) + a.shape[1:], lambda g: (g,) + (0,) * (a.ndim - 1))
    outs = pl.pallas_call(
        functools.partial(_s5_kernel, chunks_per_seq=tuple(s // T for s in seqs)),
        grid=(S5_GROUPS,),
        in_specs=blk + [grp(kk), grp(wp), grp(wc), grp(scan_mult)],
        out_specs=blk,
        out_shape=[jax.ShapeDtypeStruct(u.shape, F32) for u in u3],
        scratch_shapes=[pltpu.VMEM((S5_GROUP * T, S5_GROUP * T), BF16)],
        compiler_params=_cparams(("parallel",)),
        name="s5_scan",
    )(*u3, kk, wp, wc, scan_mult)
    return [o.reshape(W_SSM, -1) for o in outs]


def _s5_post_kernel(yt_ref, w_ref, o_ref):
    y = yt_ref[...].T
    y = 0.5 * y * (1.0 + jnp.tanh(math.sqrt(2.0 / math.pi) * (y + 0.044715 * (y * y * y))))
    z = jnp.dot(y.astype(BF16), w_ref[...], preferred_element_type=F32)
    o_ref[...] = (y * jax.nn.sigmoid(z)).astype(BF16)


def s5_post(yt, glu_w_bf, tm=2048):
    n = yt.shape[1]
    return pl.pallas_call(
        _s5_post_kernel,
        grid=(n // tm,),
        in_specs=[pl.BlockSpec((W_SSM, tm), lambda i: (0, i)),
                  pl.BlockSpec((W_SSM, W_SSM), lambda i: (0, 0))],
        out_specs=pl.BlockSpec((tm, W_SSM), lambda i: (i, 0)),
        out_shape=jax.ShapeDtypeStruct((n, W_SSM), BF16),
        compiler_params=_cparams(("parallel",)),
        name="s5_post",
    )(yt, glu_w_bf)


SHORT_HALO = 16


def _short_kernel(prev_ref, cur_ref, next_ref, w_ref, b_ref, v_o, x1_o, x2_o, *, tiles_per_seq):
    i = pl.program_id(0)
    t = i % tiles_per_seq
    cur = cur_ref[...].astype(F32)
    tt = cur.shape[0]
    before = jnp.where(t == 0, 0.0, prev_ref[SHORT_HALO - 1:SHORT_HALO, :].astype(F32))
    after = jnp.where(t == tiles_per_seq - 1, 0.0, next_ref[0:1, :].astype(F32))
    ridx = lax.broadcasted_iota(jnp.int32, cur.shape, 0)
    left = jnp.where(ridx == 0, before, pltpu.roll(cur, 1, axis=0))
    right = jnp.where(ridx == tt - 1, after, pltpu.roll(cur, tt - 1, axis=0))
    w = w_ref[...]
    p = left * w[0:1] + cur * w[1:2] + right * w[2:3] + b_ref[...]
    v_o[...] = p[:, 0:W_HYENA].astype(BF16)
    x1_o[...] = p[:, W_HYENA:2 * W_HYENA].astype(BF16)
    x2_o[...] = p[:, 2 * W_HYENA:].astype(BF16)


def hyena_short_conv(hz, short_w, short_b, seq, tt=512):
    n, c3 = hz.shape
    tps = seq // tt
    hb = tt // SHORT_HALO
    nhb = n // SHORT_HALO
    fixed = lambda i: (0, 0)
    osd = jax.ShapeDtypeStruct((n, W_HYENA), BF16)
    return pl.pallas_call(
        functools.partial(_short_kernel, tiles_per_seq=tps),
        grid=(n // tt,),
        in_specs=[
            pl.BlockSpec((SHORT_HALO, c3), lambda i: (jnp.maximum(i * hb - 1, 0), 0)),
            pl.BlockSpec((tt, c3), lambda i: (i, 0)),
            pl.BlockSpec((SHORT_HALO, c3), lambda i: (jnp.minimum((i + 1) * hb, nhb - 1), 0)),
            pl.BlockSpec((3, c3), fixed),
            pl.BlockSpec((1, c3), fixed),
        ],
        out_specs=[pl.BlockSpec((tt, W_HYENA), lambda i: (i, 0))] * 3,
        out_shape=[osd] * 3,
        compiler_params=_cparams(("parallel",)),
        name="hyena_short_conv",
    )(hz, hz, hz, short_w.astype(F32), short_b.astype(F32).reshape(1, c3))


def _filter_kernel(band_ref, w1_ref, b1_ref, w2_ref, b2_ref, w3_ref, freq_ref, decay_ref, k_o, sum_o, *, l):
    i = pl.program_id(0)
    tr = k_o.shape[0]
    m = i * tr + lax.broadcasted_iota(jnp.int32, (tr, LANES), 0)
    pos = jnp.where(m > l, 2 * l - m, m).astype(F32)
    t = pos / (l - 1)
    ang = ((2.0 * math.pi / l) * pos) * band_ref[...]
    lane = lax.broadcasted_iota(jnp.int32, (tr, LANES), 1)
    feat = jnp.where(lane == 0, t, jnp.where(lane <= HY_BANDS, jnp.cos(ang),
                                              jnp.where(lane <= 2 * HY_BANDS, -jnp.sin(ang), 0.0)))
    fr = freq_ref[...]
    hdn = jnp.sin(fr * (_dot3(feat, w1_ref[...]) + b1_ref[...]))
    hdn = jnp.sin(fr * (_dot3(hdn, w2_ref[...]) + b2_ref[...]))
    f = _dot3(hdn, w3_ref[0])
    reps = f.shape[1] // LANES
    t_all = jnp.concatenate([t] * reps, axis=1)
    m_all = jnp.concatenate([m] * reps, axis=1)
    f = jnp.where(m_all == l, 0.0, f * jnp.exp(-t_all * jnp.abs(decay_ref[0])))
    k_o[...] = f.astype(BF16)

    @pl.when(i == 0)
    def _():
        sum_o[...] = jnp.zeros_like(sum_o)

    sum_o[...] += jnp.sum(jnp.abs(f), axis=0, keepdims=True)


def hyena_filters(l, w1, b1, w2, b2, w3, freq, decay, tr=512):
    f32 = F32
    n_ord = decay.shape[0]
    hid = w1.shape[1]
    cols = n_ord * W_HYENA
    bands = jnp.linspace(1e-4, HY_BANDS - 1, HY_BANDS, dtype=f32)
    band_row = jnp.zeros((1, LANES), f32).at[0, 1:1 + HY_BANDS].set(bands).at[0, 1 + HY_BANDS:1 + 2 * HY_BANDS].set(bands)
    pad2 = lambda a, r, c: jnp.pad(a.astype(f32), ((0, r - a.shape[0]), (0, c - a.shape[1])))
    row = lambda a: pad2(a.reshape(1, -1), 1, LANES)
    w3h = jnp.transpose(w3.astype(f32).reshape(hid, n_ord, 2, W_HYENA), (2, 0, 1, 3)).reshape(2, hid, cols)
    w3h = jnp.pad(w3h, ((0, 0), (0, LANES - hid), (0, 0)))
    dech = jnp.transpose(decay.astype(f32), (1, 0, 2)).reshape(2, 1, cols)
    nt = 2 * l // tr
    fixed = lambda i: (0, 0)
    half = lambda i: (i // (nt // 2), 0, 0)
    k_un, ksum = pl.pallas_call(
        functools.partial(_filter_kernel, l=l),
        grid=(nt,),
        in_specs=[pl.BlockSpec((1, LANES), fixed), pl.BlockSpec((LANES, LANES), fixed), pl.BlockSpec((1, LANES), fixed),
                  pl.BlockSpec((LANES, LANES), fixed), pl.BlockSpec((1, LANES), fixed),
                  pl.BlockSpec((1, LANES, cols), half), pl.BlockSpec((1, LANES), fixed),
                  pl.BlockSpec((1, 1, cols), half)],
        out_specs=[pl.BlockSpec((tr, cols), lambda i: (i, 0)), pl.BlockSpec((8, cols), fixed)],
        out_shape=[jax.ShapeDtypeStruct((2 * l, cols), BF16), jax.ShapeDtypeStruct((8, cols), f32)],
        compiler_params=_cparams(("arbitrary",)),
        name="hyena_filter",
    )(band_row, pad2(w1, LANES, LANES), row(b1), pad2(w2, LANES, LANES), row(b2), w3h, row(freq), dech)
    return k_un, 1.0 / ksum[0:1]


FFT_KB = 4


def _num_k1(n1):
    return n1 // 2 + FFT_KB


def _dft_tables(n1, rows_in):
    nk = _num_k1(n1)
    nh = n1 // 2
    k1 = np.arange(nk)[:, None].astype(np.float64)
    r = np.arange(rows_in)[None, :].astype(np.float64)
    ang = 2.0 * np.pi * k1 * r / n1
    keep = (k1 <= nh).astype(np.float64)
    fwd = np.concatenate([np.cos(ang) * keep, -np.sin(ang) * keep], axis=0)
    wgt = np.where(k1[:nh] == 0, 1.0, 2.0)
    inv = np.concatenate([(np.cos(ang[:nh]) * wgt).T, (-np.sin(ang[:nh]) * wgt).T], axis=1)
    sign = np.cos(np.pi * r).T
    return jnp.asarray(fwd, BF16), jnp.asarray(inv, BF16), jnp.asarray(sign, F32)


def _mid_tables(n1):
    n2 = FFT_N2
    n = n1 * n2
    k = (np.arange(_num_k1(n1))[:, None, None] + n1 * np.arange(n2)[None, :, None]).astype(np.float64)
    m = np.arange(n2)[None, None, :].astype(np.float64)
    ang = 2.0 * np.pi * ((k * m) % n) / n
    gr, gi = np.cos(ang), -np.sin(ang)
    fwd = np.concatenate([np.concatenate([gr, -gi], axis=2), np.concatenate([gi, gr], axis=2)], axis=1)
    hr, hi = np.transpose(gr, (0, 2, 1)), -np.transpose(gi, (0, 2, 1))
    inv = np.concatenate([np.concatenate([hr, -hi], axis=2), np.concatenate([hi, hr], axis=2)], axis=1)
    return jnp.asarray(fwd, BF16), jnp.asarray(inv, BF16)


FFT_MB = 16


def _stage1_kernel(z_ref, f_ref, a_ref):
    nk = a_ref.shape[-3]
    lead = (0,) * (len(a_ref.shape) - 5)
    x = pltpu.einshape("rmc->mrc", z_ref[0])
    res = jnp.stack([jnp.dot(f_ref[...], x[m], preferred_element_type=F32).astype(BF16)
                     for m in range(x.shape[0])])
    res = pltpu.einshape("mkc->kmc", res)
    a_ref[(0, 0) + lead] = res[:nk]
    a_ref[(0, 1) + lead] = res[nk:]


def fft_stage1(z4, fwd, cb=512, blocked=True):
    bsz, r, n2, c = z4.shape
    nk = fwd.shape[0] // 2
    if blocked:
        out_spec = pl.BlockSpec((1, 2, 1, nk, FFT_MB, cb), lambda b, j, q: (b, 0, j, 0, 0, q))
        out_shape = jax.ShapeDtypeStruct((bsz, 2, n2 // FFT_MB, nk, FFT_MB, c), BF16)
    else:
        out_spec = pl.BlockSpec((1, 2, nk, FFT_MB, cb), lambda b, j, q: (b, 0, 0, j, q))
        out_shape = jax.ShapeDtypeStruct((bsz, 2, nk, n2, c), BF16)
    return pl.pallas_call(
        _stage1_kernel,
        grid=(bsz, n2 // FFT_MB, c // cb),
        in_specs=[pl.BlockSpec((1, r, FFT_MB, cb), lambda b, j, q: (b, 0, j, q)),
                  pl.BlockSpec(fwd.shape, lambda b, j, q: (0, 0))],
        out_specs=out_spec,
        out_shape=out_shape,
        compiler_params=_cparams(("parallel", "parallel", "parallel")),
        name="fft_stage1",
    )(z4, fwd)


def _mid_kernel(a_ref, g_ref, h_ref, kr_ref, ki_ref, d_ref):
    n2 = FFT_N2
    c = a_ref.shape[-1]
    blk = d_ref.shape[2], d_ref.shape[4], c
    for j in range(a_ref.shape[3]):
        ab = jnp.concatenate([a_ref[0, 0, :, j].reshape(n2, c), a_ref[0, 1, :, j].reshape(n2, c)], axis=0)
        z = jnp.dot(g_ref[j], ab, preferred_element_type=F32)
        zr, zi = z[:n2], z[n2:]
        kr, ki = kr_ref[j], ki_ref[j]
        yb = jnp.concatenate([zr * kr - zi * ki, zr * ki + zi * kr], axis=0).astype(BF16)
        d = jnp.dot(h_ref[j], yb, preferred_element_type=F32)
        d_ref[0, 0, :, j] = d[:n2].astype(BF16).reshape(blk)
        d_ref[0, 1, :, j] = d[n2:].astype(BF16).reshape(blk)


def _mid_fwd_kernel(a_ref, g_ref, s_ref, zr_ref, zi_ref):
    n2 = FFT_N2
    for j in range(a_ref.shape[2]):
        ab = jnp.concatenate([a_ref[0, 0, j], a_ref[0, 1, j]], axis=0)
        z = jnp.dot(g_ref[j], ab, preferred_element_type=F32) * s_ref[...]
        zr_ref[j] = z[:n2]
        zi_ref[j] = z[n2:]


def fft_mid(a5, g, h, kf_re, kf_im, order, kb=FFT_KB):
    bsz, _, nmb, n1, mb, c = a5.shape
    n2 = nmb * mb
    kb = min(kb, n1)
    blk = pl.BlockSpec((1, 2, nmb, kb, mb, c), lambda k, b: (b, 0, 0, k, 0, 0))
    mat = pl.BlockSpec((kb, 2 * n2, 2 * n2), lambda k, b: (k, 0, 0))
    spec = pl.BlockSpec((kb, n2, c), lambda k, b: (k, 0, order))
    return pl.pallas_call(
        _mid_kernel,
        grid=(n1 // kb, bsz),
        in_specs=[blk, mat, mat, spec, spec],
        out_specs=blk,
        out_shape=jax.ShapeDtypeStruct(a5.shape, BF16),
        compiler_params=_cparams(("parallel", "arbitrary")),
        name="fft_mid",
    )(a5, g, h, kf_re, kf_im)


def fft_mid_fwd(a5, g, col_scale, kb=FFT_KB):
    _, _, n1, n2, c = a5.shape
    kb = min(kb, n1)
    spec = pl.BlockSpec((kb, n2, c), lambda k: (k, 0, 0))
    osd = jax.ShapeDtypeStruct((n1, n2, c), F32)
    return pl.pallas_call(
        _mid_fwd_kernel,
        grid=(n1 // kb,),
        in_specs=[pl.BlockSpec((1, 2, kb, n2, c), lambda k: (0, 0, k, 0, 0)),
                  pl.BlockSpec((kb, 2 * n2, 2 * n2), lambda k: (k, 0, 0)),
                  pl.BlockSpec((1, c), lambda k: (0, 0))],
        out_specs=[spec, spec],
        out_shape=[osd, osd],
        compiler_params=_cparams(("parallel",)),
        name="fft_mid_fwd",
    )(a5, g, col_scale)


def _fin_kernel(d_ref, inv_ref, sign_ref, z_ref, gate_ref, bias_ref, *rest, scale, chain):
    nk = d_ref.shape[3]
    nh = inv_ref.shape[1] // 2
    d_re = pltpu.einshape("kmc->mkc", d_ref[0, 0, 0])
    d_im = pltpu.einshape("kmc->mkc", d_ref[0, 1, 0])
    zt = pltpu.einshape("rmc->mrc", z_ref[0])
    gt = pltpu.einshape("rmc->mrc", gate_ref[0])
    z_new, a_new = [], []
    for m in range(zt.shape[0]):
        half = jnp.concatenate([d_re[m, :nh], d_im[m, :nh]], axis=0)
        nyq = d_re[m, nh:nh + 1].astype(F32)
        conv = (jnp.dot(inv_ref[...], half, preferred_element_type=F32) + sign_ref[...] * nyq) * scale
        zb = (gt[m].astype(F32) * (conv + zt[m].astype(F32) * bias_ref[...])).astype(BF16)
        z_new.append(zb)
        if chain:
            a_new.append(jnp.dot(rest[0][...], zb, preferred_element_type=F32).astype(BF16))
    if chain:
        _, z_o, a_o = rest
        a_all = pltpu.einshape("mkc->kmc", jnp.stack(a_new))
        a_o[0, 0, 0] = a_all[:nk]
        a_o[0, 1, 0] = a_all[nk:]
    else:
        (z_o,) = rest
    z_o[0] = pltpu.einshape("mrc->rmc", jnp.stack(z_new))


def fft_final(d5, inv, sign, z4, gate4, bias_row, scale, fwd=None):
    bsz, r, n2, c = z4.shape
    nk = d5.shape[3]
    chain = fwd is not None
    tok = pl.BlockSpec((1, r, FFT_MB, c), lambda b, j: (b, 0, j, 0))
    spec = pl.BlockSpec((1, 2, 1, nk, FFT_MB, c), lambda b, j: (b, 0, j, 0, 0, 0))
    fixed = lambda a: pl.BlockSpec(a.shape, lambda b, j: (0, 0))
    in_specs = [spec, fixed(inv), fixed(sign), tok, tok, fixed(bias_row)]
    out_specs = [tok]
    out_shape = [jax.ShapeDtypeStruct(z4.shape, BF16)]
    args = [d5, inv, sign, z4, gate4, bias_row]
    if chain:
        in_specs.append(fixed(fwd))
        out_specs.append(spec)
        out_shape.append(jax.ShapeDtypeStruct(d5.shape, BF16))
        args.append(fwd)
    return pl.pallas_call(
        functools.partial(_fin_kernel, scale=scale, chain=chain),
        grid=(bsz, n2 // FFT_MB),
        in_specs=in_specs,
        out_specs=out_specs,
        out_shape=out_shape,
        compiler_params=_cparams(("parallel", "parallel")),
        name="fft_final",
    )(*args)


def hyena_mixer(hz, bsz, seq, short_w, short_b, filt, filt_scale, bias):
    c = W_HYENA
    n2 = FFT_N2
    n = 2 * seq
    n1 = n // n2
    r = n1 // 2
    fwd_half, inv_half, sign_half = _dft_tables(n1, r)
    fwd_full = _dft_tables(n1, n1)[0]
    g, h = _mid_tables(n1)
    ka = fft_stage1(filt.reshape(1, n1, n2, 2 * c), fwd_full, blocked=False)
    kf_re, kf_im = fft_mid_fwd(ka, g, filt_scale)
    v, x1, x2 = hyena_short_conv(hz, short_w, short_b, seq)
    as4 = lambda t: t.reshape(bsz, r, n2, c)
    z4 = as4(v)
    a = fft_stage1(z4, fwd_half)
    for order, gate in enumerate((x1, x2)):
        d = fft_mid(a, g, h, kf_re, kf_im, order)
        bias_row = bias[order].astype(F32).reshape(1, c)
        res = fft_final(d, inv_half, sign_half, z4, as4(gate), bias_row, 1.0 / n,
                        fwd=fwd_half if order == 0 else None)
        if order == 0:
            z4, a = res
        else:
            (z4,) = res
    return z4.reshape(bsz * seq, c)


def _trunk_layer0(x, c, wts):
    bsz, seq, _ = x.shape
    n = bsz * seq
    mod = adaln_mod(c, wts['ada_w'], wts['ada_b'])
    xf = x.reshape(n, D_MODEL)
    q, k, v, u = in_proj0(xf, mod[0], wts['norm_mix_g'][0], wts['ab_w_in'], wts['na_q_g'], wts['na_k_g'], seq)
    att = neighbourhood_attention(q, k, v, wts['att_bias'], bsz, seq)
    cnv = conv_module(u, wts['cv_dw_w'], wts['cv_dw_b'], wts['cv_ln_g'], wts['cv_ln_b'], seq)
    xf, h, gate, gate_t = out_proj_router(att, cnv, xf, mod[0], wts['ab_w_out'], wts['norm_ffn_g'][0],
                                          wts['router_w'], wts['router_b'], seq)
    xf = moe_ffn(h, gate, gate_t, xf, mod[0], wts['moe_wg'][0], wts['moe_wu'][0], wts['moe_wd'][0], seq)
    ut, hz = in_proj1(xf, mod[1], wts['norm_mix_g'][1], wts['cd_w_u_t'], wts['cd_w_z'], seq)
    return xf, mod, ut, hz


def _trunk_layer1(xf, mod, yt, hz, bsz, seq, wts):
    ssm = s5_post(yt, wts['s5_glu_w'])
    filt, filt_scale = hyena_filters(seq, *wts['hy_mlp'])
    hy = hyena_mixer(hz, bsz, seq, wts['hy_short_w'], wts['hy_short_b'], filt, filt_scale, wts['hy_bias'])
    xf, h, gate, gate_t = out_proj_router(ssm, hy, xf, mod[1], wts['cd_w_out'], wts['norm_ffn_g'][1],
                                          wts['router_w'], wts['router_b'], seq)
    xf = moe_ffn(h, gate, gate_t, xf, mod[1], wts['moe_wg'][1], wts['moe_wu'][1], wts['moe_wd'][1], seq)
    return xf.reshape(bsz, seq, D_MODEL)


def kernel(x_prompt, x_sample, c_prompt, c_sample, ada_w, ada_b, norm_mix_g, norm_ffn_g, router_w, router_b, moe_w_gate, moe_w_up, moe_w_down, ab_w_in, ab_w_out, na_q_g, na_k_g, na_rpb, cv_dw_w, cv_dw_b, cv_ln_g, cv_ln_b, cd_w_in, cd_w_out, s5_lam_re, s5_lam_im, s5_log_dt, s5_b_re, s5_b_im, s5_c_re, s5_c_im, s5_d, s5_glu_w, hy_short_w, hy_short_b, hy_w1, hy_b1, hy_w2, hy_b2, hy_w3, hy_freq, hy_decay, hy_bias):
    s5_tabs = s5_tables(s5_lam_re[0], s5_lam_im[0], s5_log_dt[0], s5_b_re[0], s5_b_im[0],
                        s5_c_re[0], s5_c_im[0], s5_d[0])
    wts = dict(
        ada_w=ada_w, ada_b=ada_b, norm_mix_g=norm_mix_g.astype(F32), norm_ffn_g=norm_ffn_g.astype(F32),
        router_w=router_w, router_b=router_b,
        moe_wg=jnp.swapaxes(moe_w_gate, -1, -2).astype(BF16), moe_wu=jnp.swapaxes(moe_w_up, -1, -2).astype(BF16),
        moe_wd=jnp.swapaxes(moe_w_down, -1, -2).astype(BF16),
        ab_w_in=ab_w_in[0].astype(BF16), ab_w_out=ab_w_out[0].astype(BF16),
        na_q_g=na_q_g[0], na_k_g=na_k_g[0], att_bias=_att_bias_table(na_rpb[0]),
        cv_dw_w=cv_dw_w[0], cv_dw_b=cv_dw_b[0], cv_ln_g=cv_ln_g[0], cv_ln_b=cv_ln_b[0],
        cd_w_u_t=cd_w_in[0][:, :W_SSM].T.astype(BF16), cd_w_z=cd_w_in[0][:, W_SSM:].astype(BF16),
        cd_w_out=cd_w_out[0].astype(BF16),
        s5_glu_w=s5_glu_w[0].astype(BF16),
        hy_short_w=hy_short_w[0], hy_short_b=hy_short_b[0],
        hy_mlp=(hy_w1[0], hy_b1[0], hy_w2[0], hy_b2[0], hy_w3[0], hy_freq[0], hy_decay[0]),
        hy_bias=hy_bias[0],
    )
    xs = (x_prompt, x_sample)
    mids = [_trunk_layer0(x, c, wts) for x, c in zip(xs, (c_prompt, c_sample))]
    yts = s5_scan([m[2] for m in mids], [x.shape[1] for x in xs], *s5_tabs)
    return tuple(_trunk_layer1(m[0], m[1], yt, m[3], x.shape[0], x.shape[1], wts)
                 for m, yt, x in zip(mids, yts, xs))
```

```python
import functools
import math

import numpy as np
import jax
import jax.numpy as jnp
from jax import lax
from jax.experimental import pallas as pl
from jax.experimental.pallas import tpu as pltpu

F32 = jnp.float32
BF16 = jnp.bfloat16
HIGHEST = lax.Precision.HIGHEST

D_MODEL = 1024
DEPTH = 2
GRID_W = 64
W_ATT = 512
W_CONV = 512
W_SSM = 512
W_HYENA = 512
HEAD_DIM = 64
N_HEADS = 8
WIN_R = 8
WIN_C = 16
CONV_W = 31
S5_GROUP = 16
S5_GROUPS = 32
S5_STATE = 64
HY_BANDS = 8
N_EXPERTS = 16
N_GROUPS = 4
D_FF = 512
EPS = 1e-6
NEG_INF = -1e30

VMEM_LIMIT_BYTES = 56 * 1024 * 1024
LANES = 128
SUBLANES = 8

ATT_QROWS = 4
ATT_KROWS = 12
S5_CHUNK = LANES
FFT_N2 = 128


def _cparams(sem):
    return pltpu.CompilerParams(dimension_semantics=sem, vmem_limit_bytes=VMEM_LIMIT_BYTES)


def _mod_kernel(c_ref, w_ref, b_ref, o_ref):
    c = c_ref[...]
    s = c * jax.nn.sigmoid(c)
    o_ref[0] = jnp.dot(s, w_ref[0], precision=HIGHEST, preferred_element_type=F32) + b_ref[0]


def adaln_mod(c, ada_w, ada_b):
    bsz = c.shape[0]
    tn = D_MODEL
    out = pl.pallas_call(
        _mod_kernel,
        grid=(DEPTH, 6 * D_MODEL // tn),
        in_specs=[
            pl.BlockSpec((bsz, D_MODEL), lambda i, j: (0, 0)),
            pl.BlockSpec((1, D_MODEL, tn), lambda i, j: (i, 0, j)),
            pl.BlockSpec((1, 1, tn), lambda i, j: (i, 0, j)),
        ],
        out_specs=pl.BlockSpec((1, bsz, tn), lambda i, j: (i, 0, j)),
        out_shape=jax.ShapeDtypeStruct((DEPTH, bsz, 6 * D_MODEL), F32),
        compiler_params=_cparams(("arbitrary", "arbitrary")),
        name="adaln_mod",
    )(c, ada_w, ada_b.reshape(DEPTH, 1, 6 * D_MODEL))
    return out.reshape(DEPTH, bsz, 6, D_MODEL)


def _dot3(a, b):
    a_hi, b_hi = a.astype(BF16), b.astype(BF16)
    a_lo = (a - a_hi.astype(F32)).astype(BF16)
    b_lo = (b - b_hi.astype(F32)).astype(BF16)
    return (jnp.dot(a_hi, b_hi, preferred_element_type=F32) + jnp.dot(a_lo, b_hi, preferred_element_type=F32)
            + jnp.dot(a_hi, b_lo, preferred_element_type=F32))


def _norm_mod(x, g, shift, scale):
    ms = jnp.mean(x * x, axis=-1, keepdims=True)
    return x * lax.rsqrt(ms + EPS) * g * (1.0 + scale) + shift


def _in0_kernel(x_ref, mod_ref, g_ref, w_ref, hm_ref, qg_ref, kg_ref, q_o, k_o, v_o, u_o):
    m = mod_ref[0]
    h = _norm_mod(x_ref[...], g_ref[...], m[0:1], m[1:2])
    p = jnp.dot(h.astype(BF16), w_ref[...], preferred_element_type=F32)
    q = p[:, 0:W_ATT]
    k = p[:, W_ATT:2 * W_ATT]
    qms = jnp.dot((q * q).astype(BF16), hm_ref[...], preferred_element_type=F32)
    kms = jnp.dot((k * k).astype(BF16), hm_ref[...], preferred_element_type=F32)
    q_o[...] = (q * lax.rsqrt(qms + EPS) * qg_ref[...]).astype(BF16)
    k_o[...] = (k * lax.rsqrt(kms + EPS) * kg_ref[...]).astype(BF16)
    v_o[...] = p[:, 2 * W_ATT:3 * W_ATT].astype(BF16)
    val = p[:, 3 * W_ATT:3 * W_ATT + W_CONV]
    gate = p[:, 3 * W_ATT + W_CONV:]
    u_o[...] = (val * jax.nn.sigmoid(gate)).astype(BF16)


def in_proj0(x, mod, g, w_in_bf, q_gain, k_gain, seq, tm=512):
    n = x.shape[0]
    head_mean = jnp.asarray(np.kron(np.eye(N_HEADS), np.full((HEAD_DIM, HEAD_DIM), 1.0 / HEAD_DIM)), BF16)
    qg = (jnp.tile(q_gain.astype(F32), N_HEADS) * (HEAD_DIM ** -0.5)).reshape(1, W_ATT)
    kg = jnp.tile(k_gain.astype(F32), N_HEADS).reshape(1, W_ATT)
    tok = lambda i: (i, 0)
    fixed = lambda i: (0, 0)
    osd = jax.ShapeDtypeStruct((n, W_ATT), BF16)
    return pl.pallas_call(
        _in0_kernel,
        grid=(n // tm,),
        in_specs=[
            pl.BlockSpec((tm, D_MODEL), tok),
            pl.BlockSpec((1, 6, D_MODEL), lambda i: ((i * tm) // seq, 0, 0)),
            pl.BlockSpec((1, D_MODEL), fixed),
            pl.BlockSpec(w_in_bf.shape, fixed),
            pl.BlockSpec((W_ATT, W_ATT), fixed),
            pl.BlockSpec((1, W_ATT), fixed),
            pl.BlockSpec((1, W_ATT), fixed),
        ],
        out_specs=[pl.BlockSpec((tm, W_ATT), tok)] * 4,
        out_shape=[osd] * 4,
        compiler_params=_cparams(("parallel",)),
        name="in_proj0",
    )(x, mod, g.reshape(1, D_MODEL), w_in_bf, head_mean, qg, kg)


def _att_bias_table(rpb):
    a = np.arange(ATT_QROWS)[:, None, None, None]
    c = np.arange(GRID_W)[None, :, None, None]
    e = np.arange(ATT_KROWS)[None, None, :, None]
    kc = np.arange(GRID_W)[None, None, None, :]
    c0 = np.clip(c - WIN_C // 2, 0, GRID_W - WIN_C)
    col_ok = (kc >= c0) & (kc < c0 + WIN_C)
    rpb = rpb.astype(F32)
    per = 2 * GRID_W
    vrow = jnp.concatenate([rpb[..., WIN_C - 1:],
                            jnp.zeros(rpb.shape[:2] + (per - (2 * WIN_C - 1),), F32),
                            rpb[..., :WIN_C - 1]], axis=-1)
    tcol = jnp.tile(vrow, (1, 1, GRID_W))[..., :GRID_W * (per - 1)]
    tcol = tcol.reshape(rpb.shape[:2] + (GRID_W, per - 1))[..., :GRID_W]
    tables = []
    for case in range(3):
        if case == 0:
            dr = e - a
            row_ok = (e >= 0) & (e < WIN_R)
        elif case == 1:
            dr = e - a - WIN_R // 2
            row_ok = (dr >= -(WIN_R // 2)) & (dr < WIN_R // 2)
        else:
            dr = e - a - (ATT_KROWS - ATT_QROWS)
            row_ok = (e >= ATT_KROWS - WIN_R) & (e < ATT_KROWS)
        ok = np.broadcast_to(row_ok & col_ok, (ATT_QROWS, GRID_W, ATT_KROWS, GRID_W))
        dri = np.clip(dr + WIN_R - 1, 0, 2 * WIN_R - 2)[:, 0, :, 0]
        nq, nk = ATT_QROWS * GRID_W, ATT_KROWS * GRID_W
        t = jnp.concatenate([jnp.concatenate([tcol[:, int(dri[qa, ke])] for ke in range(ATT_KROWS)], axis=-1)
                             for qa in range(ATT_QROWS)], axis=-2)
        tables.append(jnp.where(jnp.asarray(ok.reshape(nq, nk)), t, NEG_INF))
    return jnp.stack(tables).astype(BF16)


def _att_kernel(q_ref, k0, k1, k2, v0, v1, v2, bias_ref, o_ref):
    kt = [k0, k1, k2]
    vt = [v0, v1, v2]
    nkb = len(kt)
    kw = k0.shape[0]
    tq = q_ref.shape[0]
    first = lax.broadcasted_iota(jnp.int32, (tq, LANES), 1) < HEAD_DIM
    for hp in range(N_HEADS // 2):
        ps = slice(hp * LANES, (hp + 1) * LANES)
        qp = q_ref[:, ps]
        res = []
        for sub in range(2):
            h = 2 * hp + sub
            qm = jnp.where(first if sub == 0 else jnp.logical_not(first), qp, jnp.zeros_like(qp))
            s = [lax.dot_general(qm, kt[j][:, ps], (((1,), (1,)), ((), ())), preferred_element_type=F32)
                 + bias_ref[0, h, :, j * kw:(j + 1) * kw].astype(F32) for j in range(nkb)]
            m = s[0].max(axis=-1, keepdims=True)
            for j in range(1, nkb):
                m = jnp.maximum(m, s[j].max(axis=-1, keepdims=True))
            p = [jnp.exp(sj - m) for sj in s]
            l = p[0].sum(axis=-1, keepdims=True)
            for j in range(1, nkb):
                l = l + p[j].sum(axis=-1, keepdims=True)
            o = jnp.dot(p[0].astype(BF16), vt[0][:, ps], preferred_element_type=F32)
            for j in range(1, nkb):
                o = o + jnp.dot(p[j].astype(BF16), vt[j][:, ps], preferred_element_type=F32)
            res.append(o / l)
        o_ref[:, ps] = jnp.where(first, res[0], res[1]).astype(BF16)


def neighbourhood_attention(q, k, v, bias, bsz, seq):
    rows = seq // GRID_W
    nqb = rows // ATT_QROWS
    tq = ATT_QROWS * GRID_W
    nkb = ATT_KROWS // ATT_QROWS
    assert rows % ATT_QROWS == 0 and nqb >= nkb

    def kmap(j):
        def f(i):
            b, r = i // nqb, i % nqb
            return (b * nqb + jnp.clip(r - 1, 0, nqb - nkb) + j, 0)
        return f

    def bias_map(i):
        r = i % nqb
        return (jnp.where(r == 0, 0, jnp.where(r == nqb - 1, 2, 1)), 0, 0, 0)

    kv_specs = [pl.BlockSpec((tq, W_ATT), kmap(j)) for j in range(nkb)]
    return pl.pallas_call(
        _att_kernel,
        grid=(bsz * nqb,),
        in_specs=[pl.BlockSpec((tq, W_ATT), lambda i: (i, 0))] + kv_specs + kv_specs
        + [pl.BlockSpec((1,) + bias.shape[1:], bias_map)],
        out_specs=pl.BlockSpec((tq, W_ATT), lambda i: (i, 0)),
        out_shape=jax.ShapeDtypeStruct(q.shape, BF16),
        compiler_params=_cparams(("parallel",)),
        name="neighbourhood_attention",
    )(q, k, k, k, v, v, v, bias)


CONV_HALO = 16


def _conv_kernel(prev_ref, cur_ref, next_ref, w_ref, b_ref, g_ref, be_ref, o_ref, win_ref, sh_ref, *, tiles_per_seq):
    i = pl.program_id(0)
    t = i % tiles_per_seq
    tt = cur_ref.shape[0]
    half = CONV_W // 2
    prev = prev_ref[...].astype(F32)
    nxt = next_ref[...].astype(F32)
    win_ref[0:CONV_HALO, :] = jnp.where(t == 0, 0.0, prev)
    win_ref[CONV_HALO:CONV_HALO + tt, :] = cur_ref[...].astype(F32)
    win_ref[CONV_HALO + tt:, :] = jnp.where(t == tiles_per_seq - 1, 0.0, nxt)
    span = tt + 2 * CONV_HALO - SUBLANES
    for ph in range(1, SUBLANES):
        sh_ref[ph, 0:span, :] = win_ref[ph:ph + span, :]
    w = w_ref[...]
    acc = jnp.zeros((tt, W_CONV), F32) + b_ref[...]
    for kk in range(CONV_W):
        off = CONV_HALO - half + kk
        ph, base = off % SUBLANES, (off // SUBLANES) * SUBLANES
        tap = win_ref[base:base + tt, :] if ph == 0 else sh_ref[ph, base:base + tt, :]
        acc = acc + tap * w[kk:kk + 1, :]
    mu = jnp.mean(acc, axis=-1, keepdims=True)
    d = acc - mu
    var = jnp.mean(d * d, axis=-1, keepdims=True)
    y = d * lax.rsqrt(var + EPS) * g_ref[...] + be_ref[...]
    o_ref[...] = (y * jax.nn.sigmoid(y)).astype(BF16)


def conv_module(u, dw_w, dw_b, ln_g, ln_b, seq, tt=512):
    n = u.shape[0]
    tps = seq // tt
    hb = tt // CONV_HALO
    nhb = n // CONV_HALO
    row = lambda a: a.astype(F32).reshape(1, W_CONV)
    fixed = lambda i: (0, 0)
    return pl.pallas_call(
        functools.partial(_conv_kernel, tiles_per_seq=tps),
        grid=(n // tt,),
        in_specs=[
            pl.BlockSpec((CONV_HALO, W_CONV), lambda i: (jnp.maximum(i * hb - 1, 0), 0)),
            pl.BlockSpec((tt, W_CONV), lambda i: (i, 0)),
            pl.BlockSpec((CONV_HALO, W_CONV), lambda i: (jnp.minimum((i + 1) * hb, nhb - 1), 0)),
            pl.BlockSpec((CONV_W, W_CONV), fixed),
            pl.BlockSpec((1, W_CONV), fixed),
            pl.BlockSpec((1, W_CONV), fixed),
            pl.BlockSpec((1, W_CONV), fixed),
        ],
        out_specs=pl.BlockSpec((tt, W_CONV), lambda i: (i, 0)),
        out_shape=jax.ShapeDtypeStruct((n, W_CONV), BF16),
        scratch_shapes=[pltpu.VMEM((tt + 2 * CONV_HALO, W_CONV), F32),
                        pltpu.VMEM((SUBLANES, tt + 2 * CONV_HALO, W_CONV), F32)],
        compiler_params=_cparams(("parallel",)),
        name="conv_module",
    )(u, u, u, dw_w.astype(F32), row(dw_b), row(ln_g), row(ln_b))


def _top2_sum(a, b, c, d):
    hi1, lo1 = jnp.maximum(a, b), jnp.minimum(a, b)
    hi2, lo2 = jnp.maximum(c, d), jnp.minimum(c, d)
    return jnp.maximum(hi1, hi2) + jnp.maximum(jnp.minimum(hi1, hi2), jnp.maximum(lo1, lo2))


def _router_gates(scores_t, bias_t):
    per = N_EXPERTS // N_GROUPS
    rows = [scores_t[e:e + 1, :] + bias_t[e:e + 1, :] for e in range(N_EXPERTS)]
    gscore = [_top2_sum(*rows[g * per:(g + 1) * per]) for g in range(N_GROUPS)]
    best = gscore[0]
    best_idx = jnp.zeros_like(best, dtype=jnp.int32)
    for g in range(1, N_GROUPS):
        better = gscore[g] > best
        best = jnp.where(better, gscore[g], best)
        best_idx = jnp.where(better, g, best_idx)
    out_row = lax.broadcasted_iota(jnp.int32, (LANES, scores_t.shape[1]), 0)
    gates = jnp.zeros((LANES, scores_t.shape[1]), F32)
    total = jnp.zeros_like(best)
    for e in range(N_EXPERTS):
        g = e // per
        rank = jnp.zeros_like(best_idx)
        for e2 in range(g * per, (g + 1) * per):
            if e2 == e:
                continue
            if e2 < e:
                ahead = rows[e2] >= rows[e]
            else:
                ahead = rows[e2] > rows[e]
            rank = rank + jnp.where(ahead, 1, 0)
        chosen = jnp.where(best_idx == g, rank, 2) < 2
        gated = jnp.where(chosen, scores_t[e:e + 1, :], 0.0)
        total = total + gated
        gates = jnp.where(out_row == e, gated, gates)
    return gates / total


MOE_COLS = 256
EXPERTS_PER_GROUP = N_EXPERTS // N_GROUPS


def _moe_kernel(a_ref, b_ref, x_ref, mod_ref, wa_ref, wb_ref, g_ref, rwc_ref, rb_ref, tri_ref, wg_ref, wu_ref, wd_ref,
                o_ref, rankc_ref, rankr_ref, ht_ref, xg_ref, gg_ref, yg_ref, acc_ref, xn_ref, gt_ref, cnt_ref):
    e = pl.program_id(1)
    g = e // EXPERTS_PER_GROUP
    j = e % EXPERTS_PER_GROUP
    T = x_ref.shape[0]
    R = MOE_COLS

    @pl.when(e == 0)
    def _():
        m = mod_ref[0]
        mix = (jnp.dot(a_ref[...], wa_ref[...], preferred_element_type=F32)
               + jnp.dot(b_ref[...], wb_ref[...], preferred_element_type=F32))
        x = x_ref[...] + m[2:3] * mix
        xn_ref[...] = x
        h = _norm_mod(x, g_ref[...], m[3:4], m[4:5])
        h_hi = h.astype(BF16)
        h_lo = (h - h_hi.astype(F32)).astype(BF16)
        both = jnp.dot(h_hi, rwc_ref[...], preferred_element_type=F32)
        logits = both[:, :LANES] + both[:, LANES:] + jnp.dot(h_lo, rwc_ref[:, :LANES], preferred_element_type=F32)
        gates_t = _router_gates(jax.nn.sigmoid(logits).T, rb_ref[...])
        gt_ref[...] = gates_t
        gates = gates_t.T
        ht_ref[...] = h.T.astype(BF16)
        er = lax.broadcasted_iota(jnp.int32, (LANES, LANES), 0)
        ec = lax.broadcasted_iota(jnp.int32, (LANES, LANES), 1)
        sel_c = jnp.where((er < N_EXPERTS) & (er // EXPERTS_PER_GROUP == ec), 1.0, 0.0).astype(BF16)
        sel_r = jnp.where((ec < N_EXPERTS) & (ec // EXPERTS_PER_GROUP == er), 1.0, 0.0).astype(BF16)
        chosen_c = jnp.where(gates > 0.0, 1.0, 0.0).astype(BF16)
        memb_c = jnp.dot(chosen_c, sel_c, preferred_element_type=F32) > 0.0
        rank_c = jnp.dot(tri_ref[...], jnp.where(memb_c, 1.0, 0.0).astype(BF16), preferred_element_type=F32)
        rankc_ref[...] = jnp.where(memb_c, rank_c, -1.0)
        chosen_r = jnp.where(gates_t > 0.0, 1.0, 0.0).astype(BF16)
        memb_r = jnp.dot(sel_r, chosen_r, preferred_element_type=F32) > 0.0
        ones_r = jnp.where(memb_r, 1.0, 0.0)
        rank_r = lax.dot_general(ones_r.astype(BF16), tri_ref[...], (((1,), (1,)), ((), ())),
                                 preferred_element_type=F32)
        rankr_ref[...] = jnp.where(memb_r, rank_r, -1.0)
        for gi in range(N_GROUPS):
            cnt_ref[gi] = jnp.sum(ones_r[gi:gi + 1, :]).astype(jnp.int32)
        acc_ref[...] = jnp.zeros_like(acc_ref)

    nch = (cnt_ref[g] + (R - 1)) // R

    @pl.when(j == 0)
    def _():
        lane_t = lax.broadcasted_iota(jnp.int32, (T, LANES), 1)
        rc = jnp.sum(jnp.where(lane_t == g, rankc_ref[...], 0.0), axis=1, keepdims=True)
        col = lax.broadcasted_iota(jnp.int32, (T, R), 1).astype(F32)
        gt = gt_ref[...]
        g_hi = gt.astype(BF16)
        g_lo = (gt - g_hi.astype(F32)).astype(BF16)

        def gather(c, carry):
            pt = jnp.where(rc - (c * R).astype(F32) == col, 1.0, 0.0).astype(BF16)
            xg_ref[c] = jnp.dot(ht_ref[...], pt, preferred_element_type=F32).astype(BF16)
            gg_ref[c] = (jnp.dot(g_hi, pt, preferred_element_type=F32)
                         + jnp.dot(g_lo, pt, preferred_element_type=F32))
            yg_ref[c] = jnp.zeros((D_MODEL, R), F32)
            return carry

        lax.fori_loop(0, nch, gather, 0)

    def ffn(c, carry):
        xc = xg_ref[c]
        a = jnp.dot(wg_ref[0], xc, preferred_element_type=F32)
        u = jnp.dot(wu_ref[0], xc, preferred_element_type=F32)
        hid = (a * jax.nn.sigmoid(a)) * u * gg_ref[c, pl.ds(e, 1), :]
        yg_ref[c] += jnp.dot(wd_ref[0], hid.astype(BF16), preferred_element_type=F32)
        return carry

    lax.fori_loop(0, nch, ffn, 0)

    @pl.when(j == EXPERTS_PER_GROUP - 1)
    def _():
        rr = rankr_ref[pl.ds(g, 1), :]
        row = lax.broadcasted_iota(jnp.int32, (R, T), 0).astype(F32)

        def scatter(c, carry):
            p = jnp.where(rr - (c * R).astype(F32) == row, 1.0, 0.0).astype(BF16)
            acc_ref[...] += jnp.dot(yg_ref[c].astype(BF16), p, preferred_element_type=F32)
            return carry

        lax.fori_loop(0, nch, scatter, 0)

    @pl.when(e == N_EXPERTS - 1)
    def _():
        o_ref[...] = xn_ref[...] + mod_ref[0][5:6] * acc_ref[...].T


def out_proj_moe(a, b, x, mod, w_out_bf, g_ffn, router_w, router_b, wgt_bf, wut_bf, wdt_bf, seq, tm=1024):
    n = x.shape[0]
    tm = min(tm, seq)
    nch = tm // MOE_COLS
    half = a.shape[1]
    tok = lambda i, e: (i, 0)
    fixed = lambda i, e: (0, 0)
    once = dict(pipeline_mode=pl.Buffered(1))
    rw = jnp.pad(router_w.astype(F32), ((0, 0), (0, LANES - N_EXPERTS)))
    rw_hi = rw.astype(BF16)
    rw_cat = jnp.concatenate([rw_hi, (rw - rw_hi.astype(F32)).astype(BF16)], axis=1)
    tri = jnp.asarray(np.tril(np.ones((tm, tm), np.float32), -1), BF16)
    return pl.pallas_call(
        _moe_kernel,
        grid=(n // tm, N_EXPERTS),
        in_specs=[
            pl.BlockSpec((tm, half), tok),
            pl.BlockSpec((tm, half), tok),
            pl.BlockSpec((tm, D_MODEL), tok, **once),
            pl.BlockSpec((1, 6, D_MODEL), lambda i, e: ((i * tm) // seq, 0, 0)),
            pl.BlockSpec((half, D_MODEL), lambda i, e: (0, 0), **once),
            pl.BlockSpec((half, D_MODEL), lambda i, e: (1, 0), **once),
            pl.BlockSpec((1, D_MODEL), fixed),
            pl.BlockSpec((D_MODEL, 2 * LANES), fixed, **once),
            pl.BlockSpec((N_EXPERTS, 1), fixed),
            pl.BlockSpec((tm, tm), fixed, **once),
            pl.BlockSpec((1, D_FF, D_MODEL), lambda i, e: (e, 0, 0)),
            pl.BlockSpec((1, D_FF, D_MODEL), lambda i, e: (e, 0, 0)),
            pl.BlockSpec((1, D_MODEL, D_FF), lambda i, e: (e, 0, 0)),
        ],
        out_specs=pl.BlockSpec((tm, D_MODEL), tok),
        out_shape=jax.ShapeDtypeStruct((n, D_MODEL), F32),
        scratch_shapes=[pltpu.VMEM((tm, LANES), F32), pltpu.VMEM((LANES, tm), F32),
                        pltpu.VMEM((D_MODEL, tm), BF16), pltpu.VMEM((nch, D_MODEL, MOE_COLS), BF16),
                        pltpu.VMEM((nch, LANES, MOE_COLS), F32), pltpu.VMEM((nch, D_MODEL, MOE_COLS), F32),
                        pltpu.VMEM((D_MODEL, tm), F32), pltpu.VMEM((tm, D_MODEL), F32),
                        pltpu.VMEM((LANES, tm), F32), pltpu.SMEM((N_GROUPS,), jnp.int32)],
        compiler_params=_cparams(("parallel", "arbitrary")),
        name="out_proj_moe",
    )(a, b, x, mod, w_out_bf, w_out_bf, g_ffn.reshape(1, D_MODEL), rw_cat,
      router_b.astype(F32).reshape(N_EXPERTS, 1), tri, wgt_bf, wut_bf, wdt_bf)


def _in1_kernel(x_ref, mod_ref, g_ref, wut_ref, wz_ref, ut_o, hz_o):
    m = mod_ref[0]
    h = _norm_mod(x_ref[...], g_ref[...], m[0:1], m[1:2]).astype(BF16)
    ut_o[...] = lax.dot_general(wut_ref[...], h, (((1,), (1,)), ((), ())),
                                preferred_element_type=F32).astype(BF16)
    hz_o[...] = jnp.dot(h, wz_ref[...], preferred_element_type=F32).astype(BF16)


def in_proj1(x, mod, g, w_u_t_bf, w_z_bf, seq, tm=512):
    n = x.shape[0]
    fixed = lambda i: (0, 0)
    return pl.pallas_call(
        _in1_kernel,
        grid=(n // tm,),
        in_specs=[
            pl.BlockSpec((tm, D_MODEL), lambda i: (i, 0)),
            pl.BlockSpec((1, 6, D_MODEL), lambda i: ((i * tm) // seq, 0, 0)),
            pl.BlockSpec((1, D_MODEL), fixed),
            pl.BlockSpec(w_u_t_bf.shape, fixed),
            pl.BlockSpec(w_z_bf.shape, fixed),
        ],
        out_specs=[pl.BlockSpec((W_SSM, tm), lambda i: (0, i)),
                   pl.BlockSpec((tm, 3 * W_HYENA), lambda i: (i, 0))],
        out_shape=[jax.ShapeDtypeStruct((W_SSM, n), BF16), jax.ShapeDtypeStruct((n, 3 * W_HYENA), BF16)],
        compiler_params=_cparams(("parallel",)),
        name="in_proj1",
    )(x, mod, g.reshape(1, D_MODEL), w_u_t_bf, w_z_bf)


def s5_tables(lam_re, lam_im, log_dt, b_re, b_im, c_re, c_im, d_skip):
    T = S5_CHUNK
    f32 = F32
    lags = jnp.arange(T, dtype=f32)

    def disc(d):
        lr, li = lam_re[d].astype(f32), lam_im[d].astype(f32)
        dt = jnp.exp(log_dt[d].astype(f32))[:, None]
        mag = jnp.exp(lr * dt)
        ab_re, ab_im = mag * jnp.cos(li * dt), mag * jnp.sin(li * dt)
        den = lr * lr + li * li
        f_re = ((ab_re - 1.0) * lr + ab_im * li) / den
        f_im = (ab_im * lr - (ab_re - 1.0) * li) / den
        br, bi = b_re[d].astype(f32), b_im[d].astype(f32)
        bb_re = f_re[..., None] * br - f_im[..., None] * bi
        bb_im = f_re[..., None] * bi + f_im[..., None] * br

        def power(p):
            ang = li * dt
            mg = jnp.exp(p[:, None, None] * (lr * dt)[None])
            return mg * jnp.cos(p[:, None, None] * ang[None]), mg * jnp.sin(p[:, None, None] * ang[None])
        return bb_re, bb_im, c_re[d].astype(f32), c_im[d].astype(f32), power

    hp = dict(precision=HIGHEST)
    tabs = []
    for d in range(2):
        bb_re, bb_im, cr, ci, power = disc(d)
        pr, pi = power(lags)
        cb_rr = jnp.einsum('gon,lgn,gni->lgoi', cr, pr, bb_re, **hp)
        cb_ii = jnp.einsum('gon,lgn,gni->lgoi', cr, pi, bb_im, **hp)
        cb_ri = jnp.einsum('gon,lgn,gni->lgoi', ci, pr, bb_im, **hp)
        cb_ir = jnp.einsum('gon,lgn,gni->lgoi', ci, pi, bb_re, **hp)
        kern = cb_rr - cb_ii - cb_ri - cb_ir
        qp = (T - 1.0 - lags) if d == 0 else lags
        qr, qi = power(qp)
        wp_re = jnp.einsum('tgn,gni->gitn', qr, bb_re) - jnp.einsum('tgn,gni->gitn', qi, bb_im)
        wp_im = jnp.einsum('tgn,gni->gitn', qr, bb_im) + jnp.einsum('tgn,gni->gitn', qi, bb_re)
        rp = (lags + 1.0) if d == 0 else (T - lags)
        rr, ri = power(rp)
        m_re = jnp.einsum('gon,tgn->gnot', cr, rr) - jnp.einsum('gon,tgn->gnot', ci, ri)
        m_im = jnp.einsum('gon,tgn->gnot', cr, ri) + jnp.einsum('gon,tgn->gnot', ci, rr)
        levels = 2.0 ** jnp.arange(16, dtype=f32) * T
        ar, ai = power(levels)
        tabs.append((kern, wp_re, wp_im, m_re, -m_im, ar, ai))

    kf, kb = tabs[0][0], tabs[1][0]
    skip = jnp.eye(S5_GROUP, dtype=f32)[None] * d_skip.astype(f32).reshape(S5_GROUPS, S5_GROUP, 1)
    k0 = kf[0] + kb[0] + skip
    kk = jnp.concatenate([k0[None], kf[1:], jnp.zeros_like(k0)[None], kb[:0:-1]], axis=0)
    kk = jnp.transpose(kk, (1, 3, 2, 0))
    g_, i_, o_ = kk.shape[:3]
    kk = kk.reshape(g_, i_ * o_, 2 * T)
    w_state = jnp.concatenate([tabs[0][1], tabs[0][2], tabs[1][1], tabs[1][2]], axis=-1)
    wp = w_state.reshape(g_, i_ * T, 4 * S5_STATE).astype(BF16)
    wc = jnp.concatenate([tabs[0][3], tabs[0][4], tabs[1][3], tabs[1][4]], axis=1)
    wc = wc.reshape(g_, 4 * S5_STATE, o_ * T).astype(BF16)
    mult = []
    for d in range(2):
        ar, ai = tabs[d][5], tabs[d][6]
        mult += [jnp.concatenate([ar, ar], axis=-1), jnp.concatenate([-ai, ai], axis=-1)]
    scan_mult = jnp.transpose(jnp.stack(mult, axis=2), (1, 0, 2, 3))
    return kk, wp, wc, scan_mult


def _s5_kernel(*refs, chunks_per_seq):
    nt = len(chunks_per_seq)
    u_refs, (kk_ref, wp_ref, wc_ref, mult_ref) = refs[:nt], refs[nt:nt + 4]
    o_refs, w_ref = refs[nt + 4:2 * nt + 4], refs[2 * nt + 4]
    T = S5_CHUNK

    def build(i, carry):
        for o in range(S5_GROUP):
            row = kk_ref[0, pl.ds(i * S5_GROUP + o, 1), :]
            toe = pltpu.roll(jnp.broadcast_to(row, (T, 2 * T)), 0, axis=1, stride=1, stride_axis=0)
            w_ref[pl.ds(pl.multiple_of(i * T, T), T), o * T:(o + 1) * T] = toe[:, :T].astype(BF16)
        return carry

    lax.fori_loop(0, S5_GROUP, build, 0)
    for u_ref, o_ref, cps in zip(u_refs, o_refs, chunks_per_seq):
        _s5_apply(u_ref, o_ref, w_ref, wp_ref, wc_ref, mult_ref, cps)


def _s5_apply(u_ref, o_ref, w_ref, wp_ref, wc_ref, mult_ref, chunks_per_seq):
    T = S5_CHUNK
    ns = 2 * S5_STATE
    x = jnp.concatenate([u_ref[i] for i in range(S5_GROUP)], axis=1)
    y = jnp.dot(x, w_ref[...], preferred_element_type=F32)
    pst = jnp.dot(x, wp_ref[0], preferred_element_type=F32)
    nc = y.shape[0]
    cidx = lax.broadcasted_iota(jnp.int32, (nc, ns), 0) % chunks_per_seq

    def cmul(s, mre, mim):
        return s * mre + pltpu.roll(s, S5_STATE, axis=1) * mim

    def scan(p, d):
        s = p
        k, step = 0, 1
        while step < chunks_per_seq:
            mre = mult_ref[0, k, 2 * d:2 * d + 1, :]
            mim = mult_ref[0, k, 2 * d + 1:2 * d + 2, :]
            if d == 0:
                sh = jnp.where(cidx >= step, pltpu.roll(s, step, axis=0), 0.0)
            else:
                sh = jnp.where(cidx < chunks_per_seq - step, pltpu.roll(s, nc - step, axis=0), 0.0)
            s = s + cmul(sh, mre, mim)
            k, step = k + 1, step * 2
        if d == 0:
            return jnp.where(cidx >= 1, pltpu.roll(s, 1, axis=0), 0.0)
        return jnp.where(cidx < chunks_per_seq - 1, pltpu.roll(s, nc - 1, axis=0), 0.0)

    sf = scan(pst[:, :ns], 0)
    sb = scan(pst[:, ns:], 1)
    carry = jnp.concatenate([sf, sb], axis=1).astype(BF16)
    ytot = y + jnp.dot(carry, wc_ref[0], preferred_element_type=F32)
    for o in range(S5_GROUP):
        o_ref[o] = ytot[:, o * T:(o + 1) * T]


def s5_scan(uts, seqs, kk, wp, wc, scan_mult):
    T = S5_CHUNK
    u3 = [ut.reshape(W_SSM, ut.shape[1] // T, T) for ut in uts]
    blk = [pl.BlockSpec((S5_GROUP,) + u.shape[1:], lambda g: (g, 0, 0)) for u in u3]
    grp = lambda a: pl.BlockSpec((1,) + a.shape[1:], lambda g: (g,) + (0,) * (a.ndim - 1))
    outs = pl.pallas_call(
        functools.partial(_s5_kernel, chunks_per_seq=tuple(s // T for s in seqs)),
        grid=(S5_GROUPS,),
        in_specs=blk + [grp(kk), grp(wp), grp(wc), grp(scan_mult)],
        out_specs=blk,
        out_shape=[jax.ShapeDtypeStruct(u.shape, F32) for u in u3],
        scratch_shapes=[pltpu.VMEM((S5_GROUP * T, S5_GROUP * T), BF16)],
        compiler_params=_cparams(("parallel",)),
        name="s5_scan",
    )(*u3, kk, wp, wc, scan_mult)
    return [o.reshape(W_SSM, -1) for o in outs]


def _s5_post_kernel(yt_ref, w_ref, o_ref):
    y = yt_ref[...].T
    y = 0.5 * y * (1.0 + jnp.tanh(math.sqrt(2.0 / math.pi) * (y + 0.044715 * (y * y * y))))
    z = jnp.dot(y.astype(BF16), w_ref[...], preferred_element_type=F32)
    o_ref[...] = (y * jax.nn.sigmoid(z)).astype(BF16)


def s5_post(yt, glu_w_bf, tm=2048):
    n = yt.shape[1]
    return pl.pallas_call(
        _s5_post_kernel,
        grid=(n // tm,),
        in_specs=[pl.BlockSpec((W_SSM, tm), lambda i: (0, i)),
                  pl.BlockSpec((W_SSM, W_SSM), lambda i: (0, 0))],
        out_specs=pl.BlockSpec((tm, W_SSM), lambda i: (i, 0)),
        out_shape=jax.ShapeDtypeStruct((n, W_SSM), BF16),
        compiler_params=_cparams(("parallel",)),
        name="s5_post",
    )(yt, glu_w_bf)


SHORT_HALO = 16


def _short_kernel(prev_ref, cur_ref, next_ref, w_ref, b_ref, v_o, x1_o, x2_o, *, tiles_per_seq):
    i = pl.program_id(0)
    t = i % tiles_per_seq
    cur = cur_ref[...].astype(F32)
    tt = cur.shape[0]
    before = jnp.where(t == 0, 0.0, prev_ref[SHORT_HALO - 1:SHORT_HALO, :].astype(F32))
    after = jnp.where(t == tiles_per_seq - 1, 0.0, next_ref[0:1, :].astype(F32))
    ridx = lax.broadcasted_iota(jnp.int32, cur.shape, 0)
    left = jnp.where(ridx == 0, before, pltpu.roll(cur, 1, axis=0))
    right = jnp.where(ridx == tt - 1, after, pltpu.roll(cur, tt - 1, axis=0))
    w = w_ref[...]
    p = left * w[0:1] + cur * w[1:2] + right * w[2:3] + b_ref[...]
    v_o[...] = p[:, 0:W_HYENA].astype(BF16)
    x1_o[...] = p[:, W_HYENA:2 * W_HYENA].astype(BF16)
    x2_o[...] = p[:, 2 * W_HYENA:].astype(BF16)


def hyena_short_conv(hz, short_w, short_b, seq, tt=512):
    n, c3 = hz.shape
    tps = seq // tt
    hb = tt // SHORT_HALO
    nhb = n // SHORT_HALO
    fixed = lambda i: (0, 0)
    osd = jax.ShapeDtypeStruct((n, W_HYENA), BF16)
    return pl.pallas_call(
        functools.partial(_short_kernel, tiles_per_seq=tps),
        grid=(n // tt,),
        in_specs=[
            pl.BlockSpec((SHORT_HALO, c3), lambda i: (jnp.maximum(i * hb - 1, 0), 0)),
            pl.BlockSpec((tt, c3), lambda i: (i, 0)),
            pl.BlockSpec((SHORT_HALO, c3), lambda i: (jnp.minimum((i + 1) * hb, nhb - 1), 0)),
            pl.BlockSpec((3, c3), fixed),
            pl.BlockSpec((1, c3), fixed),
        ],
        out_specs=[pl.BlockSpec((tt, W_HYENA), lambda i: (i, 0))] * 3,
        out_shape=[osd] * 3,
        compiler_params=_cparams(("parallel",)),
        name="hyena_short_conv",
    )(hz, hz, hz, short_w.astype(F32), short_b.astype(F32).reshape(1, c3))


def _filter_kernel(band_ref, w1_ref, b1_ref, w2_ref, b2_ref, w3_ref, freq_ref, decay_ref, k_o, sum_o, *, l):
    i = pl.program_id(0)
    tr = k_o.shape[0]
    m = i * tr + lax.broadcasted_iota(jnp.int32, (tr, LANES), 0)
    pos = jnp.where(m > l, 2 * l - m, m).astype(F32)
    t = pos / (l - 1)
    ang = ((2.0 * math.pi / l) * pos) * band_ref[...]
    lane = lax.broadcasted_iota(jnp.int32, (tr, LANES), 1)
    feat = jnp.where(lane == 0, t, jnp.where(lane <= HY_BANDS, jnp.cos(ang),
                                              jnp.where(lane <= 2 * HY_BANDS, -jnp.sin(ang), 0.0)))
    fr = freq_ref[...]
    hdn = jnp.sin(fr * (_dot3(feat, w1_ref[...]) + b1_ref[...]))
    hdn = jnp.sin(fr * (_dot3(hdn, w2_ref[...]) + b2_ref[...]))
    f = _dot3(hdn, w3_ref[0])
    reps = f.shape[1] // LANES
    t_all = jnp.concatenate([t] * reps, axis=1)
    m_all = jnp.concatenate([m] * reps, axis=1)
    f = jnp.where(m_all == l, 0.0, f * jnp.exp(-t_all * jnp.abs(decay_ref[0])))
    k_o[...] = f.astype(BF16)

    @pl.when(i == 0)
    def _():
        sum_o[...] = jnp.zeros_like(sum_o)

    sum_o[...] += jnp.sum(jnp.abs(f), axis=0, keepdims=True)


def hyena_filters(l, w1, b1, w2, b2, w3, freq, decay, tr=512):
    f32 = F32
    n_ord = decay.shape[0]
    hid = w1.shape[1]
    cols = n_ord * W_HYENA
    bands = jnp.linspace(1e-4, HY_BANDS - 1, HY_BANDS, dtype=f32)
    band_row = jnp.zeros((1, LANES), f32).at[0, 1:1 + HY_BANDS].set(bands).at[0, 1 + HY_BANDS:1 + 2 * HY_BANDS].set(bands)
    pad2 = lambda a, r, c: jnp.pad(a.astype(f32), ((0, r - a.shape[0]), (0, c - a.shape[1])))
    row = lambda a: pad2(a.reshape(1, -1), 1, LANES)
    w3h = jnp.transpose(w3.astype(f32).reshape(hid, n_ord, 2, W_HYENA), (2, 0, 1, 3)).reshape(2, hid, cols)
    w3h = jnp.pad(w3h, ((0, 0), (0, LANES - hid), (0, 0)))
    dech = jnp.transpose(decay.astype(f32), (1, 0, 2)).reshape(2, 1, cols)
    nt = 2 * l // tr
    fixed = lambda i: (0, 0)
    half = lambda i: (i // (nt // 2), 0, 0)
    k_un, ksum = pl.pallas_call(
        functools.partial(_filter_kernel, l=l),
        grid=(nt,),
        in_specs=[pl.BlockSpec((1, LANES), fixed), pl.BlockSpec((LANES, LANES), fixed), pl.BlockSpec((1, LANES), fixed),
                  pl.BlockSpec((LANES, LANES), fixed), pl.BlockSpec((1, LANES), fixed),
                  pl.BlockSpec((1, LANES, cols), half), pl.BlockSpec((1, LANES), fixed),
                  pl.BlockSpec((1, 1, cols), half)],
        out_specs=[pl.BlockSpec((tr, cols), lambda i: (i, 0)), pl.BlockSpec((8, cols), fixed)],
        out_shape=[jax.ShapeDtypeStruct((2 * l, cols), BF16), jax.ShapeDtypeStruct((8, cols), f32)],
        compiler_params=_cparams(("arbitrary",)),
        name="hyena_filter",
    )(band_row, pad2(w1, LANES, LANES), row(b1), pad2(w2, LANES, LANES), row(b2), w3h, row(freq), dech)
    return k_un, 1.0 / ksum[0:1]


FFT_KB = 4


def _num_k1(n1):
    return n1 // 2 + FFT_KB


def _dft_tables(n1, rows_in):
    nk = _num_k1(n1)
    nh = n1 // 2
    k1 = np.arange(nk)[:, None].astype(np.float64)
    r = np.arange(rows_in)[None, :].astype(np.float64)
    ang = 2.0 * np.pi * k1 * r / n1
    keep = (k1 <= nh).astype(np.float64)
    fwd = np.concatenate([np.cos(ang) * keep, -np.sin(ang) * keep], axis=0)
    wgt = np.where(k1[:nh] == 0, 1.0, 2.0)
    inv = np.concatenate([(np.cos(ang[:nh]) * wgt).T, (-np.sin(ang[:nh]) * wgt).T], axis=1)
    sign = np.cos(np.pi * r).T
    return jnp.asarray(fwd, BF16), jnp.asarray(inv, BF16), jnp.asarray(sign, F32)


def _mid_tables(n1):
    n2 = FFT_N2
    n = n1 * n2
    k = (np.arange(_num_k1(n1))[:, None, None] + n1 * np.arange(n2)[None, :, None]).astype(np.float64)
    m = np.arange(n2)[None, None, :].astype(np.float64)
    ang = 2.0 * np.pi * ((k * m) % n) / n
    gr, gi = np.cos(ang), -np.sin(ang)
    fwd = np.concatenate([np.concatenate([gr, -gi], axis=2), np.concatenate([gi, gr], axis=2)], axis=1)
    hr, hi = np.transpose(gr, (0, 2, 1)), -np.transpose(gi, (0, 2, 1))
    inv = np.concatenate([np.concatenate([hr, -hi], axis=2), np.concatenate([hi, hr], axis=2)], axis=1)
    return jnp.asarray(fwd, BF16), jnp.asarray(inv, BF16)


FFT_MB = 16


def _stage1_kernel(z_ref, f_ref, a_ref):
    nk = a_ref.shape[-3]
    lead = (0,) * (len(a_ref.shape) - 5)
    x = pltpu.einshape("rmc->mrc", z_ref[0])
    res = jnp.stack([jnp.dot(f_ref[...], x[m], preferred_element_type=F32).astype(BF16)
                     for m in range(x.shape[0])])
    res = pltpu.einshape("mkc->kmc", res)
    a_ref[(0, 0) + lead] = res[:nk]
    a_ref[(0, 1) + lead] = res[nk:]


def fft_stage1(z4, fwd, cb=512, blocked=True):
    bsz, r, n2, c = z4.shape
    nk = fwd.shape[0] // 2
    if blocked:
        out_spec = pl.BlockSpec((1, 2, 1, nk, FFT_MB, cb), lambda b, j, q: (b, 0, j, 0, 0, q))
        out_shape = jax.ShapeDtypeStruct((bsz, 2, n2 // FFT_MB, nk, FFT_MB, c), BF16)
    else:
        out_spec = pl.BlockSpec((1, 2, nk, FFT_MB, cb), lambda b, j, q: (b, 0, 0, j, q))
        out_shape = jax.ShapeDtypeStruct((bsz, 2, nk, n2, c), BF16)
    return pl.pallas_call(
        _stage1_kernel,
        grid=(bsz, n2 // FFT_MB, c // cb),
        in_specs=[pl.BlockSpec((1, r, FFT_MB, cb), lambda b, j, q: (b, 0, j, q)),
                  pl.BlockSpec(fwd.shape, lambda b, j, q: (0, 0))],
        out_specs=out_spec,
        out_shape=out_shape,
        compiler_params=_cparams(("parallel", "parallel", "parallel")),
        name="fft_stage1",
    )(z4, fwd)


def _mid_kernel(a_ref, g_ref, h_ref, kr_ref, ki_ref, d_ref):
    n2 = FFT_N2
    c = a_ref.shape[-1]
    blk = d_ref.shape[2], d_ref.shape[4], c
    for j in range(a_ref.shape[3]):
        ab = jnp.concatenate([a_ref[0, 0, :, j].reshape(n2, c), a_ref[0, 1, :, j].reshape(n2, c)], axis=0)
        z = jnp.dot(g_ref[j], ab, preferred_element_type=F32)
        zr, zi = z[:n2], z[n2:]
        kr, ki = kr_ref[j], ki_ref[j]
        yb = jnp.concatenate([zr * kr - zi * ki, zr * ki + zi * kr], axis=0).astype(BF16)
        d = jnp.dot(h_ref[j], yb, preferred_element_type=F32)
        d_ref[0, 0, :, j] = d[:n2].astype(BF16).reshape(blk)
        d_ref[0, 1, :, j] = d[n2:].astype(BF16).reshape(blk)


def _mid_fwd_kernel(a_ref, g_ref, s_ref, zr_ref, zi_ref):
    n2 = FFT_N2
    for j in range(a_ref.shape[2]):
        ab = jnp.concatenate([a_ref[0, 0, j], a_ref[0, 1, j]], axis=0)
        z = jnp.dot(g_ref[j], ab, preferred_element_type=F32) * s_ref[...]
        zr_ref[j] = z[:n2]
        zi_ref[j] = z[n2:]


def fft_mid(a5, g, h, kf_re, kf_im, order, kb=FFT_KB):
    bsz, _, nmb, n1, mb, c = a5.shape
    n2 = nmb * mb
    kb = min(kb, n1)
    blk = pl.BlockSpec((1, 2, nmb, kb, mb, c), lambda k, b: (b, 0, 0, k, 0, 0))
    mat = pl.BlockSpec((kb, 2 * n2, 2 * n2), lambda k, b: (k, 0, 0))
    spec = pl.BlockSpec((kb, n2, c), lambda k, b: (k, 0, order))
    return pl.pallas_call(
        _mid_kernel,
        grid=(n1 // kb, bsz),
        in_specs=[blk, mat, mat, spec, spec],
        out_specs=blk,
        out_shape=jax.ShapeDtypeStruct(a5.shape, BF16),
        compiler_params=_cparams(("parallel", "arbitrary")),
        name="fft_mid",
    )(a5, g, h, kf_re, kf_im)


def fft_mid_fwd(a5, g, col_scale, kb=FFT_KB):
    _, _, n1, n2, c = a5.shape
    kb = min(kb, n1)
    spec = pl.BlockSpec((kb, n2, c), lambda k: (k, 0, 0))
    osd = jax.ShapeDtypeStruct((n1, n2, c), F32)
    return pl.pallas_call(
        _mid_fwd_kernel,
        grid=(n1 // kb,),
        in_specs=[pl.BlockSpec((1, 2, kb, n2, c), lambda k: (0, 0, k, 0, 0)),
                  pl.BlockSpec((kb, 2 * n2, 2 * n2), lambda k: (k, 0, 0)),
                  pl.BlockSpec((1, c), lambda k: (0, 0))],
        out_specs=[spec, spec],
        out_shape=[osd, osd],
        compiler_params=_cparams(("parallel",)),
        name="fft_mid_fwd",
    )(a5, g, col_scale)


def _fin_kernel(d_ref, inv_ref, sign_ref, z_ref, gate_ref, bias_ref, *rest, scale, chain):
    nk = d_ref.shape[3]
    nh = inv_ref.shape[1] // 2
    d_re = pltpu.einshape("kmc->mkc", d_ref[0, 0, 0])
    d_im = pltpu.einshape("kmc->mkc", d_ref[0, 1, 0])
    zt = pltpu.einshape("rmc->mrc", z_ref[0])
    gt = pltpu.einshape("rmc->mrc", gate_ref[0])
    z_new, a_new = [], []
    for m in range(zt.shape[0]):
        half = jnp.concatenate([d_re[m, :nh], d_im[m, :nh]], axis=0)
        nyq = d_re[m, nh:nh + 1].astype(F32)
        conv = (jnp.dot(inv_ref[...], half, preferred_element_type=F32) + sign_ref[...] * nyq) * scale
        zb = (gt[m].astype(F32) * (conv + zt[m].astype(F32) * bias_ref[...])).astype(BF16)
        z_new.append(zb)
        if chain:
            a_new.append(jnp.dot(rest[0][...], zb, preferred_element_type=F32).astype(BF16))
    if chain:
        _, z_o, a_o = rest
        a_all = pltpu.einshape("mkc->kmc", jnp.stack(a_new))
        a_o[0, 0, 0] = a_all[:nk]
        a_o[0, 1, 0] = a_all[nk:]
    else:
        (z_o,) = rest
    z_o[0] = pltpu.einshape("mrc->rmc", jnp.stack(z_new))


def fft_final(d5, inv, sign, z4, gate4, bias_row, scale, fwd=None):
    bsz, r, n2, c = z4.shape
    nk = d5.shape[3]
    chain = fwd is not None
    tok = pl.BlockSpec((1, r, FFT_MB, c), lambda b, j: (b, 0, j, 0))
    spec = pl.BlockSpec((1, 2, 1, nk, FFT_MB, c), lambda b, j: (b, 0, j, 0, 0, 0))
    fixed = lambda a: pl.BlockSpec(a.shape, lambda b, j: (0, 0))
    in_specs = [spec, fixed(inv), fixed(sign), tok, tok, fixed(bias_row)]
    out_specs = [tok]
    out_shape = [jax.ShapeDtypeStruct(z4.shape, BF16)]
    args = [d5, inv, sign, z4, gate4, bias_row]
    if chain:
        in_specs.append(fixed(fwd))
        out_specs.append(spec)
        out_shape.append(jax.ShapeDtypeStruct(d5.shape, BF16))
        args.append(fwd)
    return pl.pallas_call(
        functools.partial(_fin_kernel, scale=scale, chain=chain),
        grid=(bsz, n2 // FFT_MB),
        in_specs=in_specs,
        out_specs=out_specs,
        out_shape=out_shape,
        compiler_params=_cparams(("parallel", "parallel")),
        name="fft_final",
    )(*args)


def hyena_mixer(hz, bsz, seq, short_w, short_b, filt, filt_scale, bias):
    c = W_HYENA
    n2 = FFT_N2
    n = 2 * seq
    n1 = n // n2
    r = n1 // 2
    fwd_half, inv_half, sign_half = _dft_tables(n1, r)
    fwd_full = _dft_tables(n1, n1)[0]
    g, h = _mid_tables(n1)
    ka = fft_stage1(filt.reshape(1, n1, n2, 2 * c), fwd_full, blocked=False)
    kf_re, kf_im = fft_mid_fwd(ka, g, filt_scale)
    v, x1, x2 = hyena_short_conv(hz, short_w, short_b, seq)
    as4 = lambda t: t.reshape(bsz, r, n2, c)
    z4 = as4(v)
    a = fft_stage1(z4, fwd_half)
    for order, gate in enumerate((x1, x2)):
        d = fft_mid(a, g, h, kf_re, kf_im, order)
        bias_row = bias[order].astype(F32).reshape(1, c)
        res = fft_final(d, inv_half, sign_half, z4, as4(gate), bias_row, 1.0 / n,
                        fwd=fwd_half if order == 0 else None)
        if order == 0:
            z4, a = res
        else:
            (z4,) = res
    return z4.reshape(bsz * seq, c)


def _trunk_layer0(x, c, wts):
    bsz, seq, _ = x.shape
    n = bsz * seq
    mod = adaln_mod(c, wts['ada_w'], wts['ada_b'])
    xf = x.reshape(n, D_MODEL)
    q, k, v, u = in_proj0(xf, mod[0], wts['norm_mix_g'][0], wts['ab_w_in'], wts['na_q_g'], wts['na_k_g'], seq)
    att = neighbourhood_attention(q, k, v, wts['att_bias'], bsz, seq)
    cnv = conv_module(u, wts['cv_dw_w'], wts['cv_dw_b'], wts['cv_ln_g'], wts['cv_ln_b'], seq)
    xf = out_proj_moe(att, cnv, xf, mod[0], wts['ab_w_out'], wts['norm_ffn_g'][0], wts['router_w'],
                      wts['router_b'], wts['moe_wg'][0], wts['moe_wu'][0], wts['moe_wd'][0], seq)
    ut, hz = in_proj1(xf, mod[1], wts['norm_mix_g'][1], wts['cd_w_u_t'], wts['cd_w_z'], seq)
    return xf, mod, ut, hz


def _trunk_layer1(xf, mod, yt, hz, bsz, seq, wts):
    ssm = s5_post(yt, wts['s5_glu_w'])
    filt, filt_scale = hyena_filters(seq, *wts['hy_mlp'])
    hy = hyena_mixer(hz, bsz, seq, wts['hy_short_w'], wts['hy_short_b'], filt, filt_scale, wts['hy_bias'])
    xf = out_proj_moe(ssm, hy, xf, mod[1], wts['cd_w_out'], wts['norm_ffn_g'][1], wts['router_w'],
                      wts['router_b'], wts['moe_wg'][1], wts['moe_wu'][1], wts['moe_wd'][1], seq)
    return xf.reshape(bsz, seq, D_MODEL)


def kernel(x_prompt, x_sample, c_prompt, c_sample, ada_w, ada_b, norm_mix_g, norm_ffn_g, router_w, router_b, moe_w_gate, moe_w_up, moe_w_down, ab_w_in, ab_w_out, na_q_g, na_k_g, na_rpb, cv_dw_w, cv_dw_b, cv_ln_g, cv_ln_b, cd_w_in, cd_w_out, s5_lam_re, s5_lam_im, s5_log_dt, s5_b_re, s5_b_im, s5_c_re, s5_c_im, s5_d, s5_glu_w, hy_short_w, hy_short_b, hy_w1, hy_b1, hy_w2, hy_b2, hy_w3, hy_freq, hy_decay, hy_bias):
    s5_tabs = s5_tables(s5_lam_re[0], s5_lam_im[0], s5_log_dt[0], s5_b_re[0], s5_b_im[0],
                        s5_c_re[0], s5_c_im[0], s5_d[0])
    wts = dict(
        ada_w=ada_w, ada_b=ada_b, norm_mix_g=norm_mix_g.astype(F32), norm_ffn_g=norm_ffn_g.astype(F32),
        router_w=router_w, router_b=router_b,
        moe_wg=jnp.swapaxes(moe_w_gate, -1, -2).astype(BF16), moe_wu=jnp.swapaxes(moe_w_up, -1, -2).astype(BF16),
        moe_wd=jnp.swapaxes(moe_w_down, -1, -2).astype(BF16),
        ab_w_in=ab_w_in[0].astype(BF16), ab_w_out=ab_w_out[0].astype(BF16),
        na_q_g=na_q_g[0], na_k_g=na_k_g[0], att_bias=_att_bias_table(na_rpb[0]),
        cv_dw_w=cv_dw_w[0], cv_dw_b=cv_dw_b[0], cv_ln_g=cv_ln_g[0], cv_ln_b=cv_ln_b[0],
        cd_w_u_t=cd_w_in[0][:, :W_SSM].T.astype(BF16), cd_w_z=cd_w_in[0][:, W_SSM:].astype(BF16),
        cd_w_out=cd_w_out[0].astype(BF16),
        s5_glu_w=s5_glu_w[0].astype(BF16),
        hy_short_w=hy_short_w[0], hy_short_b=hy_short_b[0],
        hy_mlp=(hy_w1[0], hy_b1[0], hy_w2[0], hy_b2[0], hy_w3[0], hy_freq[0], hy_decay[0]),
        hy_bias=hy_bias[0],
    )
    xs = (x_prompt, x_sample)
    mids = [_trunk_layer0(x, c, wts) for x, c in zip(xs, (c_prompt, c_sample))]
    yts = s5_scan([m[2] for m in mids], [x.shape[1] for x in xs], *s5_tabs)
    return tuple(_trunk_layer1(m[0], m[1], yt, m[3], x.shape[0], x.shape[1], wts)
                 for m, yt, x in zip(mids, yts, xs))
```
